```python
import math
import jax
import jax.numpy as jnp
from jax import lax
import numpy as np

D_MODEL = 1024
BATCH = 8
SEQ = 2048
DEPTH = 2
DEC_BATCH = 32
DEC_SEQ = 4
PAST_LEN = 16384
PAGE_SIZE = 128

CONV_WIDTH = 4
SSD_HEADS = 16
SSD_HEAD_DIM = 64
SSD_WIDTH = SSD_HEADS * SSD_HEAD_DIM
SSD_GROUPS = 2
SSD_STATE = 128
SSD_CONV_DIM = SSD_WIDTH + 2 * SSD_GROUPS * SSD_STATE
SSD_CHUNK = 128
MLA_HEADS = 8
MLA_NOPE = 64
MLA_ROPE = 32
MLA_V = 64
MLA_WIDTH = MLA_HEADS * MLA_V
MLA_Q_RANK = 384
MLA_KV_RANK = 256
MLA_SCALE = (MLA_NOPE + MLA_ROPE) ** -0.5
ROPE_THETA = 10000.0
Q_BLOCK = 128
GDN_HEADS = 4
GDN_HEAD_DIM = 128
GDN_WIDTH = GDN_HEADS * GDN_HEAD_DIM
GDN_CONV_DIM = 3 * GDN_WIDTH
GDN_CHUNK = 64

MIX_WIDTH = SSD_WIDTH + MLA_WIDTH + GDN_WIDTH
IN_SIZES = (SSD_WIDTH, SSD_CONV_DIM, SSD_HEADS,
            MLA_Q_RANK, MLA_KV_RANK, MLA_ROPE, MLA_WIDTH,
            GDN_CONV_DIM, GDN_WIDTH, GDN_HEADS, GDN_HEADS)
IN_WIDTH = sum(IN_SIZES)
IN_OFFSETS = tuple(int(o) for o in np.cumsum(IN_SIZES)[:-1])

DEEPNORM_ALPHA = (2 * DEPTH) ** 0.25
DEEPNORM_BETA = (8 * DEPTH) ** -0.25
LN_EPS = 1e-5
RMS_EPS = 1e-6
L2_EPS = 1e-6

kernel_name = 'hymba_ssd_mla_gdn_deepnorm_step'


def layer_norm(x, g, b):
    xf = x.astype(jnp.float32)
    mu = jnp.mean(xf, -1, keepdims=True)
    var = jnp.mean(jnp.square(xf - mu), -1, keepdims=True)
    return ((xf - mu) * lax.rsqrt(var + LN_EPS) * g + b).astype(x.dtype)


def rms_norm(x, g):
    xf = x.astype(jnp.float32)
    return (xf * lax.rsqrt(jnp.mean(xf * xf, -1, keepdims=True) + RMS_EPS) * g).astype(x.dtype)


def l2_normalize(x):
    return x * lax.rsqrt(jnp.sum(x * x, -1, keepdims=True) + L2_EPS)


def causal_conv(x, prev, w):
    xp = jnp.concatenate([prev.astype(x.dtype), x], axis=1)
    y = lax.conv_general_dilated(xp, w[:, None, :].astype(x.dtype), (1,), 'VALID',
                                 dimension_numbers=('NWC', 'WIO', 'NWC'),
                                 feature_group_count=x.shape[-1])
    return y, xp[:, xp.shape[1] - (CONV_WIDTH - 1):]


def rope(x, pos):
    half = x.shape[-1] // 2
    inv = ROPE_THETA ** (-jnp.arange(half, dtype=jnp.float32) / half)
    ang = pos.astype(jnp.float32)[:, None] * inv[None, :]
    cos, sin = jnp.cos(ang)[:, None, :], jnp.sin(ang)[:, None, :]
    xf = x.astype(jnp.float32)
    x1, x2 = xf[..., :half], xf[..., half:]
    return jnp.concatenate([x1 * cos - x2 * sin, x2 * cos + x1 * sin], -1).astype(x.dtype)


def chunk_len(t, c):
    return c if t % c == 0 else t


def to_chunks(a, L):
    b, t = a.shape[:2]
    return jnp.moveaxis(a.reshape(b, t // L, L, *a.shape[2:]), 1, 0)


def from_chunks(a):
    nc, b, L = a.shape[:3]
    return jnp.moveaxis(a, 0, 1).reshape(b, nc * L, *a.shape[3:])


def ssd_scan(x, dt, a, bm, cm, h0):
    L = chunk_len(x.shape[1], SSD_CHUNK)
    causal = jnp.tril(jnp.ones((L, L), bool))

    def step(h, inp):
        xc, dtc, bc, cc = inp
        acum = jnp.cumsum(dtc * a, axis=1)
        seg = acum[:, :, None, :] - acum[:, None, :, :]
        decay = jnp.exp(jnp.where(causal[None, :, :, None], seg, -jnp.inf))
        xdt = xc * dtc[..., None]
        scores = jnp.einsum('bthn,bshn->btsh', cc, bc) * decay
        y_in = jnp.einsum('btsh,bshp->bthp', scores, xdt)
        y_st = jnp.einsum('bthn,bhpn->bthp', cc, h) * jnp.exp(acum)[..., None]
        last = acum[:, -1]
        wdec = jnp.exp(last[:, None, :] - acum)
        h_new = h * jnp.exp(last)[:, :, None, None] + jnp.einsum('bshn,bshp->bhpn', bc * wdec[..., None], xdt)
        return h_new, y_in + y_st

    h_fin, ys = lax.scan(step, h0, (to_chunks(x, L), to_chunks(dt, L), to_chunks(bm, L), to_chunks(cm, L)))
    return from_chunks(ys), h_fin


def gdn_scan(q, k, v, g, beta, s0):
    L = chunk_len(q.shape[1], GDN_CHUNK)
    incl = jnp.tril(jnp.ones((L, L), bool))
    strict = jnp.tril(jnp.ones((L, L), bool), -1)
    eye = jnp.eye(L, dtype=jnp.float32)

    def step(s, inp):
        qc, kc, vc, gc, bc = inp
        gcum = jnp.cumsum(gc, axis=1)
        gh = jnp.swapaxes(gcum, 1, 2)
        diff = gh[..., :, None] - gh[..., None, :]
        dec = jnp.exp(jnp.where(incl, diff, -jnp.inf))
        kb = kc * bc[..., None]
        amat = jnp.where(strict, jnp.einsum('bthd,bshd->bhts', kb, kc) * dec, 0.0)
        tmat = lax.linalg.triangular_solve(amat + eye, jnp.broadcast_to(eye, amat.shape),
                                           left_side=True, lower=True)
        u = jnp.einsum('bhts,bshe->bthe', tmat, vc * bc[..., None])
        w = jnp.einsum('bhts,bshd->bthd', tmat, kb * jnp.exp(gcum)[..., None])
        v_new = u - jnp.einsum('bthd,bhde->bthe', w, s)
        attn = jnp.where(incl, jnp.einsum('bthd,bshd->bhts', qc, kc) * dec, 0.0)
        o = (jnp.einsum('bthd,bhde->bthe', qc * jnp.exp(gcum)[..., None], s)
             + jnp.einsum('bhts,bshe->bthe', attn, v_new))
        last = gcum[:, -1]
        kd = kc * jnp.exp(last[:, None, :] - gcum)[..., None]
        s_new = s * jnp.exp(last)[..., None, None] + jnp.einsum('bshd,bshe->bhde', kd, v_new)
        return s_new, o

    s_fin, os_ = lax.scan(step, s0, (to_chunks(q, L), to_chunks(k, L), to_chunks(v, L),
                                     to_chunks(g, L), to_chunks(beta, L)))
    return from_chunks(os_), s_fin


def mla_attention(q_lat, q_rope, kv_lat, k_rope, q_pos):
    nt = q_lat.shape[1]
    k_pos = jnp.arange(kv_lat.shape[1])

    def attend(blk):
        ql, qr, qp = blk
        s = (jnp.einsum('bthr,bsr->bhts', ql, kv_lat)
             + jnp.einsum('bthd,bsd->bhts', qr, k_rope)).astype(jnp.float32) * MLA_SCALE
        s = jnp.where(k_pos[None, :] <= qp[:, None], s, -jnp.inf)
        p = jax.nn.softmax(s, axis=-1).astype(kv_lat.dtype)
        return jnp.einsum('bhts,bsr->bthr', p, kv_lat)

    if nt > Q_BLOCK and nt % Q_BLOCK == 0:
        blocks = (to_chunks(q_lat, Q_BLOCK), to_chunks(q_rope, Q_BLOCK), q_pos.reshape(nt // Q_BLOCK, Q_BLOCK))
        return from_chunks(lax.map(attend, blocks))
    return attend((q_lat, q_rope, q_pos))


def mixer_layer(x, pos, past_lat, past_rope, ssd_conv_prev, ssd_prev, gdn_conv_prev, gdn_prev,
                w_in, ssd_conv_w, ssd_conv_b, ssd_dt_bias, ssd_a_log, ssd_d, ssd_norm_w,
                mla_q_norm_w, mla_w_uq, mla_kv_norm_w, mla_w_uk, mla_w_uv,
                gdn_conv_w, gdn_dt_bias, gdn_a_log, gdn_norm_w, w_out, ln_g, ln_b):
    nb, nt, _ = x.shape
    f32 = jnp.float32
    proj = x @ w_in
    (ssd_z, ssd_xbc, ssd_dt, mla_cq, mla_ckv, mla_kr, mla_gate,
     gdn_qkv, gdn_z, gdn_b, gdn_a) = jnp.split(proj, IN_OFFSETS, axis=-1)

    xbc, ssd_conv_new = causal_conv(ssd_xbc, ssd_conv_prev, ssd_conv_w)
    xbc = jax.nn.silu(xbc + ssd_conv_b)
    xs, bs, cs = jnp.split(xbc, (SSD_WIDTH, SSD_WIDTH + SSD_GROUPS * SSD_STATE), axis=-1)
    xh = xs.reshape(nb, nt, SSD_HEADS, SSD_HEAD_DIM).astype(f32)
    rep = SSD_HEADS // SSD_GROUPS
    bh = jnp.repeat(bs.reshape(nb, nt, SSD_GROUPS, SSD_STATE).astype(f32), rep, axis=2)
    ch = jnp.repeat(cs.reshape(nb, nt, SSD_GROUPS, SSD_STATE).astype(f32), rep, axis=2)
    dt = jax.nn.softplus(ssd_dt.astype(f32) + ssd_dt_bias)
    a = -jnp.exp(ssd_a_log.astype(f32))
    y, ssd_new = ssd_scan(xh, dt, a, bh, ch, ssd_prev.astype(f32))
    y = y + ssd_d.astype(f32)[:, None] * xh
    y = (y.reshape(nb, nt, SSD_WIDTH) * jax.nn.silu(ssd_z.astype(f32))).reshape(nb, nt, SSD_GROUPS, -1)
    y_ssd = rms_norm(y, ssd_norm_w.reshape(SSD_GROUPS, -1)).reshape(nb, nt, SSD_WIDTH).astype(x.dtype)

    cq = rms_norm(mla_cq, mla_q_norm_w)
    q = (cq @ mla_w_uq).reshape(nb, nt, MLA_HEADS, MLA_NOPE + MLA_ROPE)
    q_nope, q_rope = q[..., :MLA_NOPE], rope(q[..., MLA_NOPE:], pos)
    ckv = rms_norm(mla_ckv, mla_kv_norm_w)
    kr = rope(mla_kr[:, :, None, :], pos)[:, :, 0]
    q_lat = jnp.einsum('bthd,rhd->bthr', q_nope, mla_w_uk)
    keys_lat = jnp.concatenate([past_lat.astype(x.dtype), ckv], axis=1)
    keys_rope = jnp.concatenate([past_rope.astype(x.dtype), kr], axis=1)
    o_lat = mla_attention(q_lat, q_rope, keys_lat, keys_rope, pos)
    o = jnp.einsum('bthr,rhd->bthd', o_lat, mla_w_uv).reshape(nb, nt, MLA_WIDTH)
    y_mla = o * jax.nn.silu(mla_gate)

    qkv, gdn_conv_new = causal_conv(gdn_qkv, gdn_conv_prev, gdn_conv_w)
    qkv = jax.nn.silu(qkv).astype(f32).reshape(nb, nt, 3, GDN_HEADS, GDN_HEAD_DIM)
    gq = l2_normalize(qkv[:, :, 0]) * GDN_HEAD_DIM ** -0.5
    gk = l2_normalize(qkv[:, :, 1])
    gv = qkv[:, :, 2]
    beta = jax.nn.sigmoid(gdn_b.astype(f32))
    g = -jnp.exp(gdn_a_log.astype(f32)) * jax.nn.softplus(gdn_a.astype(f32) + gdn_dt_bias)
    go, gdn_new = gdn_scan(gq, gk, gv, g, beta, gdn_prev.astype(f32))
    go = rms_norm(go, gdn_norm_w) * jax.nn.silu(gdn_z.astype(f32).reshape(nb, nt, GDN_HEADS, GDN_HEAD_DIM))
    y_gdn = go.reshape(nb, nt, GDN_WIDTH).astype(x.dtype)

    mix = jnp.concatenate([y_ssd, y_mla, y_gdn], axis=-1)
    x_new = layer_norm(DEEPNORM_ALPHA * x + mix @ w_out, ln_g, ln_b)
    return x_new, (ckv, kr, ssd_conv_new, ssd_new.astype(x.dtype), gdn_conv_new, gdn_new.astype(x.dtype))


def trunk(x, pos, past_lat, past_rope, ssd_conv, ssd_state, gdn_conv, gdn_state,
          emb_ln_g, emb_ln_b, layer_weights):
    h = layer_norm(x, emb_ln_g, emb_ln_b)
    new = []
    for l in range(DEPTH):
        h, st = mixer_layer(h, pos, past_lat[l], past_rope[l], ssd_conv[l], ssd_state[l],
                            gdn_conv[l], gdn_state[l], *[w[l] for w in layer_weights])
        new.append(st)
    return h, tuple(jnp.stack(s) for s in zip(*new))


def setup_inputs(seed: int = 0) -> dict:
    key = jax.random.key(seed)
    ks = list(jax.random.split(key, 32))
    f32 = jnp.float32

    def nrm(i, shape, scale):
        return jax.random.normal(ks[i], shape, f32) * scale

    def gain(i, n):
        return 1.0 + nrm(i, (DEPTH, n), 0.02)

    def dt_bias(i, n):
        dt = jnp.exp(jax.random.uniform(ks[i], (DEPTH, n), f32, math.log(1e-3), math.log(1e-1)))
        return dt + jnp.log(-jnp.expm1(-dt))

    def a_log(i, n):
        return jnp.log(jax.random.uniform(ks[i], (DEPTH, n), f32, 1.0, 16.0))

    n_pages = PAST_LEN // PAGE_SIZE
    n_used = DEC_BATCH * n_pages
    n_phys = n_used + max(1, n_used // 4)
    page_table = jax.random.permutation(ks[0], n_phys)[:n_used].reshape(DEC_BATCH, n_pages).astype(jnp.int32)
    return {
        'x_prompt': nrm(1, (BATCH, SEQ, D_MODEL), 1.0),
        'x_sample': nrm(2, (DEC_BATCH, DEC_SEQ, D_MODEL), 1.0),
        'cache_kv_latent': nrm(3, (DEPTH, n_phys, PAGE_SIZE, MLA_KV_RANK), 1.0),
        'cache_k_rope': nrm(4, (DEPTH, n_phys, PAGE_SIZE, MLA_ROPE), 1.0),
        'state_ssd_conv': nrm(5, (DEPTH, DEC_BATCH, CONV_WIDTH - 1, SSD_CONV_DIM), 1.0),
        'state_ssd': nrm(6, (DEPTH, DEC_BATCH, SSD_HEADS, SSD_HEAD_DIM, SSD_STATE), 0.1),
        'state_gdn_conv': nrm(7, (DEPTH, DEC_BATCH, CONV_WIDTH - 1, GDN_CONV_DIM), 1.0),
        'state_gdn': nrm(8, (DEPTH, DEC_BATCH, GDN_HEADS, GDN_HEAD_DIM, GDN_HEAD_DIM), 0.1),
        'page_table': page_table,
        'emb_ln_g': 1.0 + nrm(9, (D_MODEL,), 0.02),
        'emb_ln_b': nrm(10, (D_MODEL,), 0.02),
        'w_in': nrm(11, (DEPTH, D_MODEL, IN_WIDTH), D_MODEL ** -0.5),
        'ssd_conv_w': nrm(12, (DEPTH, CONV_WIDTH, SSD_CONV_DIM), CONV_WIDTH ** -0.5),
        'ssd_conv_b': nrm(13, (DEPTH, SSD_CONV_DIM), 0.02),
        'ssd_dt_bias': dt_bias(14, SSD_HEADS),
        'ssd_a_log': a_log(15, SSD_HEADS),
        'ssd_d': gain(16, SSD_HEADS),
        'ssd_norm_w': gain(17, SSD_WIDTH),
        'mla_q_norm_w': gain(18, MLA_Q_RANK),
        'mla_w_uq': nrm(19, (DEPTH, MLA_Q_RANK, MLA_HEADS * (MLA_NOPE + MLA_ROPE)), MLA_Q_RANK ** -0.5),
        'mla_kv_norm_w': gain(20, MLA_KV_RANK),
        'mla_w_uk': nrm(21, (DEPTH, MLA_KV_RANK, MLA_HEADS, MLA_NOPE), MLA_KV_RANK ** -0.5),
        'mla_w_uv': nrm(22, (DEPTH, MLA_KV_RANK, MLA_HEADS, MLA_V), MLA_KV_RANK ** -0.5),
        'gdn_conv_w': nrm(23, (DEPTH, CONV_WIDTH, GDN_CONV_DIM), CONV_WIDTH ** -0.5),
        'gdn_dt_bias': dt_bias(24, GDN_HEADS),
        'gdn_a_log': a_log(25, GDN_HEADS),
        'gdn_norm_w': gain(26, GDN_HEAD_DIM),
        'w_out': nrm(27, (DEPTH, MIX_WIDTH, D_MODEL), DEEPNORM_BETA * MIX_WIDTH ** -0.5),
        'ln_g': gain(28, D_MODEL),
        'ln_b': nrm(29, (DEPTH, D_MODEL), 0.02),
    }


def reference(x_prompt, x_sample, cache_kv_latent, cache_k_rope, state_ssd_conv, state_ssd,
              state_gdn_conv, state_gdn, page_table, emb_ln_g, emb_ln_b, w_in, ssd_conv_w, ssd_conv_b,
              ssd_dt_bias, ssd_a_log, ssd_d, ssd_norm_w, mla_q_norm_w, mla_w_uq, mla_kv_norm_w,
              mla_w_uk, mla_w_uv, gdn_conv_w, gdn_dt_bias, gdn_a_log, gdn_norm_w, w_out, ln_g, ln_b):
    layer_weights = (w_in, ssd_conv_w, ssd_conv_b, ssd_dt_bias, ssd_a_log, ssd_d, ssd_norm_w,
                     mla_q_norm_w, mla_w_uq, mla_kv_norm_w, mla_w_uk, mla_w_uv,
                     gdn_conv_w, gdn_dt_bias, gdn_a_log, gdn_norm_w, w_out, ln_g, ln_b)
    dtype = x_prompt.dtype

    bp, tp, _ = x_prompt.shape
    pos_p = jnp.arange(tp)
    y_prompt, (p_lat, p_rope, p_ssd_conv, p_ssd, p_gdn_conv, p_gdn) = trunk(
        x_prompt, pos_p,
        jnp.zeros((DEPTH, bp, 0, MLA_KV_RANK), dtype), jnp.zeros((DEPTH, bp, 0, MLA_ROPE), dtype),
        jnp.zeros((DEPTH, bp, CONV_WIDTH - 1, SSD_CONV_DIM), dtype),
        jnp.zeros((DEPTH, bp, SSD_HEADS, SSD_HEAD_DIM, SSD_STATE), dtype),
        jnp.zeros((DEPTH, bp, CONV_WIDTH - 1, GDN_CONV_DIM), dtype),
        jnp.zeros((DEPTH, bp, GDN_HEADS, GDN_HEAD_DIM, GDN_HEAD_DIM), dtype),
        emb_ln_g, emb_ln_b, layer_weights)

    bs, ts, _ = x_sample.shape
    past_len = page_table.shape[1] * PAGE_SIZE
    pos_s = past_len + jnp.arange(ts)
    past_lat = [cache_kv_latent[l][page_table].reshape(bs, past_len, MLA_KV_RANK) for l in range(DEPTH)]
    past_rope = [cache_k_rope[l][page_table].reshape(bs, past_len, MLA_ROPE) for l in range(DEPTH)]
    y_sample, (s_lat, s_rope, s_ssd_conv, s_ssd, s_gdn_conv, s_gdn) = trunk(
        x_sample, pos_s, past_lat, past_rope, state_ssd_conv, state_ssd, state_gdn_conv, state_gdn,
        emb_ln_g, emb_ln_b, layer_weights)

    return (y_prompt, y_sample, p_lat, p_rope, p_ssd_conv, p_ssd, p_gdn_conv, p_gdn,
            s_lat, s_rope, s_ssd_conv, s_ssd, s_gdn_conv, s_gdn)
```

```python
import functools
import math

import jax
import jax.numpy as jnp
import numpy as np
from jax import lax
from jax.experimental import pallas as pl
from jax.experimental.pallas import tpu as pltpu

F32 = jnp.float32
BF16 = jnp.bfloat16

D_MODEL = 1024
CONV_WIDTH = 4
SSD_HEADS = 16
SSD_HEAD_DIM = 64
SSD_WIDTH = SSD_HEADS * SSD_HEAD_DIM
SSD_GROUPS = 2
SSD_STATE = 128
SSD_CONV_DIM = SSD_WIDTH + 2 * SSD_GROUPS * SSD_STATE
SSD_CHUNK = 128
MLA_HEADS = 8
MLA_NOPE = 64
MLA_ROPE = 32
MLA_V = 64
MLA_WIDTH = MLA_HEADS * MLA_V
MLA_Q_RANK = 384
MLA_KV_RANK = 256
MLA_SCALE = (MLA_NOPE + MLA_ROPE) ** -0.5
ROPE_THETA = 10000.0
GDN_HEADS = 4
GDN_HEAD_DIM = 128
GDN_WIDTH = GDN_HEADS * GDN_HEAD_DIM
GDN_CONV_DIM = 3 * GDN_WIDTH
GDN_CHUNK = 64
MIX_WIDTH = SSD_WIDTH + MLA_WIDTH + GDN_WIDTH
IN_SIZES = (SSD_WIDTH, SSD_CONV_DIM, SSD_HEADS, MLA_Q_RANK, MLA_KV_RANK, MLA_ROPE, MLA_WIDTH,
            GDN_CONV_DIM, GDN_WIDTH, GDN_HEADS, GDN_HEADS)
IN_OFFSETS = tuple(int(o) for o in np.cumsum(IN_SIZES)[:-1])
DEPTH = 2
DEEPNORM_ALPHA = (2 * DEPTH) ** 0.25
LN_EPS = 1e-5
RMS_EPS = 1e-6
L2_EPS = 1e-6
PAGE_SIZE = 128

LANES = 128
SUBLANES = 8
ROPE_HALF = MLA_ROPE // 2
P_SSD = SSD_WIDTH + SSD_CONV_DIM + LANES
P_MLA = MLA_Q_RANK + MLA_KV_RANK + 2 * LANES + MLA_WIDTH
P_GDN = GDN_CONV_DIM + GDN_WIDTH + LANES
QK_WIDTH = MLA_KV_RANK + 2 * LANES
VMEM_LIMIT = 56 * 1024 * 1024


def _cparams(sem):
    return pltpu.CompilerParams(dimension_semantics=sem, vmem_limit_bytes=VMEM_LIMIT)


def _bdot(a, b):
    return jnp.dot(a.astype(BF16), b.astype(BF16), preferred_element_type=F32)


def _bdot_nt(a, b):
    return lax.dot_general(a.astype(BF16), b.astype(BF16), (((1,), (1,)), ((), ())),
                           preferred_element_type=F32)


def _fdot(a, b):
    return jnp.dot(a, b, precision=lax.Precision.HIGHEST, preferred_element_type=F32)


def _silu(x):
    return x * (1.0 / (1.0 + jnp.exp(-x)))


def _softplus(x):
    return jnp.maximum(x, 0.0) + jnp.log1p(jnp.exp(-jnp.abs(x)))


def _const_spec(shape):
    nd = len(shape)
    return pl.BlockSpec(shape, lambda *_: (0,) * nd)


def _ln_kernel(x_ref, g_ref, b_ref, o_ref):
    x = x_ref[...]
    mu = jnp.mean(x, -1, keepdims=True)
    xc = x - mu
    var = jnp.mean(xc * xc, -1, keepdims=True)
    o_ref[...] = xc * lax.rsqrt(var + LN_EPS) * g_ref[...] + b_ref[...]


def _layer_norm(x, g, b, tm):
    m, d = x.shape
    return pl.pallas_call(
        _ln_kernel, grid=(m // tm,),
        in_specs=[pl.BlockSpec((tm, d), lambda i: (i, 0)), _const_spec((1, d)), _const_spec((1, d))],
        out_specs=pl.BlockSpec((tm, d), lambda i: (i, 0)),
        out_shape=jax.ShapeDtypeStruct((m, d), F32),
        compiler_params=_cparams(("parallel",)), name="emb_ln")(x, g.reshape(1, d), b.reshape(1, d))


def _proj_kernel(x_ref, w_ref, o_ref):
    o_ref[...] = _bdot(x_ref[...], w_ref[...])


def _project(x, w, tm, name):
    m, k = x.shape
    n = w.shape[1]
    return pl.pallas_call(
        _proj_kernel, grid=(m // tm,),
        in_specs=[pl.BlockSpec((tm, k), lambda i: (i, 0)), _const_spec((k, n))],
        out_specs=pl.BlockSpec((tm, n), lambda i: (i, 0)),
        out_shape=jax.ShapeDtypeStruct((m, n), F32),
        compiler_params=_cparams(("parallel",)), name=name)(x, w)


def _conv_chunk(xbuf, x_new, cw_ref, c, L):
    xbuf[SUBLANES:SUBLANES + L, :] = x_new
    y = cw_ref[CONV_WIDTH - 1:CONV_WIDTH, :] * x_new
    for k in range(CONV_WIDTH - 1):
        off = SUBLANES - (CONV_WIDTH - 1) + k
        y = y + cw_ref[k:k + 1, :] * xbuf[off:off + L, :]
    return y


def _ssd_kernel(p_ref, cprev_ref, h0_ref, cw_ref, cb_ref, dtb_ref, alog_ref, dexp_ref, nw_ref, e_ref,
                y_ref, cnew_ref, hout_ref, xbuf, ht, *, L, tv):
    c = pl.program_id(1)
    nc = pl.num_programs(1)

    @pl.when(c == 0)
    def _():
        xbuf[0:SUBLANES, :] = cprev_ref[0]
        ht[...] = h0_ref[0]

    z = p_ref[0, :, 0:SSD_WIDTH]
    xbc_raw = p_ref[0, :, SSD_WIDTH:SSD_WIDTH + SSD_CONV_DIM]
    dt_raw = p_ref[0, :, SSD_WIDTH + SSD_CONV_DIM:P_SSD]

    xbc = _silu(_conv_chunk(xbuf, xbc_raw, cw_ref, c, L) + cb_ref[...])
    cnew_ref[0] = xbuf[tv:tv + SUBLANES, :]
    xbuf[0:SUBLANES, :] = xbuf[L:L + SUBLANES, :]

    xs = xbc[:, 0:SSD_WIDTH]
    gs = SSD_GROUPS * SSD_STATE
    bm = xbc[:, SSD_WIDTH:SSD_WIDTH + gs]
    cm = xbc[:, SSD_WIDTH + gs:SSD_WIDTH + 2 * gs]

    dt = _softplus(dt_raw + dtb_ref[...])
    if tv < L:
        rows = lax.broadcasted_iota(jnp.int32, (L, LANES), 0)
        dt = jnp.where(rows < tv, dt, 0.0)
    a = -jnp.exp(alog_ref[...])
    da = dt * a
    r_i = lax.broadcasted_iota(jnp.int32, (L, L), 0)
    c_i = lax.broadcasted_iota(jnp.int32, (L, L), 1)
    causal = c_i <= r_i
    tri = jnp.where(causal, 1.0, 0.0).astype(F32)
    acum = _fdot(tri, da)
    last = acum[L - 1:L, :]
    ea = jnp.exp(acum)
    wdec = jnp.exp(last - acum)

    def pad_rows(v):
        if L == LANES:
            return v
        return jnp.concatenate([v, jnp.zeros((LANES - L, v.shape[1]), v.dtype)], axis=0)

    acum_t = jnp.transpose(pad_rows(acum))

    def hilo(v):
        hi = v.astype(BF16)
        lo = (v - hi.astype(F32)).astype(BF16)
        return hi, lo

    parts = []
    for v in (dt, ea, wdec):
        parts.extend(hilo(v))
    stacked = jnp.concatenate(parts, axis=0)
    expanded = jnp.dot(stacked, e_ref[...], preferred_element_type=F32)
    dt_e = expanded[0:L] + expanded[L:2 * L]
    ea_e = expanded[2 * L:3 * L] + expanded[3 * L:4 * L]
    wd_e = expanded[4 * L:5 * L] + expanded[5 * L:6 * L]

    xdt = xs * dt_e
    xdt_b = xdt.astype(BF16)
    lane = lax.broadcasted_iota(jnp.int32, (L, LANES), 1)
    hpg = SSD_HEADS // SSD_GROUPS

    y_parts = []
    for g in range(SSD_GROUPS):
        bg = bm[:, g * SSD_STATE:(g + 1) * SSD_STATE]
        cg = cm[:, g * SSD_STATE:(g + 1) * SSD_STATE]
        cb = _bdot_nt(cg, bg)
        for j in range(hpg // 2):
            pair = g * (hpg // 2) + j
            xp = xdt_b[:, pair * LANES:(pair + 1) * LANES]
            ys = []
            for h in (2 * pair, 2 * pair + 1):
                seg = acum[:, h:h + 1] - acum_t[h:h + 1, 0:L]
                dec = jnp.exp(jnp.where(causal, seg, -jnp.inf))
                ys.append(_bdot(cb * dec, xp))
            y_parts.append(jnp.where(lane < SSD_HEAD_DIM, ys[0], ys[1]))
    y_in = jnp.concatenate(y_parts, axis=1)

    gw = hpg * SSD_HEAD_DIM
    y_st_parts = []
    for g in range(SSD_GROUPS):
        cg = cm[:, g * SSD_STATE:(g + 1) * SSD_STATE]
        y_st_parts.append(_bdot(cg, ht[g]))
    y_st = jnp.concatenate(y_st_parts, axis=1) * ea_e

    xw = (xdt * wd_e)
    for g in range(SSD_GROUPS):
        bg_t = jnp.transpose(pad_rows(bm[:, g * SSD_STATE:(g + 1) * SSD_STATE]))
        xw_g = pad_rows(xw[:, g * gw:(g + 1) * gw])
        ht[g] = ht[g] * ea_e[L - 1:L, g * gw:(g + 1) * gw] + _bdot(bg_t, xw_g)

    y = (y_in + y_st + dexp_ref[...] * xs) * _silu(z)
    outs = []
    for g in range(SSD_GROUPS):
        yg = y[:, g * gw:(g + 1) * gw]
        ms = jnp.mean(yg * yg, -1, keepdims=True)
        outs.append(yg * lax.rsqrt(ms + RMS_EPS) * nw_ref[:, g * gw:(g + 1) * gw])
    y_ref[0] = jnp.concatenate(outs, axis=1)

    @pl.when(c == nc - 1)
    def _():
        hout_ref[0] = ht[...]


def _ssd_scan(p_ssd, conv_prev8, h0t, cw, cb, dtb, alog, dexp, nw, emat, *, L, tv):
    b, t, _ = p_ssd.shape
    nc = t // L
    gw = SSD_WIDTH // SSD_GROUPS
    kern = functools.partial(_ssd_kernel, L=L, tv=tv)
    return pl.pallas_call(
        kern, grid=(b, nc),
        in_specs=[pl.BlockSpec((1, L, P_SSD), lambda i, c: (i, c, 0)),
                  pl.BlockSpec((1, SUBLANES, SSD_CONV_DIM), lambda i, c: (i, 0, 0)),
                  pl.BlockSpec((1, SSD_GROUPS, SSD_STATE, gw), lambda i, c: (i, 0, 0, 0)),
                  _const_spec((CONV_WIDTH, SSD_CONV_DIM)), _const_spec((1, SSD_CONV_DIM)),
                  _const_spec((1, LANES)), _const_spec((1, LANES)),
                  _const_spec((1, SSD_WIDTH)), _const_spec((1, SSD_WIDTH)),
                  _const_spec((LANES, SSD_WIDTH))],
        out_specs=[pl.BlockSpec((1, L, SSD_WIDTH), lambda i, c: (i, c, 0)),
                   pl.BlockSpec((1, SUBLANES, SSD_CONV_DIM), lambda i, c: (i, 0, 0)),
                   pl.BlockSpec((1, SSD_GROUPS, SSD_STATE, gw), lambda i, c: (i, 0, 0, 0))],
        out_shape=[jax.ShapeDtypeStruct((b, t, SSD_WIDTH), F32),
                   jax.ShapeDtypeStruct((b, SUBLANES, SSD_CONV_DIM), F32),
                   jax.ShapeDtypeStruct((b, SSD_GROUPS, SSD_STATE, gw), F32)],
        scratch_shapes=[pltpu.VMEM((L + SUBLANES, SSD_CONV_DIM), F32),
                        pltpu.VMEM((SSD_GROUPS, SSD_STATE, gw), F32)],
        compiler_params=_cparams(("parallel", "arbitrary")), name="ssd_scan",
    )(p_ssd, conv_prev8, h0t, cw, cb, dtb, alog, dexp, nw, emat)


def _gdn_kernel(p_ref, cprev_ref, s0_ref, cw_ref, bias_ref, alog_ref, nw_ref,
                y_ref, cnew_ref, sout_ref, xbuf, st, *, L, tv):
    c = pl.program_id(1)
    nc = pl.num_programs(1)
    H = GDN_HEADS
    D = GDN_HEAD_DIM
    R = H * L
    S = max(R, LANES)

    @pl.when(c == 0)
    def _():
        xbuf[0:SUBLANES, :] = cprev_ref[0]
        st[...] = s0_ref[0]

    qkv_raw = p_ref[0, :, 0:GDN_CONV_DIM]
    z = p_ref[0, :, GDN_CONV_DIM:GDN_CONV_DIM + GDN_WIDTH]
    ba = p_ref[0, :, GDN_CONV_DIM + GDN_WIDTH:P_GDN]

    qkv = _silu(_conv_chunk(xbuf, qkv_raw, cw_ref, c, L))
    cnew_ref[0] = xbuf[tv:tv + SUBLANES, :]
    xbuf[0:SUBLANES, :] = xbuf[L:L + SUBLANES, :]

    beta_f = 1.0 / (1.0 + jnp.exp(-ba))
    g_f = -jnp.exp(alog_ref[...]) * _softplus(ba + bias_ref[...])
    if tv < L:
        rows = lax.broadcasted_iota(jnp.int32, (L, LANES), 0)
        beta_f = jnp.where(rows < tv, beta_f, 0.0)
        g_f = jnp.where(rows < tv, g_f, 0.0)
    r_i = lax.broadcasted_iota(jnp.int32, (L, L), 0)
    c_i = lax.broadcasted_iota(jnp.int32, (L, L), 1)
    tri = jnp.where(c_i <= r_i, 1.0, 0.0).astype(F32)
    gcum_f = _fdot(tri, g_f)
    glast_f = jnp.broadcast_to(gcum_f[L - 1:L, :], (L, LANES))

    def pad_s(v):
        if R == S:
            return v
        return jnp.concatenate([v, jnp.zeros((S - R, v.shape[1]), v.dtype)], axis=0)

    def stack(v):
        return pad_s(jnp.concatenate([v[:, h * D:(h + 1) * D] for h in range(H)], axis=0))

    def col(v, off):
        return pad_s(jnp.concatenate([v[:, off + h:off + h + 1] for h in range(H)], axis=0))

    q_s = stack(qkv[:, 0:GDN_WIDTH])
    k_s = stack(qkv[:, GDN_WIDTH:2 * GDN_WIDTH])
    v_s = stack(qkv[:, 2 * GDN_WIDTH:3 * GDN_WIDTH])
    z_s = stack(z)
    q_s = q_s * lax.rsqrt(jnp.sum(q_s * q_s, -1, keepdims=True) + L2_EPS) * (D ** -0.5)
    k_s = k_s * lax.rsqrt(jnp.sum(k_s * k_s, -1, keepdims=True) + L2_EPS)
    beta = col(beta_f, 0)
    gcum = col(gcum_f, H)
    glast = col(glast_f, H)

    cmat = jnp.broadcast_to(gcum, (S, S))
    diff = cmat - jnp.transpose(cmat)
    rs = lax.broadcasted_iota(jnp.int32, (S, S), 0)
    cs = lax.broadcasted_iota(jnp.int32, (S, S), 1)
    if L & (L - 1) == 0:
        sh = L.bit_length() - 1
        same = (rs >> sh) == (cs >> sh)
    else:
        same = (rs // L) == (cs // L)
    incl = same & (cs <= rs)
    strict = same & (cs < rs)
    dec = jnp.exp(jnp.where(incl, diff, -jnp.inf))

    kb = k_s * beta
    n_mat = -jnp.where(strict, _bdot_nt(kb, k_s) * dec, 0.0)
    eye = jnp.where(rs == cs, 1.0, 0.0).astype(F32)
    t_mat = eye + n_mat
    npow = n_mat
    span = 2
    while span < L:
        npow = _fdot(npow, npow)
        t_mat = t_mat + _fdot(t_mat, npow)
        span *= 2

    eg = jnp.exp(gcum)
    u = _fdot(t_mat, v_s * beta)
    w = _fdot(t_mat, kb * eg)
    qg = q_s * eg
    ws = []
    qs_ = []
    for h in range(H):
        sh_b = st[h]
        ws.append(_bdot(w[h * L:(h + 1) * L], sh_b))
        qs_.append(_bdot(qg[h * L:(h + 1) * L], sh_b))
    v_new = u - pad_s(jnp.concatenate(ws, axis=0))
    attn = jnp.where(incl, _bdot_nt(q_s, k_s) * dec, 0.0)
    o = pad_s(jnp.concatenate(qs_, axis=0)) + _bdot(attn, v_new)

    kd = k_s * jnp.exp(glast - gcum)
    kd_t = jnp.transpose(kd)
    row_head = lax.broadcasted_iota(jnp.int32, (S, D), 0)
    eg_last = jnp.exp(glast)
    for h in range(H):
        vm = jnp.where((row_head >= h * L) & (row_head < (h + 1) * L), v_new, 0.0)
        st[h] = st[h] * eg_last[h * L:h * L + 1, :] + _bdot(kd_t, vm)

    ms = jnp.mean(o * o, -1, keepdims=True)
    o = o * lax.rsqrt(ms + RMS_EPS) * nw_ref[...] * _silu(z_s)
    y_ref[0] = jnp.concatenate([o[h * L:(h + 1) * L] for h in range(H)], axis=1)

    @pl.when(c == nc - 1)
    def _():
        sout_ref[0] = st[...]


def _gdn_scan(p_gdn, conv_prev8, s0, cw, bias, alog, nw, *, L, tv):
    b, t, _ = p_gdn.shape
    nc = t // L
    kern = functools.partial(_gdn_kernel, L=L, tv=tv)
    sshape = (1, GDN_HEADS, GDN_HEAD_DIM, GDN_HEAD_DIM)
    return pl.pallas_call(
        kern, grid=(b, nc),
        in_specs=[pl.BlockSpec((1, L, P_GDN), lambda i, c: (i, c, 0)),
                  pl.BlockSpec((1, SUBLANES, GDN_CONV_DIM), lambda i, c: (i, 0, 0)),
                  pl.BlockSpec(sshape, lambda i, c: (i, 0, 0, 0)),
                  _const_spec((CONV_WIDTH, GDN_CONV_DIM)),
                  _const_spec((1, LANES)), _const_spec((1, LANES)), _const_spec((1, GDN_HEAD_DIM))],
        out_specs=[pl.BlockSpec((1, L, GDN_WIDTH), lambda i, c: (i, c, 0)),
                   pl.BlockSpec((1, SUBLANES, GDN_CONV_DIM), lambda i, c: (i, 0, 0)),
                   pl.BlockSpec(sshape, lambda i, c: (i, 0, 0, 0))],
        out_shape=[jax.ShapeDtypeStruct((b, t, GDN_WIDTH), F32),
                   jax.ShapeDtypeStruct((b, SUBLANES, GDN_CONV_DIM), F32),
                   jax.ShapeDtypeStruct((b,) + sshape[1:], F32)],
        scratch_shapes=[pltpu.VMEM((L + SUBLANES, GDN_CONV_DIM), F32),
                        pltpu.VMEM(sshape[1:], F32)],
        compiler_params=_cparams(("parallel", "arbitrary")), name="gdn_scan",
    )(p_gdn, conv_prev8, s0, cw, bias, alog, nw)


def _mla_prep_kernel(p_ref, cos_ref, sin_ref, qnw_ref, kvnw_ref, wuq_ref, wuk_ref,
                     q_ref, kcat_ref, ckv_ref, kr_ref, gate_ref):
    cq = p_ref[:, 0:MLA_Q_RANK]
    ckv = p_ref[:, MLA_Q_RANK:MLA_Q_RANK + MLA_KV_RANK]
    kr = p_ref[:, MLA_Q_RANK + MLA_KV_RANK:MLA_Q_RANK + MLA_KV_RANK + 2 * LANES]
    gate = p_ref[:, MLA_Q_RANK + MLA_KV_RANK + 2 * LANES:P_MLA]
    cos = cos_ref[...]
    sin = sin_ref[...]

    cqn = cq * lax.rsqrt(jnp.mean(cq * cq, -1, keepdims=True) + RMS_EPS) * qnw_ref[...]
    q = _bdot(cqn, wuq_ref[...])
    nope_w = MLA_HEADS * MLA_NOPE
    x1 = q[:, nope_w:nope_w + LANES]
    x2 = q[:, nope_w + LANES:nope_w + 2 * LANES]
    r_all = jnp.concatenate([x1 * cos - x2 * sin, x2 * cos + x1 * sin], axis=1) * MLA_SCALE

    ckvn = ckv * lax.rsqrt(jnp.mean(ckv * ckv, -1, keepdims=True) + RMS_EPS) * kvnw_ref[...]
    k1 = kr[:, 0:LANES]
    k2 = kr[:, LANES:2 * LANES]
    kr_rot = jnp.concatenate([k1 * cos - k2 * sin, k2 * cos + k1 * sin], axis=1)
    ckv_ref[...] = ckvn
    kr_ref[...] = kr_rot
    kcat_ref[...] = jnp.concatenate([ckvn, kr_rot], axis=1).astype(BF16)
    gate_ref[...] = _silu(gate)

    lane = lax.broadcasted_iota(jnp.int32, (1, 2 * LANES), 1)
    head_of_lane = (lane & (LANES - 1)) >> 4
    for pair in range(MLA_HEADS // 2):
        qn = q[:, pair * LANES:(pair + 1) * LANES] * MLA_SCALE
        qlat = _bdot(qn, wuk_ref[pair])
        for i in range(2):
            h = 2 * pair + i
            base = h * QK_WIDTH
            q_ref[:, base:base + MLA_KV_RANK] = qlat[:, i * MLA_KV_RANK:(i + 1) * MLA_KV_RANK].astype(BF16)
            q_ref[:, base + MLA_KV_RANK:base + QK_WIDTH] = jnp.where(head_of_lane == h, r_all, 0.0).astype(BF16)


def _mla_prep(p_mla, cos_t, sin_t, qnw, kvnw, wuq, wuk_pairs, tm):
    m = p_mla.shape[0]
    nt = cos_t.shape[0] // tm
    row = lambda i: (i, 0)
    tab = lambda i: (i % nt, 0)
    qw = MLA_HEADS * QK_WIDTH
    return pl.pallas_call(
        _mla_prep_kernel, grid=(m // tm,),
        in_specs=[pl.BlockSpec((tm, P_MLA), row), pl.BlockSpec((tm, LANES), tab), pl.BlockSpec((tm, LANES), tab),
                  _const_spec((1, MLA_Q_RANK)), _const_spec((1, MLA_KV_RANK)),
                  _const_spec(wuq.shape), _const_spec(wuk_pairs.shape)],
        out_specs=[pl.BlockSpec((tm, qw), row), pl.BlockSpec((tm, QK_WIDTH), row),
                   pl.BlockSpec((tm, MLA_KV_RANK), row), pl.BlockSpec((tm, 2 * LANES), row),
                   pl.BlockSpec((tm, MLA_WIDTH), row)],
        out_shape=[jax.ShapeDtypeStruct((m, qw), BF16), jax.ShapeDtypeStruct((m, QK_WIDTH), BF16),
                   jax.ShapeDtypeStruct((m, MLA_KV_RANK), F32), jax.ShapeDtypeStruct((m, 2 * LANES), F32),
                   jax.ShapeDtypeStruct((m, MLA_WIDTH), F32)],
        compiler_params=_cparams(("parallel",)), name="mla_prep",
    )(p_mla, cos_t, sin_t, qnw, kvnw, wuq, wuk_pairs)


def _online_softmax_step(s, v_b, m_scr, l_scr, acc_scr):
    m_prev = m_scr[...]
    m_new = jnp.maximum(m_prev, jnp.max(s, -1, keepdims=True))
    alpha = jnp.exp(m_prev - m_new)
    p = jnp.exp(s - m_new)
    l_scr[...] = alpha * l_scr[...] + jnp.sum(p, -1, keepdims=True)
    acc_scr[...] = alpha * acc_scr[...] + jnp.dot(p.astype(BF16), v_b, preferred_element_type=F32)
    m_scr[...] = m_new


def _flash_kernel(q_ref, k_ref, o_ref, m_scr, l_scr, acc_scr, *, tq, tk):
    qi = pl.program_id(1)
    ki = pl.program_id(2)
    nk = pl.num_programs(2)
    rows = tq * MLA_HEADS

    @pl.when(ki == 0)
    def _():
        m_scr[...] = jnp.full(m_scr.shape, -jnp.inf, F32)
        l_scr[...] = jnp.zeros(l_scr.shape, F32)
        acc_scr[...] = jnp.zeros(acc_scr.shape, F32)

    @pl.when(ki * tk < (qi + 1) * tq)
    def _():
        k_b = k_ref[0]
        s = lax.dot_general(q_ref[0], k_b, (((1,), (1,)), ((), ())), preferred_element_type=F32)
        q_pos = qi * tq + (lax.broadcasted_iota(jnp.int32, (rows, tk), 0) >> 3)
        k_pos = ki * tk + lax.broadcasted_iota(jnp.int32, (rows, tk), 1)
        s = jnp.where(k_pos <= q_pos, s, -jnp.inf)
        _online_softmax_step(s, k_b[:, 0:MLA_KV_RANK], m_scr, l_scr, acc_scr)

    @pl.when(ki == nk - 1)
    def _():
        o_ref[0] = (acc_scr[...] * (1.0 / l_scr[...])).astype(o_ref.dtype)


def _flash_attention(q, kcat, *, tq, tk):
    b, th, _ = q.shape
    t = th // MLA_HEADS
    rows = tq * MLA_HEADS
    kern = functools.partial(_flash_kernel, tq=tq, tk=tk)

    def k_map(i, qi, ki):
        return (i, jnp.minimum(ki, ((qi + 1) * tq - 1) // tk), 0)

    return pl.pallas_call(
        kern, grid=(b, t // tq, t // tk),
        in_specs=[pl.BlockSpec((1, rows, QK_WIDTH), lambda i, qi, ki: (i, qi, 0)),
                  pl.BlockSpec((1, tk, QK_WIDTH), k_map)],
        out_specs=pl.BlockSpec((1, rows, MLA_KV_RANK), lambda i, qi, ki: (i, qi, 0)),
        out_shape=jax.ShapeDtypeStruct((b, th, MLA_KV_RANK), BF16),
        scratch_shapes=[pltpu.VMEM((rows, 1), F32), pltpu.VMEM((rows, 1), F32),
                        pltpu.VMEM((rows, MLA_KV_RANK), F32)],
        compiler_params=_cparams(("parallel", "parallel", "arbitrary")), name="mla_flash",
    )(q, kcat)


PAGES_PER_STEP = 8


def _paged_kernel(pt_ref, q_ref, knew_ref, sel_ref, *refs, tpad):
    npg = PAGES_PER_STEP
    lat_refs = refs[0:npg]
    rope_refs = refs[npg:2 * npg]
    o_ref = refs[2 * npg]
    m_scr, l_scr, acc_scr, qr_scr = refs[2 * npg + 1:]
    j = pl.program_id(1)
    nj = pl.num_programs(1)
    rows = tpad * MLA_HEADS

    @pl.when(j == 0)
    def _():
        m_scr[...] = jnp.full(m_scr.shape, -jnp.inf, F32)
        l_scr[...] = jnp.zeros(l_scr.shape, F32)
        acc_scr[...] = jnp.zeros(acc_scr.shape, F32)
        qr_scr[...] = jnp.dot(q_ref[0, :, MLA_KV_RANK:QK_WIDTH], sel_ref[...],
                              preferred_element_type=F32).astype(BF16)

    q_lat = q_ref[0, :, 0:MLA_KV_RANK]
    q_rd = qr_scr[...]
    lat_b = [r[0].astype(BF16) for r in lat_refs]
    s_parts = []
    for i in range(npg):
        s_parts.append(lax.dot_general(q_lat, lat_b[i], (((1,), (1,)), ((), ())), preferred_element_type=F32)
                       + lax.dot_general(q_rd, rope_refs[i][0].astype(BF16), (((1,), (1,)), ((), ())),
                                         preferred_element_type=F32))
    s = jnp.concatenate(s_parts, axis=1)
    v_b = jnp.concatenate(lat_b, axis=0)
    _online_softmax_step(s, v_b, m_scr, l_scr, acc_scr)

    @pl.when(j == nj - 1)
    def _():
        k_b = knew_ref[0]
        s2 = lax.dot_general(q_ref[0], k_b, (((1,), (1,)), ((), ())), preferred_element_type=F32)
        q_pos = lax.broadcasted_iota(jnp.int32, (rows, tpad), 0) >> 3
        k_pos = lax.broadcasted_iota(jnp.int32, (rows, tpad), 1)
        s2 = jnp.where(k_pos <= q_pos, s2, -jnp.inf)
        _online_softmax_step(s2, k_b[:, 0:MLA_KV_RANK], m_scr, l_scr, acc_scr)
        o_ref[0] = (acc_scr[...] * (1.0 / l_scr[...])).astype(o_ref.dtype)


def _paged_attention(q, knew, cache_lat, cache_rope, page_table, sel, *, tpad):
    b = q.shape[0]
    n_pages = page_table.shape[1]
    npg = PAGES_PER_STEP
    rows = tpad * MLA_HEADS
    kern = functools.partial(_paged_kernel, tpad=tpad)

    def page_map(i):
        return lambda bi, j, pt: (pt[bi, j * npg + i], 0, 0)

    in_specs = [pl.BlockSpec((1, rows, QK_WIDTH), lambda bi, j, pt: (bi, 0, 0)),
                pl.BlockSpec((1, tpad, QK_WIDTH), lambda bi, j, pt: (bi, 0, 0)),
                pl.BlockSpec(sel.shape, lambda bi, j, pt: (0, 0))]
    in_specs += [pl.BlockSpec((1, PAGE_SIZE, MLA_KV_RANK), page_map(i)) for i in range(npg)]
    in_specs += [pl.BlockSpec((1, PAGE_SIZE, MLA_ROPE), page_map(i)) for i in range(npg)]
    grid_spec = pltpu.PrefetchScalarGridSpec(
        num_scalar_prefetch=1, grid=(b, n_pages // npg), in_specs=in_specs,
        out_specs=pl.BlockSpec((1, rows, MLA_KV_RANK), lambda bi, j, pt: (bi, 0, 0)),
        scratch_shapes=[pltpu.VMEM((rows, 1), F32), pltpu.VMEM((rows, 1), F32),
                        pltpu.VMEM((rows, MLA_KV_RANK), F32), pltpu.VMEM((rows, MLA_ROPE), BF16)])
    return pl.pallas_call(
        kern, grid_spec=grid_spec,
        out_shape=jax.ShapeDtypeStruct((b, rows, MLA_KV_RANK), BF16),
        compiler_params=_cparams(("parallel", "arbitrary")), name="mla_paged",
    )(page_table, q, knew, sel, *([cache_lat] * npg), *([cache_rope] * npg))


def _out_kernel(x_ref, yssd_ref, olat_ref, gate_ref, ygdn_ref, wuv_ref, wout_ref, g_ref, b_ref, o_ref):
    y_mla = []
    pw = 2 * MLA_KV_RANK
    for pair in range(MLA_HEADS // 2):
        o_pair = jnp.dot(olat_ref[:, pair * pw:(pair + 1) * pw], wuv_ref[pair], preferred_element_type=F32)
        y_mla.append(o_pair * gate_ref[:, pair * LANES:(pair + 1) * LANES])
    y_mla = jnp.concatenate(y_mla, axis=1)
    mm = (_bdot(yssd_ref[...], wout_ref[0:SSD_WIDTH, :])
          + _bdot(y_mla, wout_ref[SSD_WIDTH:SSD_WIDTH + MLA_WIDTH, :])
          + _bdot(ygdn_ref[...], wout_ref[SSD_WIDTH + MLA_WIDTH:MIX_WIDTH, :]))
    v = DEEPNORM_ALPHA * x_ref[...] + mm
    mu = jnp.mean(v, -1, keepdims=True)
    vc = v - mu
    var = jnp.mean(vc * vc, -1, keepdims=True)
    o_ref[...] = vc * lax.rsqrt(var + LN_EPS) * g_ref[...] + b_ref[...]


def _out_proj(x, y_ssd, o_lat, gate, y_gdn, wuv_pairs, wout, g, b, tm):
    m = x.shape[0]
    row = lambda i: (i, 0)
    return pl.pallas_call(
        _out_kernel, grid=(m // tm,),
        in_specs=[pl.BlockSpec((tm, D_MODEL), row), pl.BlockSpec((tm, SSD_WIDTH), row),
                  pl.BlockSpec((tm, MLA_HEADS * MLA_KV_RANK), row), pl.BlockSpec((tm, MLA_WIDTH), row),
                  pl.BlockSpec((tm, GDN_WIDTH), row),
                  _const_spec(wuv_pairs.shape), _const_spec(wout.shape),
                  _const_spec((1, D_MODEL)), _const_spec((1, D_MODEL))],
        out_specs=pl.BlockSpec((tm, D_MODEL), row),
        out_shape=jax.ShapeDtypeStruct((m, D_MODEL), F32),
        compiler_params=_cparams(("parallel",)), name="out_proj",
    )(x, y_ssd, o_lat, gate, y_gdn, wuv_pairs, wout, g, b)


def _pad_lanes(v, n=LANES):
    return jnp.pad(v, ((0, 0), (0, n - v.shape[1])))


def _prep_layer_weights(l, w_in, ssd_conv_w, ssd_conv_b, ssd_dt_bias, ssd_a_log, ssd_d, ssd_norm_w,
                        mla_q_norm_w, mla_w_uq, mla_kv_norm_w, mla_w_uk, mla_w_uv,
                        gdn_conv_w, gdn_dt_bias, gdn_a_log, gdn_norm_w, w_out, ln_g, ln_b):
    (w_z, w_xbc, w_dt, w_cq, w_ckv, w_kr, w_gate, w_qkv, w_gz, w_gb, w_ga) = jnp.split(w_in[l], IN_OFFSETS, axis=1)
    w_ssd = jnp.concatenate([w_z, w_xbc, _pad_lanes(w_dt)], axis=1).astype(BF16)
    kr_tiled = jnp.concatenate([jnp.tile(w_kr[:, :ROPE_HALF], (1, MLA_HEADS)),
                                jnp.tile(w_kr[:, ROPE_HALF:], (1, MLA_HEADS))], axis=1)
    w_mla = jnp.concatenate([w_cq, w_ckv, kr_tiled, w_gate], axis=1).astype(BF16)
    w_gdn = jnp.concatenate([w_qkv, w_gz, _pad_lanes(jnp.concatenate([w_gb, w_ga], axis=1))], axis=1).astype(BF16)

    uq = mla_w_uq[l].reshape(MLA_Q_RANK, MLA_HEADS, MLA_NOPE + MLA_ROPE)
    wuq = jnp.concatenate([uq[:, :, :MLA_NOPE].reshape(MLA_Q_RANK, -1),
                           uq[:, :, MLA_NOPE:MLA_NOPE + ROPE_HALF].reshape(MLA_Q_RANK, -1),
                           uq[:, :, MLA_NOPE + ROPE_HALF:].reshape(MLA_Q_RANK, -1)], axis=1).astype(BF16)
    uk = jnp.transpose(mla_w_uk[l], (1, 2, 0))
    uv = jnp.transpose(mla_w_uv[l], (1, 0, 2))
    zk = jnp.zeros((MLA_NOPE, MLA_KV_RANK), F32)
    zv = jnp.zeros((MLA_KV_RANK, MLA_V), F32)
    wuk_pairs = jnp.stack([jnp.block([[uk[2 * p], zk], [zk, uk[2 * p + 1]]]) for p in range(MLA_HEADS // 2)]).astype(BF16)
    wuv_pairs = jnp.stack([jnp.block([[uv[2 * p], zv], [zv, uv[2 * p + 1]]]) for p in range(MLA_HEADS // 2)]).astype(BF16)

    gdn_bias = jnp.pad(gdn_dt_bias[l][None, :], ((0, 0), (GDN_HEADS, LANES - 2 * GDN_HEADS)))
    gdn_alog = jnp.pad(gdn_a_log[l][None, :], ((0, 0), (GDN_HEADS, LANES - 2 * GDN_HEADS)))
    return dict(
        w_ssd=w_ssd, w_mla=w_mla, w_gdn=w_gdn,
        ssd_cw=ssd_conv_w[l], ssd_cb=ssd_conv_b[l][None, :],
        ssd_dtb=_pad_lanes(ssd_dt_bias[l][None, :]), ssd_alog=_pad_lanes(ssd_a_log[l][None, :]),
        ssd_dexp=jnp.repeat(ssd_d[l], SSD_HEAD_DIM)[None, :], ssd_nw=ssd_norm_w[l][None, :],
        qnw=mla_q_norm_w[l][None, :], kvnw=mla_kv_norm_w[l][None, :], wuq=wuq,
        wuk_pairs=wuk_pairs, wuv_pairs=wuv_pairs,
        gdn_cw=gdn_conv_w[l], gdn_bias=gdn_bias, gdn_alog=gdn_alog, gdn_nw=gdn_norm_w[l][None, :],
        w_out=w_out[l].astype(BF16), ln_g=ln_g[l][None, :], ln_b=ln_b[l][None, :])


def _rope_tables(pos):
    inv = ROPE_THETA ** (-jnp.arange(ROPE_HALF, dtype=F32) / ROPE_HALF)
    ang = pos.astype(F32)[:, None] * inv[None, :]
    return jnp.tile(jnp.cos(ang), (1, MLA_HEADS)), jnp.tile(jnp.sin(ang), (1, MLA_HEADS))


def _head_expand_matrix():
    e = np.zeros((LANES, SSD_WIDTH), np.float32)
    for h in range(SSD_HEADS):
        e[h, h * SSD_HEAD_DIM:(h + 1) * SSD_HEAD_DIM] = 1.0
    return jnp.asarray(e, BF16)


def _rope_select_matrix():
    s = np.zeros((2 * LANES, MLA_ROPE), np.float32)
    for j in range(2 * LANES):
        s[j, (j // LANES) * ROPE_HALF + (j % ROPE_HALF)] = 1.0
    return jnp.asarray(s, BF16)


def _tail8(conv_state):
    return jnp.pad(conv_state, ((0, 0), (SUBLANES - (CONV_WIDTH - 1), 0), (0, 0)))


def _ssd_state_to_kernel(s):
    b = s.shape[0]
    hpg = SSD_HEADS // SSD_GROUPS
    s = s.reshape(b, SSD_GROUPS, hpg, SSD_HEAD_DIM, SSD_STATE)
    return jnp.transpose(s, (0, 1, 4, 2, 3)).reshape(b, SSD_GROUPS, SSD_STATE, hpg * SSD_HEAD_DIM)


def _ssd_state_from_kernel(s):
    b = s.shape[0]
    hpg = SSD_HEADS // SSD_GROUPS
    s = s.reshape(b, SSD_GROUPS, SSD_STATE, hpg, SSD_HEAD_DIM)
    return jnp.transpose(s, (0, 1, 3, 4, 2)).reshape(b, SSD_HEADS, SSD_HEAD_DIM, SSD_STATE)


def _trunk(x, pos, tv, ssd_conv, ssd_state, gdn_conv, gdn_state, emb_g, emb_b, weights, cfg, paged=None):
    b, t, _ = x.shape
    m = b * t
    tm = cfg["tm"]
    cos_t, sin_t = _rope_tables(pos)
    if cos_t.shape[0] < tm:
        reps = tm // cos_t.shape[0]
        cos_t, sin_t = jnp.tile(cos_t, (reps, 1)), jnp.tile(sin_t, (reps, 1))
    emat = _head_expand_matrix()
    sel = _rope_select_matrix()
    h = _layer_norm(x.reshape(m, D_MODEL), emb_g, emb_b, tm)
    new_states = []
    for l in range(DEPTH):
        w = weights[l]
        p_ssd = _project(h, w["w_ssd"], tm, "proj_ssd").reshape(b, t, P_SSD)
        p_mla = _project(h, w["w_mla"], tm, "proj_mla")
        p_gdn = _project(h, w["w_gdn"], tm, "proj_gdn").reshape(b, t, P_GDN)

        y_ssd, ssd_c8, ssd_ht = _ssd_scan(
            p_ssd, _tail8(ssd_conv[l]), _ssd_state_to_kernel(ssd_state[l]),
            w["ssd_cw"], w["ssd_cb"], w["ssd_dtb"], w["ssd_alog"], w["ssd_dexp"], w["ssd_nw"], emat,
            L=cfg["ssd_chunk"], tv=tv if tv < cfg["ssd_chunk"] else cfg["ssd_chunk"])
        y_gdn, gdn_c8, gdn_s = _gdn_scan(
            p_gdn, _tail8(gdn_conv[l]), gdn_state[l], w["gdn_cw"], w["gdn_bias"], w["gdn_alog"], w["gdn_nw"],
            L=cfg["gdn_chunk"], tv=tv if tv < cfg["gdn_chunk"] else cfg["gdn_chunk"])

        q, kcat, ckv, kr_t, gate = _mla_prep(p_mla, cos_t, sin_t, w["qnw"], w["kvnw"], w["wuq"], w["wuk_pairs"], tm)
        q = q.reshape(b, t * MLA_HEADS, QK_WIDTH)
        kcat = kcat.reshape(b, t, QK_WIDTH)
        if paged is None:
            o_lat = _flash_attention(q, kcat, tq=cfg["tq"], tk=cfg["tk"])
        else:
            cache_lat, cache_rope, page_table = paged
            o_lat = _paged_attention(q, kcat, cache_lat[l], cache_rope[l], page_table, sel, tpad=t)
        o_lat = o_lat.reshape(m, MLA_HEADS * MLA_KV_RANK)

        h = _out_proj(h, y_ssd.reshape(m, SSD_WIDTH), o_lat, gate, y_gdn.reshape(m, GDN_WIDTH),
                      w["wuv_pairs"], w["w_out"], w["ln_g"], w["ln_b"], tm)

        kr = jnp.concatenate([kr_t[:, 0:ROPE_HALF], kr_t[:, LANES:LANES + ROPE_HALF]], axis=1)
        sl = slice(SUBLANES - (CONV_WIDTH - 1), SUBLANES)
        new_states.append((ckv.reshape(b, t, MLA_KV_RANK), kr.reshape(b, t, MLA_ROPE),
                           ssd_c8[:, sl], _ssd_state_from_kernel(ssd_ht), gdn_c8[:, sl], gdn_s))
    return h.reshape(b, t, D_MODEL), tuple(jnp.stack(s) for s in zip(*new_states))


def kernel(x_prompt, x_sample, cache_kv_latent, cache_k_rope, state_ssd_conv, state_ssd, state_gdn_conv, state_gdn, page_table, emb_ln_g, emb_ln_b, w_in, ssd_conv_w, ssd_conv_b, ssd_dt_bias, ssd_a_log, ssd_d, ssd_norm_w, mla_q_norm_w, mla_w_uq, mla_kv_norm_w, mla_w_uk, mla_w_uv, gdn_conv_w, gdn_dt_bias, gdn_a_log, gdn_norm_w, w_out, ln_g, ln_b):
    weights = [_prep_layer_weights(l, w_in, ssd_conv_w, ssd_conv_b, ssd_dt_bias, ssd_a_log, ssd_d, ssd_norm_w,
                                   mla_q_norm_w, mla_w_uq, mla_kv_norm_w, mla_w_uk, mla_w_uv,
                                   gdn_conv_w, gdn_dt_bias, gdn_a_log, gdn_norm_w, w_out, ln_g, ln_b)
               for l in range(DEPTH)]

    bp, tp, _ = x_prompt.shape
    zeros = lambda *s: jnp.zeros(s, F32)
    ssd_chunk = SSD_CHUNK if tp % SSD_CHUNK == 0 else tp
    gdn_chunk = GDN_CHUNK if tp % GDN_CHUNK == 0 else tp
    cfg_p = dict(tm=min(512, bp * tp), ssd_chunk=ssd_chunk, gdn_chunk=gdn_chunk,
                 tq=min(128, tp), tk=min(512, tp))
    y_prompt, st_p = _trunk(
        x_prompt, jnp.arange(tp), tp,
        zeros(DEPTH, bp, CONV_WIDTH - 1, SSD_CONV_DIM), zeros(DEPTH, bp, SSD_HEADS, SSD_HEAD_DIM, SSD_STATE),
        zeros(DEPTH, bp, CONV_WIDTH - 1, GDN_CONV_DIM), zeros(DEPTH, bp, GDN_HEADS, GDN_HEAD_DIM, GDN_HEAD_DIM),
        emb_ln_g, emb_ln_b, weights, cfg_p)

    bs, ts, _ = x_sample.shape
    tpad = -(-ts // SUBLANES) * SUBLANES
    past_len = page_table.shape[1] * PAGE_SIZE
    xs = jnp.pad(x_sample, ((0, 0), (0, tpad - ts), (0, 0)))
    cfg_s = dict(tm=min(256, bs * tpad), ssd_chunk=tpad, gdn_chunk=tpad, tq=tpad, tk=tpad)
    y_s, st_s = _trunk(
        xs, past_len + jnp.arange(tpad), ts, state_ssd_conv, state_ssd, state_gdn_conv, state_gdn,
        emb_ln_g, emb_ln_b, weights, cfg_s, paged=(cache_kv_latent, cache_k_rope, page_table))
    y_sample = y_s[:, :ts]
    s_lat, s_rope, s_ssd_conv, s_ssd, s_gdn_conv, s_gdn = st_s
    return (y_prompt, y_sample) + tuple(st_p) + (s_lat[:, :, :ts], s_rope[:, :, :ts], s_ssd_conv, s_ssd, s_gdn_conv, s_gdn)
```

```python
import functools

import jax
import jax.numpy as jnp
import numpy as np
from jax import lax
from jax.experimental import pallas as pl
from jax.experimental.pallas import tpu as pltpu

F32 = jnp.float32
BF16 = jnp.bfloat16

D_MODEL = 1024
CONV_WIDTH = 4
SSD_HEADS = 16
SSD_HEAD_DIM = 64
SSD_WIDTH = SSD_HEADS * SSD_HEAD_DIM
SSD_GROUPS = 2
SSD_STATE = 128
SSD_CONV_DIM = SSD_WIDTH + 2 * SSD_GROUPS * SSD_STATE
SSD_CHUNK = 128
MLA_HEADS = 8
MLA_NOPE = 64
MLA_ROPE = 32
MLA_V = 64
MLA_WIDTH = MLA_HEADS * MLA_V
MLA_Q_RANK = 384
MLA_KV_RANK = 256
MLA_SCALE = (MLA_NOPE + MLA_ROPE) ** -0.5
ROPE_THETA = 10000.0
GDN_HEADS = 4
GDN_HEAD_DIM = 128
GDN_WIDTH = GDN_HEADS * GDN_HEAD_DIM
GDN_CONV_DIM = 3 * GDN_WIDTH
GDN_CHUNK = 64
MIX_WIDTH = SSD_WIDTH + MLA_WIDTH + GDN_WIDTH
IN_SIZES = (SSD_WIDTH, SSD_CONV_DIM, SSD_HEADS, MLA_Q_RANK, MLA_KV_RANK, MLA_ROPE, MLA_WIDTH,
            GDN_CONV_DIM, GDN_WIDTH, GDN_HEADS, GDN_HEADS)
IN_OFFSETS = tuple(int(o) for o in np.cumsum(IN_SIZES)[:-1])
DEPTH = 2
DEEPNORM_ALPHA = (2 * DEPTH) ** 0.25
LN_EPS = 1e-5
RMS_EPS = 1e-6
L2_EPS = 1e-6
PAGE_SIZE = 128

LANES = 128
SUBLANES = 8
BF16_ROWS = 16
ROPE_HALF = MLA_ROPE // 2
P_SSD = SSD_WIDTH + SSD_CONV_DIM + LANES
P_MLA = MLA_Q_RANK + MLA_KV_RANK + 2 * LANES + MLA_WIDTH
P_GDN = GDN_CONV_DIM + GDN_WIDTH + LANES
QK_WIDTH = MLA_KV_RANK + 2 * LANES
VMEM_LIMIT = 56 * 1024 * 1024
PAGES_PER_GROUP = 16
GDN_BATCH_PER_STEP = 2


def _cparams(sem):
    return pltpu.CompilerParams(dimension_semantics=sem, vmem_limit_bytes=VMEM_LIMIT)


def _bdot(a, b):
    return jnp.dot(a.astype(BF16), b.astype(BF16), preferred_element_type=F32)


def _bdot_nt(a, b):
    return lax.dot_general(a.astype(BF16), b.astype(BF16), (((1,), (1,)), ((), ())),
                           preferred_element_type=F32)


def _fdot(a, b):
    return jnp.dot(a, b, precision=lax.Precision.HIGHEST, preferred_element_type=F32)


def _silu(x):
    return x * (1.0 / (1.0 + jnp.exp(-x)))


def _softplus(x):
    return jnp.maximum(x, 0.0) + jnp.log1p(jnp.exp(-jnp.abs(x)))


def _const_spec(shape):
    nd = len(shape)
    return pl.BlockSpec(shape, lambda *_: (0,) * nd)


def _ln_kernel(x_ref, g_ref, b_ref, o_ref):
    x = x_ref[...]
    mu = jnp.mean(x, -1, keepdims=True)
    xc = x - mu
    var = jnp.mean(xc * xc, -1, keepdims=True)
    o_ref[...] = xc * lax.rsqrt(var + LN_EPS) * g_ref[...] + b_ref[...]


def _layer_norm(x, g, b, tm):
    m, d = x.shape
    return pl.pallas_call(
        _ln_kernel, grid=(m // tm,),
        in_specs=[pl.BlockSpec((tm, d), lambda i: (i, 0)), _const_spec((1, d)), _const_spec((1, d))],
        out_specs=pl.BlockSpec((tm, d), lambda i: (i, 0)),
        out_shape=jax.ShapeDtypeStruct((m, d), F32),
        compiler_params=_cparams(("parallel",)), name="emb_ln")(x, g.reshape(1, d), b.reshape(1, d))


def _proj_kernel(x_ref, w_ref, o_ref):
    o_ref[...] = _bdot(x_ref[...], w_ref[...])


def _project(x, w, tm, name):
    m, k = x.shape
    n = w.shape[1]
    return pl.pallas_call(
        _proj_kernel, grid=(m // tm,),
        in_specs=[pl.BlockSpec((tm, k), lambda i: (i, 0)), _const_spec((k, n))],
        out_specs=pl.BlockSpec((tm, n), lambda i: (i, 0)),
        out_shape=jax.ShapeDtypeStruct((m, n), F32),
        compiler_params=_cparams(("parallel",)), name=name)(x, w)


def _conv_chunk(xbuf, x_new, cw_ref, L):
    xbuf[SUBLANES:SUBLANES + L, :] = x_new
    y = cw_ref[CONV_WIDTH - 1:CONV_WIDTH, :] * x_new
    for k in range(CONV_WIDTH - 1):
        off = SUBLANES - (CONV_WIDTH - 1) + k
        y = y + cw_ref[k:k + 1, :] * xbuf[off:off + L, :]
    return y


def _ssd_kernel(p_ref, cprev_ref, h0_ref, cw_ref, cb_ref, dtb_ref, alog_ref, dexp_ref, nw_ref, e_ref,
                y_ref, cnew_ref, hout_ref, xbuf, ht, *, L, tv):
    c = pl.program_id(1)
    nc = pl.num_programs(1)

    @pl.when(c == 0)
    def _():
        xbuf[0:SUBLANES, :] = cprev_ref[0]
        ht[...] = h0_ref[0]

    z = p_ref[0, :, 0:SSD_WIDTH]
    xbc_raw = p_ref[0, :, SSD_WIDTH:SSD_WIDTH + SSD_CONV_DIM]
    dt_raw = p_ref[0, :, SSD_WIDTH + SSD_CONV_DIM:P_SSD]

    xbc = _silu(_conv_chunk(xbuf, xbc_raw, cw_ref, L) + cb_ref[...])
    cnew_ref[0] = xbuf[tv:tv + SUBLANES, :]
    xbuf[0:SUBLANES, :] = xbuf[L:L + SUBLANES, :]

    xs = xbc[:, 0:SSD_WIDTH]
    gs = SSD_GROUPS * SSD_STATE
    bm = xbc[:, SSD_WIDTH:SSD_WIDTH + gs]
    cm = xbc[:, SSD_WIDTH + gs:SSD_WIDTH + 2 * gs]

    dt = _softplus(dt_raw + dtb_ref[...])
    if tv < L:
        rows = lax.broadcasted_iota(jnp.int32, (L, LANES), 0)
        dt = jnp.where(rows < tv, dt, 0.0)
    a = -jnp.exp(alog_ref[...])
    da = dt * a
    r_i = lax.broadcasted_iota(jnp.int32, (L, L), 0)
    c_i = lax.broadcasted_iota(jnp.int32, (L, L), 1)
    causal = c_i <= r_i
    tri = jnp.where(causal, 1.0, 0.0).astype(F32)
    acum = _fdot(tri, da)
    last = acum[L - 1:L, :]
    ea = jnp.exp(acum)
    wdec = jnp.exp(last - acum)

    def pad_rows(v):
        if L == LANES:
            return v
        return jnp.concatenate([v, jnp.zeros((LANES - L, v.shape[1]), v.dtype)], axis=0)

    acum_t = jnp.transpose(pad_rows(acum))

    def hilo(v):
        hi = v.astype(BF16)
        lo = (v - hi.astype(F32)).astype(BF16)
        return hi, lo

    parts = []
    for v in (dt, ea, wdec):
        parts.extend(hilo(v))
    stacked = jnp.concatenate(parts, axis=0)
    expanded = jnp.dot(stacked, e_ref[...], preferred_element_type=F32)
    dt_e = expanded[0:L] + expanded[L:2 * L]
    ea_e = expanded[2 * L:3 * L] + expanded[3 * L:4 * L]
    wd_e = expanded[4 * L:5 * L] + expanded[5 * L:6 * L]

    xdt = xs * dt_e
    xdt_b = xdt.astype(BF16)
    lane = lax.broadcasted_iota(jnp.int32, (L, LANES), 1)
    hpg = SSD_HEADS // SSD_GROUPS

    y_parts = []
    for g in range(SSD_GROUPS):
        bg = bm[:, g * SSD_STATE:(g + 1) * SSD_STATE]
        cg = cm[:, g * SSD_STATE:(g + 1) * SSD_STATE]
        cb = _bdot_nt(cg, bg)
        for j in range(hpg // 2):
            pair = g * (hpg // 2) + j
            xp = xdt_b[:, pair * LANES:(pair + 1) * LANES]
            ys = []
            for h in (2 * pair, 2 * pair + 1):
                seg = acum[:, h:h + 1] - acum_t[h:h + 1, 0:L]
                dec = jnp.exp(jnp.where(causal, seg, -jnp.inf))
                ys.append(_bdot(cb * dec, xp))
            y_parts.append(jnp.where(lane < SSD_HEAD_DIM, ys[0], ys[1]))
    y_in = jnp.concatenate(y_parts, axis=1)

    gw = hpg * SSD_HEAD_DIM
    y_st_parts = []
    for g in range(SSD_GROUPS):
        cg = cm[:, g * SSD_STATE:(g + 1) * SSD_STATE]
        y_st_parts.append(_bdot(cg, ht[g]))
    y_st = jnp.concatenate(y_st_parts, axis=1) * ea_e

    xw = (xdt * wd_e)
    for g in range(SSD_GROUPS):
        bg_t = jnp.transpose(pad_rows(bm[:, g * SSD_STATE:(g + 1) * SSD_STATE]))
        xw_g = pad_rows(xw[:, g * gw:(g + 1) * gw])
        ht[g] = ht[g] * ea_e[L - 1:L, g * gw:(g + 1) * gw] + _bdot(bg_t, xw_g)

    y = (y_in + y_st + dexp_ref[...] * xs) * _silu(z)
    outs = []
    for g in range(SSD_GROUPS):
        yg = y[:, g * gw:(g + 1) * gw]
        ms = jnp.mean(yg * yg, -1, keepdims=True)
        outs.append(yg * lax.rsqrt(ms + RMS_EPS) * nw_ref[:, g * gw:(g + 1) * gw])
    y_ref[0] = jnp.concatenate(outs, axis=1)

    @pl.when(c == nc - 1)
    def _():
        hout_ref[0] = ht[...]


def _ssd_scan(p_ssd, conv_prev8, h0t, cw, cb, dtb, alog, dexp, nw, emat, *, L, tv):
    b, t, _ = p_ssd.shape
    nc = t // L
    gw = SSD_WIDTH // SSD_GROUPS
    kern = functools.partial(_ssd_kernel, L=L, tv=tv)
    return pl.pallas_call(
        kern, grid=(b, nc),
        in_specs=[pl.BlockSpec((1, L, P_SSD), lambda i, c: (i, c, 0)),
                  pl.BlockSpec((1, SUBLANES, SSD_CONV_DIM), lambda i, c: (i, 0, 0)),
                  pl.BlockSpec((1, SSD_GROUPS, SSD_STATE, gw), lambda i, c: (i, 0, 0, 0)),
                  _const_spec((CONV_WIDTH, SSD_CONV_DIM)), _const_spec((1, SSD_CONV_DIM)),
                  _const_spec((1, LANES)), _const_spec((1, LANES)),
                  _const_spec((1, SSD_WIDTH)), _const_spec((1, SSD_WIDTH)),
                  _const_spec((LANES, SSD_WIDTH))],
        out_specs=[pl.BlockSpec((1, L, SSD_WIDTH), lambda i, c: (i, c, 0)),
                   pl.BlockSpec((1, SUBLANES, SSD_CONV_DIM), lambda i, c: (i, 0, 0)),
                   pl.BlockSpec((1, SSD_GROUPS, SSD_STATE, gw), lambda i, c: (i, 0, 0, 0))],
        out_shape=[jax.ShapeDtypeStruct((b, t, SSD_WIDTH), F32),
                   jax.ShapeDtypeStruct((b, SUBLANES, SSD_CONV_DIM), F32),
                   jax.ShapeDtypeStruct((b, SSD_GROUPS, SSD_STATE, gw), F32)],
        scratch_shapes=[pltpu.VMEM((L + SUBLANES, SSD_CONV_DIM), F32),
                        pltpu.VMEM((SSD_GROUPS, SSD_STATE, gw), F32)],
        compiler_params=_cparams(("parallel", "arbitrary")), name="ssd_scan",
    )(p_ssd, conv_prev8, h0t, cw, cb, dtb, alog, dexp, nw, emat)


def _gdn_chunk(p_ref, cprev_ref, s0_ref, cw_ref, bias_ref, alog_ref, nw_ref,
               y_ref, cnew_ref, sout_ref, xbuf, st, *, L, tv):
    c = pl.program_id(1)
    nc = pl.num_programs(1)
    H = GDN_HEADS
    D = GDN_HEAD_DIM
    R = H * L
    S = max(R, LANES)

    @pl.when(c == 0)
    def _():
        xbuf[0:SUBLANES, :] = cprev_ref[...]
        st[...] = s0_ref[...]

    qkv_raw = p_ref[:, 0:GDN_CONV_DIM]
    z = p_ref[:, GDN_CONV_DIM:GDN_CONV_DIM + GDN_WIDTH]
    ba = p_ref[:, GDN_CONV_DIM + GDN_WIDTH:P_GDN]

    qkv = _silu(_conv_chunk(xbuf, qkv_raw, cw_ref, L))
    cnew_ref[...] = xbuf[tv:tv + SUBLANES, :]
    xbuf[0:SUBLANES, :] = xbuf[L:L + SUBLANES, :]

    beta_f = 1.0 / (1.0 + jnp.exp(-ba))
    g_f = -jnp.exp(alog_ref[...]) * _softplus(ba + bias_ref[...])
    if tv < L:
        rows = lax.broadcasted_iota(jnp.int32, (L, LANES), 0)
        beta_f = jnp.where(rows < tv, beta_f, 0.0)
        g_f = jnp.where(rows < tv, g_f, 0.0)
    r_i = lax.broadcasted_iota(jnp.int32, (L, L), 0)
    c_i = lax.broadcasted_iota(jnp.int32, (L, L), 1)
    tri = jnp.where(c_i <= r_i, 1.0, 0.0).astype(F32)
    gcum_f = _fdot(tri, g_f)
    glast_f = jnp.broadcast_to(gcum_f[L - 1:L, :], (L, LANES))

    def pad_s(v):
        if R == S:
            return v
        return jnp.concatenate([v, jnp.zeros((S - R, v.shape[1]), v.dtype)], axis=0)

    def stack(v):
        return pad_s(jnp.concatenate([v[:, h * D:(h + 1) * D] for h in range(H)], axis=0))

    def col(v, off):
        return pad_s(jnp.concatenate([v[:, off + h:off + h + 1] for h in range(H)], axis=0))

    q_s = stack(qkv[:, 0:GDN_WIDTH])
    k_s = stack(qkv[:, GDN_WIDTH:2 * GDN_WIDTH])
    v_s = stack(qkv[:, 2 * GDN_WIDTH:3 * GDN_WIDTH])
    z_s = stack(z)
    q_s = q_s * lax.rsqrt(jnp.sum(q_s * q_s, -1, keepdims=True) + L2_EPS) * (D ** -0.5)
    k_s = k_s * lax.rsqrt(jnp.sum(k_s * k_s, -1, keepdims=True) + L2_EPS)
    beta = col(beta_f, 0)
    gcum = col(gcum_f, H)
    glast = col(glast_f, H)

    cmat = jnp.broadcast_to(gcum, (S, S))
    diff = cmat - jnp.transpose(cmat)
    rs = lax.broadcasted_iota(jnp.int32, (S, S), 0)
    cs = lax.broadcasted_iota(jnp.int32, (S, S), 1)
    if L & (L - 1) == 0:
        sh = L.bit_length() - 1
        same = (rs >> sh) == (cs >> sh)
    else:
        same = (rs // L) == (cs // L)
    incl = same & (cs <= rs)
    strict = same & (cs < rs)
    dec = jnp.exp(jnp.where(incl, diff, -jnp.inf))

    kb = k_s * beta
    n_mat = -jnp.where(strict, _bdot_nt(kb, k_s) * dec, 0.0)
    eye = jnp.where(rs == cs, 1.0, 0.0).astype(F32)
    t_mat = eye + n_mat
    npow = n_mat
    span = 2
    while span < L:
        npow = _bdot(npow, npow)
        t_mat = t_mat + _bdot(t_mat, npow)
        span *= 2

    eg = jnp.exp(gcum)
    t_b = t_mat.astype(BF16)
    u = _bdot(t_b, v_s * beta)
    w = _bdot(t_b, kb * eg)
    qg = q_s * eg
    ws = []
    qs_ = []
    for h in range(H):
        sh_b = st[h].astype(BF16)
        ws.append(_bdot(w[h * L:(h + 1) * L], sh_b))
        qs_.append(_bdot(qg[h * L:(h + 1) * L], sh_b))
    v_new = u - pad_s(jnp.concatenate(ws, axis=0))
    attn = jnp.where(incl, _bdot_nt(q_s, k_s) * dec, 0.0)
    o = pad_s(jnp.concatenate(qs_, axis=0)) + _bdot(attn, v_new)

    kd = k_s * jnp.exp(glast - gcum)
    kd_t = jnp.transpose(kd).astype(BF16)
    row_head = lax.broadcasted_iota(jnp.int32, (S, D), 0)
    eg_last = jnp.exp(glast)
    for h in range(H):
        vm = jnp.where((row_head >= h * L) & (row_head < (h + 1) * L), v_new, 0.0)
        st[h] = st[h] * eg_last[h * L:h * L + 1, :] + _bdot(kd_t, vm)

    ms = jnp.mean(o * o, -1, keepdims=True)
    o = o * lax.rsqrt(ms + RMS_EPS) * nw_ref[...] * _silu(z_s)
    y_ref[...] = jnp.concatenate([o[h * L:(h + 1) * L] for h in range(H)], axis=1)

    @pl.when(c == nc - 1)
    def _():
        sout_ref[...] = st[...]


def _gdn_kernel(p_ref, cprev_ref, s0_ref, cw_ref, bias_ref, alog_ref, nw_ref,
                y_ref, cnew_ref, sout_ref, xbuf, st, *, L, tv, nbat):
    for i in range(nbat):
        _gdn_chunk(p_ref.at[i], cprev_ref.at[i], s0_ref.at[i], cw_ref, bias_ref, alog_ref, nw_ref,
                   y_ref.at[i], cnew_ref.at[i], sout_ref.at[i], xbuf.at[i], st.at[i], L=L, tv=tv)


def _gdn_scan(p_gdn, conv_prev8, s0, cw, bias, alog, nw, *, L, tv):
    b, t, _ = p_gdn.shape
    nc = t // L
    nbat = GDN_BATCH_PER_STEP
    kern = functools.partial(_gdn_kernel, L=L, tv=tv, nbat=nbat)
    sshape = (nbat, GDN_HEADS, GDN_HEAD_DIM, GDN_HEAD_DIM)
    return pl.pallas_call(
        kern, grid=(b // nbat, nc),
        in_specs=[pl.BlockSpec((nbat, L, P_GDN), lambda i, c: (i, c, 0)),
                  pl.BlockSpec((nbat, SUBLANES, GDN_CONV_DIM), lambda i, c: (i, 0, 0)),
                  pl.BlockSpec(sshape, lambda i, c: (i, 0, 0, 0)),
                  _const_spec((CONV_WIDTH, GDN_CONV_DIM)),
                  _const_spec((1, LANES)), _const_spec((1, LANES)), _const_spec((1, GDN_HEAD_DIM))],
        out_specs=[pl.BlockSpec((nbat, L, GDN_WIDTH), lambda i, c: (i, c, 0)),
                   pl.BlockSpec((nbat, SUBLANES, GDN_CONV_DIM), lambda i, c: (i, 0, 0)),
                   pl.BlockSpec(sshape, lambda i, c: (i, 0, 0, 0))],
        out_shape=[jax.ShapeDtypeStruct((b, t, GDN_WIDTH), F32),
                   jax.ShapeDtypeStruct((b, SUBLANES, GDN_CONV_DIM), F32),
                   jax.ShapeDtypeStruct((b,) + sshape[1:], F32)],
        scratch_shapes=[pltpu.VMEM((nbat, L + SUBLANES, GDN_CONV_DIM), F32),
                        pltpu.VMEM(sshape, F32)],
        compiler_params=_cparams(("parallel", "arbitrary")), name="gdn_scan",
    )(p_gdn, conv_prev8, s0, cw, bias, alog, nw)


def _mla_prep_kernel(p_ref, cos_ref, sin_ref, qnw_ref, kvnw_ref, wuq_ref, wuk_ref,
                     q_ref, kcat_ref, kt_ref, ckv_ref, kr_ref, gate_ref, *, nb, tt):
    cq = p_ref[:, 0:MLA_Q_RANK]
    ckv = p_ref[:, MLA_Q_RANK:MLA_Q_RANK + MLA_KV_RANK]
    kr = p_ref[:, MLA_Q_RANK + MLA_KV_RANK:MLA_Q_RANK + MLA_KV_RANK + 2 * LANES]
    gate = p_ref[:, MLA_Q_RANK + MLA_KV_RANK + 2 * LANES:P_MLA]
    cos = cos_ref[...]
    sin = sin_ref[...]

    cqn = cq * lax.rsqrt(jnp.mean(cq * cq, -1, keepdims=True) + RMS_EPS) * qnw_ref[...]
    q = _bdot(cqn, wuq_ref[...])
    nope_w = MLA_HEADS * MLA_NOPE
    x1 = q[:, nope_w:nope_w + LANES]
    x2 = q[:, nope_w + LANES:nope_w + 2 * LANES]
    r_all = jnp.concatenate([x1 * cos - x2 * sin, x2 * cos + x1 * sin], axis=1) * MLA_SCALE

    ckvn = ckv * lax.rsqrt(jnp.mean(ckv * ckv, -1, keepdims=True) + RMS_EPS) * kvnw_ref[...]
    k1 = kr[:, 0:LANES]
    k2 = kr[:, LANES:2 * LANES]
    kr_rot = jnp.concatenate([k1 * cos - k2 * sin, k2 * cos + k1 * sin], axis=1)
    ckv_ref[...] = ckvn
    kr_ref[...] = kr_rot
    kcat_ref[...] = jnp.concatenate([ckvn, kr_rot], axis=1).astype(BF16)
    kt_ref[...] = jnp.transpose(ckvn).astype(BF16)
    gate_ref[...] = _silu(gate)

    lane = lax.broadcasted_iota(jnp.int32, (1, 2 * LANES), 1)
    head_of_lane = (lane & (LANES - 1)) >> 4
    for pair in range(MLA_HEADS // 2):
        qn = q[:, pair * LANES:(pair + 1) * LANES] * MLA_SCALE
        qlat = _bdot(qn, wuk_ref[pair])
        for i in range(2):
            h = 2 * pair + i
            ql = qlat[:, i * MLA_KV_RANK:(i + 1) * MLA_KV_RANK].astype(BF16)
            qr = jnp.where(head_of_lane == h, r_all, 0.0).astype(BF16)
            q_ref[:, h, :, 0:MLA_KV_RANK] = ql.reshape(nb, tt, MLA_KV_RANK)
            q_ref[:, h, :, MLA_KV_RANK:QK_WIDTH] = qr.reshape(nb, tt, 2 * LANES)


def _head_major_block(t, tm):
    if tm <= t:
        npb = t // tm
        return 1, tm, (lambda i: (i // npb, 0, i % npb, 0))
    return tm // t, t, (lambda i: (i, 0, 0, 0))


def _mla_prep(p_mla, cos_t, sin_t, qnw, kvnw, wuq, wuk_pairs, b, t, tm):
    m = p_mla.shape[0]
    nt = cos_t.shape[0] // tm
    row = lambda i: (i, 0)
    tab = lambda i: (i % nt, 0)
    nb, tt, hm_map = _head_major_block(t, tm)
    kern = functools.partial(_mla_prep_kernel, nb=nb, tt=tt)
    return pl.pallas_call(
        kern, grid=(m // tm,),
        in_specs=[pl.BlockSpec((tm, P_MLA), row), pl.BlockSpec((tm, LANES), tab), pl.BlockSpec((tm, LANES), tab),
                  _const_spec((1, MLA_Q_RANK)), _const_spec((1, MLA_KV_RANK)),
                  _const_spec(wuq.shape), _const_spec(wuk_pairs.shape)],
        out_specs=[pl.BlockSpec((nb, MLA_HEADS, tt, QK_WIDTH), hm_map), pl.BlockSpec((tm, QK_WIDTH), row),
                   pl.BlockSpec((MLA_KV_RANK, tm), lambda i: (0, i)),
                   pl.BlockSpec((tm, MLA_KV_RANK), row), pl.BlockSpec((tm, 2 * LANES), row),
                   pl.BlockSpec((tm, MLA_WIDTH), row)],
        out_shape=[jax.ShapeDtypeStruct((b, MLA_HEADS, t, QK_WIDTH), BF16),
                   jax.ShapeDtypeStruct((m, QK_WIDTH), BF16), jax.ShapeDtypeStruct((MLA_KV_RANK, m), BF16),
                   jax.ShapeDtypeStruct((m, MLA_KV_RANK), F32), jax.ShapeDtypeStruct((m, 2 * LANES), F32),
                   jax.ShapeDtypeStruct((m, MLA_WIDTH), F32)],
        compiler_params=_cparams(("parallel",)), name="mla_prep",
    )(p_mla, cos_t, sin_t, qnw, kvnw, wuq, wuk_pairs)


def _softmax_rows(s, m_scr, l_scr):
    n = s.shape[1]
    m_prev = m_scr[...]
    m_new = jnp.maximum(m_prev, jnp.max(s, -1, keepdims=True))
    alpha = jnp.exp(m_prev - m_new)
    m_wide = jnp.concatenate([m_new] * (n // LANES), axis=1) if n >= LANES else m_new[:, 0:n]
    p = jnp.exp(s - m_wide)
    l_scr[...] = alpha * l_scr[...] + jnp.sum(p, -1, keepdims=True)
    m_scr[...] = m_new
    return p, alpha


def _softmax_init(m_scr, l_scr, acc_scr):
    m_scr[...] = jnp.full(m_scr.shape, -jnp.inf, F32)
    l_scr[...] = jnp.zeros(l_scr.shape, F32)
    acc_scr[...] = jnp.zeros(acc_scr.shape, F32)


def _flash_kernel(q_ref, k_ref, kt_ref, o_ref, m_scr, l_scr, acc_scr, *, tq, tk):
    qi = pl.program_id(1)
    ki = pl.program_id(2)
    nk = pl.num_programs(2)
    rows = tq * MLA_HEADS

    @pl.when(ki == 0)
    def _():
        _softmax_init(m_scr, l_scr, acc_scr)

    def step(masked):
        q = q_ref[0].reshape(rows, QK_WIDTH)
        s_t = lax.dot_general(k_ref[0], q, (((1,), (1,)), ((), ())), preferred_element_type=F32)
        if masked:
            k_pos = ki * tk + lax.broadcasted_iota(jnp.int32, (tk, rows), 0)
            q_pos = qi * tq + (lax.broadcasted_iota(jnp.int32, (tk, rows), 1) & (tq - 1))
            s_t = jnp.where(k_pos <= q_pos, s_t, -jnp.inf)
        m_prev = m_scr[...]
        m_new = jnp.maximum(m_prev, jnp.max(s_t, axis=0, keepdims=True))
        alpha = jnp.exp(m_prev - m_new)
        p_t = jnp.exp(s_t - m_new)
        l_scr[...] = alpha * l_scr[...] + jnp.sum(p_t, axis=0, keepdims=True)
        m_scr[...] = m_new
        acc_scr[...] = acc_scr[...] * alpha + jnp.dot(kt_ref[...], p_t.astype(BF16), preferred_element_type=F32)

    needed = ki * tk < (qi + 1) * tq
    crosses = (ki + 1) * tk > qi * tq + 1

    @pl.when(needed & crosses)
    def _():
        step(True)

    @pl.when(needed & jnp.logical_not(crosses))
    def _():
        step(False)

    @pl.when(ki == nk - 1)
    def _():
        o_t = acc_scr[...] * (1.0 / l_scr[...])
        o_ref[0] = jnp.transpose(o_t).astype(o_ref.dtype).reshape(MLA_HEADS, tq, MLA_KV_RANK)


def _flash_attention(q, kcat, kt, *, tq, tk):
    b, _, t, _ = q.shape
    rows = tq * MLA_HEADS
    nkb = t // tk
    kern = functools.partial(_flash_kernel, tq=tq, tk=tk)

    def k_blk(qi, ki):
        return jnp.minimum(ki, ((qi + 1) * tq - 1) // tk)

    return pl.pallas_call(
        kern, grid=(b, t // tq, nkb),
        in_specs=[pl.BlockSpec((1, MLA_HEADS, tq, QK_WIDTH), lambda i, qi, ki: (i, 0, qi, 0)),
                  pl.BlockSpec((1, tk, QK_WIDTH), lambda i, qi, ki: (i, k_blk(qi, ki), 0)),
                  pl.BlockSpec((MLA_KV_RANK, tk), lambda i, qi, ki: (0, i * nkb + k_blk(qi, ki)))],
        out_specs=pl.BlockSpec((1, MLA_HEADS, tq, MLA_KV_RANK), lambda i, qi, ki: (i, 0, qi, 0)),
        out_shape=jax.ShapeDtypeStruct((b, MLA_HEADS, t, MLA_KV_RANK), BF16),
        scratch_shapes=[pltpu.VMEM((1, rows), F32), pltpu.VMEM((1, rows), F32),
                        pltpu.VMEM((MLA_KV_RANK, rows), F32)],
        compiler_params=_cparams(("parallel", "parallel", "arbitrary")), name="mla_flash",
    )(q, kcat, kt)


def _paged_kernel(pt_ref, q_ref, knew_ref, sel_ref, lat_hbm, rope_hbm, o_ref,
                  latbuf, ropebuf, sems, m_scr, l_scr, acc_scr, qr_scr, *, tpad, layer, ngroups):
    bi = pl.program_id(0)
    nb = pl.num_programs(0)
    npg = PAGES_PER_GROUP
    rows = tpad * MLA_HEADS
    reps = MLA_KV_RANK // LANES

    def copies(seq, grp, slot):
        out = []
        for i in range(npg):
            page = pt_ref[seq, grp * npg + i]
            dst = pl.ds(i * PAGE_SIZE, PAGE_SIZE)
            out.append(pltpu.make_async_copy(lat_hbm.at[layer, page], latbuf.at[slot, dst], sems.at[0, slot, i]))
            out.append(pltpu.make_async_copy(rope_hbm.at[layer, page], ropebuf.at[slot, dst], sems.at[1, slot, i]))
        return out

    def start(seq, grp, slot):
        for cp in copies(seq, grp, slot):
            cp.start()

    def wait(seq, grp, slot):
        for cp in copies(seq, grp, slot):
            cp.wait()

    @pl.when(bi == 0)
    def _():
        start(0, 0, 0)

    _softmax_init(m_scr, l_scr, acc_scr)
    q_all = q_ref[0].reshape(rows, QK_WIDTH)
    qr_scr[...] = jnp.dot(q_all[:, MLA_KV_RANK:QK_WIDTH], sel_ref[...], preferred_element_type=F32).astype(BF16)

    nt = (((1,), (1,)), ((), ()))

    def accumulate(s, v_b):
        p, alpha = _softmax_rows(s, m_scr, l_scr)
        acc_scr[...] = (acc_scr[...] * jnp.concatenate([alpha] * reps, axis=1)
                        + jnp.dot(p.astype(BF16), v_b, preferred_element_type=F32))

    def consume(grp, slot):
        wait(bi, grp, slot)
        lat_b = latbuf[slot].astype(BF16)
        rope_b = ropebuf[slot].astype(BF16)
        s = (lax.dot_general(q_all[:, 0:MLA_KV_RANK], lat_b, nt, preferred_element_type=F32)
             + lax.dot_general(qr_scr[...], rope_b, nt, preferred_element_type=F32))
        accumulate(s, lat_b)

    def pair(j, carry):
        g0 = 2 * j
        start(bi, g0 + 1, 1)
        consume(g0, 0)

        @pl.when(g0 + 2 < ngroups)
        def _():
            start(bi, g0 + 2, 0)

        @pl.when((g0 + 2 == ngroups) & (bi + 1 < nb))
        def _():
            start(bi + 1, 0, 0)

        consume(g0 + 1, 1)
        return carry

    lax.fori_loop(0, ngroups // 2, pair, 0)

    k_b = knew_ref[0]
    s2 = lax.dot_general(q_all, k_b, nt, preferred_element_type=F32)
    q_pos = lax.broadcasted_iota(jnp.int32, (rows, tpad), 0) & (tpad - 1)
    k_pos = lax.broadcasted_iota(jnp.int32, (rows, tpad), 1)
    accumulate(jnp.where(k_pos <= q_pos, s2, -jnp.inf), k_b[:, 0:MLA_KV_RANK])
    o = acc_scr[...] * jnp.concatenate([1.0 / l_scr[...]] * reps, axis=1)
    o_ref[0] = o.astype(o_ref.dtype).reshape(MLA_HEADS, tpad, MLA_KV_RANK)


def _paged_attention(q, knew, cache_lat, cache_rope, page_table, sel, *, tpad, layer):
    b = q.shape[0]
    n_pages = page_table.shape[1]
    npg = PAGES_PER_GROUP
    ngroups = n_pages // npg
    assert n_pages % (2 * npg) == 0, "page groups are consumed in slot pairs"
    rows = tpad * MLA_HEADS
    gk = npg * PAGE_SIZE
    kern = functools.partial(_paged_kernel, tpad=tpad, layer=layer, ngroups=ngroups)
    stat = pltpu.VMEM((rows, LANES), F32)
    grid_spec = pltpu.PrefetchScalarGridSpec(
        num_scalar_prefetch=1, grid=(b,),
        in_specs=[pl.BlockSpec((1, MLA_HEADS, tpad, QK_WIDTH), lambda bi, pt: (bi, 0, 0, 0)),
                  pl.BlockSpec((1, tpad, QK_WIDTH), lambda bi, pt: (bi, 0, 0)),
                  pl.BlockSpec(sel.shape, lambda bi, pt: (0, 0)),
                  pl.BlockSpec(memory_space=pl.ANY), pl.BlockSpec(memory_space=pl.ANY)],
        out_specs=pl.BlockSpec((1, MLA_HEADS, tpad, MLA_KV_RANK), lambda bi, pt: (bi, 0, 0, 0)),
        scratch_shapes=[pltpu.VMEM((2, gk, MLA_KV_RANK), F32), pltpu.VMEM((2, gk, MLA_ROPE), F32),
                        pltpu.SemaphoreType.DMA((2, 2, npg)),
                        stat, stat, pltpu.VMEM((rows, MLA_KV_RANK), F32),
                        pltpu.VMEM((rows, MLA_ROPE), BF16)])
    return pl.pallas_call(
        kern, grid_spec=grid_spec,
        out_shape=jax.ShapeDtypeStruct((b, MLA_HEADS, tpad, MLA_KV_RANK), BF16),
        compiler_params=_cparams(("arbitrary",)), name="mla_paged",
    )(page_table, q, knew, sel, cache_lat, cache_rope)


def _out_kernel(x_ref, yssd_ref, olat_ref, gate_ref, ygdn_ref, wuv_ref, wout_ref, g_ref, b_ref, o_ref, *, tm):
    y_mla = []
    for pair in range(MLA_HEADS // 2):
        o_pair = None
        for i in range(2):
            o_h = olat_ref[:, 2 * pair + i].reshape(tm, MLA_KV_RANK)
            part = jnp.dot(o_h, wuv_ref[pair, i * MLA_KV_RANK:(i + 1) * MLA_KV_RANK, :], preferred_element_type=F32)
            o_pair = part if o_pair is None else o_pair + part
        y_mla.append(o_pair * gate_ref[:, pair * LANES:(pair + 1) * LANES])
    y_mla = jnp.concatenate(y_mla, axis=1)
    mm = (_bdot(yssd_ref[...], wout_ref[0:SSD_WIDTH, :])
          + _bdot(y_mla, wout_ref[SSD_WIDTH:SSD_WIDTH + MLA_WIDTH, :])
          + _bdot(ygdn_ref[...], wout_ref[SSD_WIDTH + MLA_WIDTH:MIX_WIDTH, :]))
    v = DEEPNORM_ALPHA * x_ref[...] + mm
    mu = jnp.mean(v, -1, keepdims=True)
    vc = v - mu
    var = jnp.mean(vc * vc, -1, keepdims=True)
    o_ref[...] = vc * lax.rsqrt(var + LN_EPS) * g_ref[...] + b_ref[...]


def _out_proj(x, y_ssd, o_lat, gate, y_gdn, wuv_pairs, wout, g, b, tm):
    m = x.shape[0]
    t = o_lat.shape[2]
    row = lambda i: (i, 0)
    nb, tt, hm_map = _head_major_block(t, tm)
    return pl.pallas_call(
        functools.partial(_out_kernel, tm=tm), grid=(m // tm,),
        in_specs=[pl.BlockSpec((tm, D_MODEL), row), pl.BlockSpec((tm, SSD_WIDTH), row),
                  pl.BlockSpec((nb, MLA_HEADS, tt, MLA_KV_RANK), hm_map), pl.BlockSpec((tm, MLA_WIDTH), row),
                  pl.BlockSpec((tm, GDN_WIDTH), row),
                  _const_spec(wuv_pairs.shape), _const_spec(wout.shape),
                  _const_spec((1, D_MODEL)), _const_spec((1, D_MODEL))],
        out_specs=pl.BlockSpec((tm, D_MODEL), row),
        out_shape=jax.ShapeDtypeStruct((m, D_MODEL), F32),
        compiler_params=_cparams(("parallel",)), name="out_proj",
    )(x, y_ssd, o_lat, gate, y_gdn, wuv_pairs, wout, g, b)


def _pad_lanes(v, n=LANES):
    return jnp.pad(v, ((0, 0), (0, n - v.shape[1])))


def _prep_layer_weights(l, w_in, ssd_conv_w, ssd_conv_b, ssd_dt_bias, ssd_a_log, ssd_d, ssd_norm_w,
                        mla_q_norm_w, mla_w_uq, mla_kv_norm_w, mla_w_uk, mla_w_uv,
                        gdn_conv_w, gdn_dt_bias, gdn_a_log, gdn_norm_w, w_out, ln_g, ln_b):
    (w_z, w_xbc, w_dt, w_cq, w_ckv, w_kr, w_gate, w_qkv, w_gz, w_gb, w_ga) = jnp.split(w_in[l], IN_OFFSETS, axis=1)
    w_ssd = jnp.concatenate([w_z, w_xbc, _pad_lanes(w_dt)], axis=1).astype(BF16)
    kr_tiled = jnp.concatenate([jnp.tile(w_kr[:, :ROPE_HALF], (1, MLA_HEADS)),
                                jnp.tile(w_kr[:, ROPE_HALF:], (1, MLA_HEADS))], axis=1)
    w_mla = jnp.concatenate([w_cq, w_ckv, kr_tiled, w_gate], axis=1).astype(BF16)
    w_gdn = jnp.concatenate([w_qkv, w_gz, _pad_lanes(jnp.concatenate([w_gb, w_ga], axis=1))], axis=1).astype(BF16)

    uq = mla_w_uq[l].reshape(MLA_Q_RANK, MLA_HEADS, MLA_NOPE + MLA_ROPE)
    wuq = jnp.concatenate([uq[:, :, :MLA_NOPE].reshape(MLA_Q_RANK, -1),
                           uq[:, :, MLA_NOPE:MLA_NOPE + ROPE_HALF].reshape(MLA_Q_RANK, -1),
                           uq[:, :, MLA_NOPE + ROPE_HALF:].reshape(MLA_Q_RANK, -1)], axis=1).astype(BF16)
    uk = jnp.transpose(mla_w_uk[l], (1, 2, 0))
    uv = jnp.transpose(mla_w_uv[l], (1, 0, 2))
    zk = jnp.zeros((MLA_NOPE, MLA_KV_RANK), F32)
    zv = jnp.zeros((MLA_KV_RANK, MLA_V), F32)
    wuk_pairs = jnp.stack([jnp.block([[uk[2 * p], zk], [zk, uk[2 * p + 1]]]) for p in range(MLA_HEADS // 2)]).astype(BF16)
    wuv_pairs = jnp.stack([jnp.block([[uv[2 * p], zv], [zv, uv[2 * p + 1]]]) for p in range(MLA_HEADS // 2)]).astype(BF16)

    gdn_bias = jnp.pad(gdn_dt_bias[l][None, :], ((0, 0), (GDN_HEADS, LANES - 2 * GDN_HEADS)))
    gdn_alog = jnp.pad(gdn_a_log[l][None, :], ((0, 0), (GDN_HEADS, LANES - 2 * GDN_HEADS)))
    return dict(
        w_ssd=w_ssd, w_mla=w_mla, w_gdn=w_gdn,
        ssd_cw=ssd_conv_w[l], ssd_cb=ssd_conv_b[l][None, :],
        ssd_dtb=_pad_lanes(ssd_dt_bias[l][None, :]), ssd_alog=_pad_lanes(ssd_a_log[l][None, :]),
        ssd_dexp=jnp.repeat(ssd_d[l], SSD_HEAD_DIM)[None, :], ssd_nw=ssd_norm_w[l][None, :],
        qnw=mla_q_norm_w[l][None, :], kvnw=mla_kv_norm_w[l][None, :], wuq=wuq,
        wuk_pairs=wuk_pairs, wuv_pairs=wuv_pairs,
        gdn_cw=gdn_conv_w[l], gdn_bias=gdn_bias, gdn_alog=gdn_alog, gdn_nw=gdn_norm_w[l][None, :],
        w_out=w_out[l].astype(BF16), ln_g=ln_g[l][None, :], ln_b=ln_b[l][None, :])


def _rope_tables(pos):
    inv = ROPE_THETA ** (-jnp.arange(ROPE_HALF, dtype=F32) / ROPE_HALF)
    ang = pos.astype(F32)[:, None] * inv[None, :]
    return jnp.tile(jnp.cos(ang), (1, MLA_HEADS)), jnp.tile(jnp.sin(ang), (1, MLA_HEADS))


def _head_expand_matrix():
    e = np.zeros((LANES, SSD_WIDTH), np.float32)
    for h in range(SSD_HEADS):
        e[h, h * SSD_HEAD_DIM:(h + 1) * SSD_HEAD_DIM] = 1.0
    return jnp.asarray(e, BF16)


def _rope_select_matrix():
    s = np.zeros((2 * LANES, MLA_ROPE), np.float32)
    for j in range(2 * LANES):
        s[j, (j // LANES) * ROPE_HALF + (j % ROPE_HALF)] = 1.0
    return jnp.asarray(s, BF16)


def _tail8(conv_state):
    return jnp.pad(conv_state, ((0, 0), (SUBLANES - (CONV_WIDTH - 1), 0), (0, 0)))


def _ssd_state_to_kernel(s):
    b = s.shape[0]
    hpg = SSD_HEADS // SSD_GROUPS
    s = s.reshape(b, SSD_GROUPS, hpg, SSD_HEAD_DIM, SSD_STATE)
    return jnp.transpose(s, (0, 1, 4, 2, 3)).reshape(b, SSD_GROUPS, SSD_STATE, hpg * SSD_HEAD_DIM)


def _ssd_state_from_kernel(s):
    b = s.shape[0]
    hpg = SSD_HEADS // SSD_GROUPS
    s = s.reshape(b, SSD_GROUPS, SSD_STATE, hpg, SSD_HEAD_DIM)
    return jnp.transpose(s, (0, 1, 3, 4, 2)).reshape(b, SSD_HEADS, SSD_HEAD_DIM, SSD_STATE)


def _trunk(x, pos, tv, ssd_conv, ssd_state, gdn_conv, gdn_state, emb_g, emb_b, weights, cfg, paged=None):
    b, t, _ = x.shape
    m = b * t
    tm = cfg["tm"]
    cos_t, sin_t = _rope_tables(pos)
    if cos_t.shape[0] < tm:
        reps = tm // cos_t.shape[0]
        cos_t, sin_t = jnp.tile(cos_t, (reps, 1)), jnp.tile(sin_t, (reps, 1))
    emat = _head_expand_matrix()
    sel = _rope_select_matrix()
    h = _layer_norm(x.reshape(m, D_MODEL), emb_g, emb_b, tm)
    new_states = []
    for l in range(DEPTH):
        w = weights[l]
        p_ssd = _project(h, w["w_ssd"], tm, "proj_ssd").reshape(b, t, P_SSD)
        p_mla = _project(h, w["w_mla"], tm, "proj_mla")
        p_gdn = _project(h, w["w_gdn"], tm, "proj_gdn").reshape(b, t, P_GDN)

        y_ssd, ssd_c8, ssd_ht = _ssd_scan(
            p_ssd, _tail8(ssd_conv[l]), _ssd_state_to_kernel(ssd_state[l]),
            w["ssd_cw"], w["ssd_cb"], w["ssd_dtb"], w["ssd_alog"], w["ssd_dexp"], w["ssd_nw"], emat,
            L=cfg["ssd_chunk"], tv=min(tv, cfg["ssd_chunk"]))
        y_gdn, gdn_c8, gdn_s = _gdn_scan(
            p_gdn, _tail8(gdn_conv[l]), gdn_state[l], w["gdn_cw"], w["gdn_bias"], w["gdn_alog"], w["gdn_nw"],
            L=cfg["gdn_chunk"], tv=min(tv, cfg["gdn_chunk"]))

        q, kcat, kt, ckv, kr_t, gate = _mla_prep(p_mla, cos_t, sin_t, w["qnw"], w["kvnw"], w["wuq"], w["wuk_pairs"],
                                             b, t, tm)
        kcat = kcat.reshape(b, t, QK_WIDTH)
        if paged is None:
            o_lat = _flash_attention(q, kcat, kt, tq=cfg["tq"], tk=cfg["tk"])
        else:
            cache_lat, cache_rope, page_table = paged
            o_lat = _paged_attention(q, kcat, cache_lat, cache_rope, page_table, sel, tpad=t, layer=l)

        h = _out_proj(h, y_ssd.reshape(m, SSD_WIDTH), o_lat, gate, y_gdn.reshape(m, GDN_WIDTH),
                      w["wuv_pairs"], w["w_out"], w["ln_g"], w["ln_b"], tm)

        kr = jnp.concatenate([kr_t[:, 0:ROPE_HALF], kr_t[:, LANES:LANES + ROPE_HALF]], axis=1)
        sl = slice(SUBLANES - (CONV_WIDTH - 1), SUBLANES)
        new_states.append((ckv.reshape(b, t, MLA_KV_RANK), kr.reshape(b, t, MLA_ROPE),
                           ssd_c8[:, sl], _ssd_state_from_kernel(ssd_ht), gdn_c8[:, sl], gdn_s))
    return h.reshape(b, t, D_MODEL), tuple(jnp.stack(s) for s in zip(*new_states))


def kernel(x_prompt, x_sample, cache_kv_latent, cache_k_rope, state_ssd_conv, state_ssd, state_gdn_conv, state_gdn, page_table, emb_ln_g, emb_ln_b, w_in, ssd_conv_w, ssd_conv_b, ssd_dt_bias, ssd_a_log, ssd_d, ssd_norm_w, mla_q_norm_w, mla_w_uq, mla_kv_norm_w, mla_w_uk, mla_w_uv, gdn_conv_w, gdn_dt_bias, gdn_a_log, gdn_norm_w, w_out, ln_g, ln_b):
    weights = [_prep_layer_weights(l, w_in, ssd_conv_w, ssd_conv_b, ssd_dt_bias, ssd_a_log, ssd_d, ssd_norm_w,
                                   mla_q_norm_w, mla_w_uq, mla_kv_norm_w, mla_w_uk, mla_w_uv,
                                   gdn_conv_w, gdn_dt_bias, gdn_a_log, gdn_norm_w, w_out, ln_g, ln_b)
               for l in range(DEPTH)]

    bp, tp, _ = x_prompt.shape
    zeros = lambda *s: jnp.zeros(s, F32)
    ssd_chunk = SSD_CHUNK if tp % SSD_CHUNK == 0 else tp
    gdn_chunk = GDN_CHUNK if tp % GDN_CHUNK == 0 else tp
    cfg_p = dict(tm=min(512, bp * tp), ssd_chunk=ssd_chunk, gdn_chunk=gdn_chunk,
                 tq=min(128, tp), tk=min(512, tp))
    y_prompt, st_p = _trunk(
        x_prompt, jnp.arange(tp), tp,
        zeros(DEPTH, bp, CONV_WIDTH - 1, SSD_CONV_DIM), zeros(DEPTH, bp, SSD_HEADS, SSD_HEAD_DIM, SSD_STATE),
        zeros(DEPTH, bp, CONV_WIDTH - 1, GDN_CONV_DIM), zeros(DEPTH, bp, GDN_HEADS, GDN_HEAD_DIM, GDN_HEAD_DIM),
        emb_ln_g, emb_ln_b, weights, cfg_p)

    bs, ts, _ = x_sample.shape
    tpad = -(-ts // BF16_ROWS) * BF16_ROWS
    past_len = page_table.shape[1] * PAGE_SIZE
    xs = jnp.pad(x_sample, ((0, 0), (0, tpad - ts), (0, 0)))
    cfg_s = dict(tm=min(512, bs * tpad), ssd_chunk=tpad, gdn_chunk=tpad, tq=tpad, tk=tpad)
    y_s, st_s = _trunk(
        xs, past_len + jnp.arange(tpad), ts, state_ssd_conv, state_ssd, state_gdn_conv, state_gdn,
        emb_ln_g, emb_ln_b, weights, cfg_s, paged=(cache_kv_latent, cache_k_rope, page_table))
    y_sample = y_s[:, :ts]
    s_lat, s_rope, s_ssd_conv, s_ssd, s_gdn_conv, s_gdn = st_s
    return (y_prompt, y_sample) + tuple(st_p) + (s_lat[:, :, :ts], s_rope[:, :, :ts], s_ssd_conv, s_ssd, s_gdn_conv, s_gdn)
```

```python
import functools

import jax
import jax.numpy as jnp
import numpy as np
from jax import lax
from jax.experimental import pallas as pl
from jax.experimental.pallas import tpu as pltpu

F32 = jnp.float32
BF16 = jnp.bfloat16

D_MODEL = 1024
CONV_WIDTH = 4
SSD_HEADS = 16
SSD_HEAD_DIM = 64
SSD_WIDTH = SSD_HEADS * SSD_HEAD_DIM
SSD_GROUPS = 2
SSD_STATE = 128
SSD_CONV_DIM = SSD_WIDTH + 2 * SSD_GROUPS * SSD_STATE
SSD_CHUNK = 128
MLA_HEADS = 8
MLA_NOPE = 64
MLA_ROPE = 32
MLA_V = 64
MLA_WIDTH = MLA_HEADS * MLA_V
MLA_Q_RANK = 384
MLA_KV_RANK = 256
MLA_SCALE = (MLA_NOPE + MLA_ROPE) ** -0.5
ROPE_THETA = 10000.0
GDN_HEADS = 4
GDN_HEAD_DIM = 128
GDN_WIDTH = GDN_HEADS * GDN_HEAD_DIM
GDN_CONV_DIM = 3 * GDN_WIDTH
GDN_CHUNK = 64
MIX_WIDTH = SSD_WIDTH + MLA_WIDTH + GDN_WIDTH
IN_SIZES = (SSD_WIDTH, SSD_CONV_DIM, SSD_HEADS, MLA_Q_RANK, MLA_KV_RANK, MLA_ROPE, MLA_WIDTH,
            GDN_CONV_DIM, GDN_WIDTH, GDN_HEADS, GDN_HEADS)
IN_OFFSETS = tuple(int(o) for o in np.cumsum(IN_SIZES)[:-1])
DEPTH = 2
DEEPNORM_ALPHA = (2 * DEPTH) ** 0.25
LN_EPS = 1e-5
RMS_EPS = 1e-6
L2_EPS = 1e-6
PAGE_SIZE = 128

LANES = 128
SUBLANES = 8
BF16_ROWS = 16
ROPE_HALF = MLA_ROPE // 2
P_SSD = SSD_WIDTH + SSD_CONV_DIM + LANES
P_MLA = MLA_Q_RANK + MLA_KV_RANK + 2 * LANES + MLA_WIDTH
P_GDN = GDN_CONV_DIM + GDN_WIDTH + LANES
QK_WIDTH = MLA_KV_RANK + 2 * LANES
VMEM_LIMIT = 56 * 1024 * 1024
PAGES_PER_GROUP = 32
GDN_BATCH_PER_STEP = 2


def _cparams(sem):
    return pltpu.CompilerParams(dimension_semantics=sem, vmem_limit_bytes=VMEM_LIMIT)


def _bdot(a, b):
    return jnp.dot(a.astype(BF16), b.astype(BF16), preferred_element_type=F32)


def _bdot_nt(a, b):
    return lax.dot_general(a.astype(BF16), b.astype(BF16), (((1,), (1,)), ((), ())),
                           preferred_element_type=F32)


def _fdot(a, b):
    return jnp.dot(a, b, precision=lax.Precision.HIGHEST, preferred_element_type=F32)


def _silu(x):
    return x * (1.0 / (1.0 + jnp.exp(-x)))


def _softplus(x):
    return jnp.maximum(x, 0.0) + jnp.log1p(jnp.exp(-jnp.abs(x)))


def _const_spec(shape):
    nd = len(shape)
    return pl.BlockSpec(shape, lambda *_: (0,) * nd)


def _ln_kernel(x_ref, g_ref, b_ref, o_ref):
    x = x_ref[...]
    mu = jnp.mean(x, -1, keepdims=True)
    xc = x - mu
    var = jnp.mean(xc * xc, -1, keepdims=True)
    o_ref[...] = xc * lax.rsqrt(var + LN_EPS) * g_ref[...] + b_ref[...]


def _layer_norm(x, g, b, tm):
    m, d = x.shape
    return pl.pallas_call(
        _ln_kernel, grid=(m // tm,),
        in_specs=[pl.BlockSpec((tm, d), lambda i: (i, 0)), _const_spec((1, d)), _const_spec((1, d))],
        out_specs=pl.BlockSpec((tm, d), lambda i: (i, 0)),
        out_shape=jax.ShapeDtypeStruct((m, d), F32),
        compiler_params=_cparams(("parallel",)), name="emb_ln")(x, g.reshape(1, d), b.reshape(1, d))


def _proj_kernel(x_ref, w_ref, o_ref):
    o_ref[...] = _bdot(x_ref[...], w_ref[...])


def _project(x, w, tm, name):
    m, k = x.shape
    n = w.shape[1]
    return pl.pallas_call(
        _proj_kernel, grid=(m // tm,),
        in_specs=[pl.BlockSpec((tm, k), lambda i: (i, 0)), _const_spec((k, n))],
        out_specs=pl.BlockSpec((tm, n), lambda i: (i, 0)),
        out_shape=jax.ShapeDtypeStruct((m, n), F32),
        compiler_params=_cparams(("parallel",)), name=name)(x, w)


def _conv_chunk(xbuf, x_new, cw_ref, L):
    xbuf[SUBLANES:SUBLANES + L, :] = x_new
    y = cw_ref[CONV_WIDTH - 1:CONV_WIDTH, :] * x_new
    for k in range(CONV_WIDTH - 1):
        off = SUBLANES - (CONV_WIDTH - 1) + k
        y = y + cw_ref[k:k + 1, :] * xbuf[off:off + L, :]
    return y


def _ssd_kernel(p_ref, cprev_ref, h0_ref, cw_ref, cb_ref, dtb_ref, alog_ref, dexp_ref, nw_ref, e_ref,
                y_ref, cnew_ref, hout_ref, xbuf, ht, *, L, tv):
    c = pl.program_id(1)
    nc = pl.num_programs(1)

    @pl.when(c == 0)
    def _():
        xbuf[0:SUBLANES, :] = cprev_ref[0]
        ht[...] = h0_ref[0]

    z = p_ref[0, :, 0:SSD_WIDTH]
    xbc_raw = p_ref[0, :, SSD_WIDTH:SSD_WIDTH + SSD_CONV_DIM]
    dt_raw = p_ref[0, :, SSD_WIDTH + SSD_CONV_DIM:P_SSD]

    xbc = _silu(_conv_chunk(xbuf, xbc_raw, cw_ref, L) + cb_ref[...])
    cnew_ref[0] = xbuf[tv:tv + SUBLANES, :]
    xbuf[0:SUBLANES, :] = xbuf[L:L + SUBLANES, :]

    xs = xbc[:, 0:SSD_WIDTH]
    gs = SSD_GROUPS * SSD_STATE
    bm = xbc[:, SSD_WIDTH:SSD_WIDTH + gs]
    cm = xbc[:, SSD_WIDTH + gs:SSD_WIDTH + 2 * gs]

    dt = _softplus(dt_raw + dtb_ref[...])
    if tv < L:
        rows = lax.broadcasted_iota(jnp.int32, (L, LANES), 0)
        dt = jnp.where(rows < tv, dt, 0.0)
    a = -jnp.exp(alog_ref[...])
    da = dt * a
    r_i = lax.broadcasted_iota(jnp.int32, (L, L), 0)
    c_i = lax.broadcasted_iota(jnp.int32, (L, L), 1)
    causal = c_i <= r_i
    tri = jnp.where(causal, 1.0, 0.0).astype(F32)
    acum = _fdot(tri, da)
    last = acum[L - 1:L, :]
    ea = jnp.exp(acum)
    wdec = jnp.exp(last - acum)

    def pad_rows(v):
        if L == LANES:
            return v
        return jnp.concatenate([v, jnp.zeros((LANES - L, v.shape[1]), v.dtype)], axis=0)

    acum_t = jnp.transpose(pad_rows(acum))

    def hilo(v):
        hi = v.astype(BF16)
        lo = (v - hi.astype(F32)).astype(BF16)
        return hi, lo

    parts = []
    for v in (dt, ea, wdec):
        parts.extend(hilo(v))
    stacked = jnp.concatenate(parts, axis=0)
    expanded = jnp.dot(stacked, e_ref[...], preferred_element_type=F32)
    dt_e = expanded[0:L] + expanded[L:2 * L]
    ea_e = expanded[2 * L:3 * L] + expanded[3 * L:4 * L]
    wd_e = expanded[4 * L:5 * L] + expanded[5 * L:6 * L]

    xdt = xs * dt_e
    xdt_b = xdt.astype(BF16)
    lane = lax.broadcasted_iota(jnp.int32, (L, LANES), 1)
    hpg = SSD_HEADS // SSD_GROUPS

    y_parts = []
    for g in range(SSD_GROUPS):
        bg = bm[:, g * SSD_STATE:(g + 1) * SSD_STATE]
        cg = cm[:, g * SSD_STATE:(g + 1) * SSD_STATE]
        cb = _bdot_nt(cg, bg)
        for j in range(hpg // 2):
            pair = g * (hpg // 2) + j
            xp = xdt_b[:, pair * LANES:(pair + 1) * LANES]
            ys = []
            for h in (2 * pair, 2 * pair + 1):
                seg = acum[:, h:h + 1] - acum_t[h:h + 1, 0:L]
                dec = jnp.exp(jnp.where(causal, seg, -jnp.inf))
                ys.append(_bdot(cb * dec, xp))
            y_parts.append(jnp.where(lane < SSD_HEAD_DIM, ys[0], ys[1]))
    y_in = jnp.concatenate(y_parts, axis=1)

    gw = hpg * SSD_HEAD_DIM
    y_st_parts = []
    for g in range(SSD_GROUPS):
        cg = cm[:, g * SSD_STATE:(g + 1) * SSD_STATE]
        y_st_parts.append(_bdot(cg, ht[g]))
    y_st = jnp.concatenate(y_st_parts, axis=1) * ea_e

    xw = (xdt * wd_e)
    for g in range(SSD_GROUPS):
        bg_t = jnp.transpose(pad_rows(bm[:, g * SSD_STATE:(g + 1) * SSD_STATE]))
        xw_g = pad_rows(xw[:, g * gw:(g + 1) * gw])
        ht[g] = ht[g] * ea_e[L - 1:L, g * gw:(g + 1) * gw] + _bdot(bg_t, xw_g)

    y = (y_in + y_st + dexp_ref[...] * xs) * _silu(z)
    outs = []
    for g in range(SSD_GROUPS):
        yg = y[:, g * gw:(g + 1) * gw]
        ms = jnp.mean(yg * yg, -1, keepdims=True)
        outs.append(yg * lax.rsqrt(ms + RMS_EPS) * nw_ref[:, g * gw:(g + 1) * gw])
    y_ref[0] = jnp.concatenate(outs, axis=1).astype(y_ref.dtype)

    @pl.when(c == nc - 1)
    def _():
        hout_ref[0] = ht[...]


def _ssd_scan(p_ssd, conv_prev8, h0t, cw, cb, dtb, alog, dexp, nw, emat, *, L, tv):
    b, t, _ = p_ssd.shape
    nc = t // L
    gw = SSD_WIDTH // SSD_GROUPS
    kern = functools.partial(_ssd_kernel, L=L, tv=tv)
    return pl.pallas_call(
        kern, grid=(b, nc),
        in_specs=[pl.BlockSpec((1, L, P_SSD), lambda i, c: (i, c, 0)),
                  pl.BlockSpec((1, SUBLANES, SSD_CONV_DIM), lambda i, c: (i, 0, 0)),
                  pl.BlockSpec((1, SSD_GROUPS, SSD_STATE, gw), lambda i, c: (i, 0, 0, 0)),
                  _const_spec((CONV_WIDTH, SSD_CONV_DIM)), _const_spec((1, SSD_CONV_DIM)),
                  _const_spec((1, LANES)), _const_spec((1, LANES)),
                  _const_spec((1, SSD_WIDTH)), _const_spec((1, SSD_WIDTH)),
                  _const_spec((LANES, SSD_WIDTH))],
        out_specs=[pl.BlockSpec((1, L, SSD_WIDTH), lambda i, c: (i, c, 0)),
                   pl.BlockSpec((1, SUBLANES, SSD_CONV_DIM), lambda i, c: (i, 0, 0)),
                   pl.BlockSpec((1, SSD_GROUPS, SSD_STATE, gw), lambda i, c: (i, 0, 0, 0))],
        out_shape=[jax.ShapeDtypeStruct((b, t, SSD_WIDTH), BF16),
                   jax.ShapeDtypeStruct((b, SUBLANES, SSD_CONV_DIM), F32),
                   jax.ShapeDtypeStruct((b, SSD_GROUPS, SSD_STATE, gw), F32)],
        scratch_shapes=[pltpu.VMEM((L + SUBLANES, SSD_CONV_DIM), F32),
                        pltpu.VMEM((SSD_GROUPS, SSD_STATE, gw), F32)],
        compiler_params=_cparams(("parallel", "arbitrary")), name="ssd_scan",
    )(p_ssd, conv_prev8, h0t, cw, cb, dtb, alog, dexp, nw, emat)


def _gdn_chunk(p_ref, cw_ref, bias_ref, alog_ref, nw_ref, y_ref, cnew_ref, xbuf, st, *, L, tv):
    H = GDN_HEADS
    D = GDN_HEAD_DIM
    R = H * L
    S = max(R, LANES)

    qkv_raw = p_ref[:, 0:GDN_CONV_DIM]
    z = p_ref[:, GDN_CONV_DIM:GDN_CONV_DIM + GDN_WIDTH]
    ba = p_ref[:, GDN_CONV_DIM + GDN_WIDTH:P_GDN]

    qkv = _silu(_conv_chunk(xbuf, qkv_raw, cw_ref, L))
    cnew_ref[...] = xbuf[tv:tv + SUBLANES, :]
    xbuf[0:SUBLANES, :] = xbuf[L:L + SUBLANES, :]
    yield

    beta_f = 1.0 / (1.0 + jnp.exp(-ba))
    g_f = -jnp.exp(alog_ref[...]) * _softplus(ba + bias_ref[...])
    if tv < L:
        rows = lax.broadcasted_iota(jnp.int32, (L, LANES), 0)
        beta_f = jnp.where(rows < tv, beta_f, 0.0)
        g_f = jnp.where(rows < tv, g_f, 0.0)
    r_i = lax.broadcasted_iota(jnp.int32, (L, L), 0)
    c_i = lax.broadcasted_iota(jnp.int32, (L, L), 1)
    tri = jnp.where(c_i <= r_i, 1.0, 0.0).astype(F32)
    gcum_f = _fdot(tri, g_f)
    glast_f = jnp.broadcast_to(gcum_f[L - 1:L, :], (L, LANES))

    def pad_s(v):
        if R == S:
            return v
        return jnp.concatenate([v, jnp.zeros((S - R, v.shape[1]), v.dtype)], axis=0)

    def stack(v):
        return pad_s(jnp.concatenate([v[:, h * D:(h + 1) * D] for h in range(H)], axis=0))

    def col(v, off):
        return pad_s(jnp.concatenate([v[:, off + h:off + h + 1] for h in range(H)], axis=0))

    q_s = stack(qkv[:, 0:GDN_WIDTH])
    k_s = stack(qkv[:, GDN_WIDTH:2 * GDN_WIDTH])
    v_s = stack(qkv[:, 2 * GDN_WIDTH:3 * GDN_WIDTH])
    z_s = stack(z)
    q_s = q_s * lax.rsqrt(jnp.sum(q_s * q_s, -1, keepdims=True) + L2_EPS) * (D ** -0.5)
    k_s = k_s * lax.rsqrt(jnp.sum(k_s * k_s, -1, keepdims=True) + L2_EPS)
    beta = col(beta_f, 0)
    gcum = col(gcum_f, H)
    glast = col(glast_f, H)
    yield

    cmat = jnp.broadcast_to(gcum, (S, S))
    diff = cmat - jnp.transpose(cmat)
    rs = lax.broadcasted_iota(jnp.int32, (S, S), 0)
    cs = lax.broadcasted_iota(jnp.int32, (S, S), 1)
    if L & (L - 1) == 0:
        sh = L.bit_length() - 1
        same = (rs >> sh) == (cs >> sh)
    else:
        same = (rs // L) == (cs // L)
    incl = same & (cs <= rs)
    strict = same & (cs < rs)
    dec = jnp.exp(jnp.where(incl, diff, -jnp.inf))

    kb = k_s * beta
    n_mat = -jnp.where(strict, _bdot_nt(kb, k_s) * dec, 0.0)
    eye = jnp.where(rs == cs, 1.0, 0.0).astype(F32)
    t_mat = eye + n_mat
    npow = n_mat
    yield
    span = 2
    while span < L:
        npow = _bdot(npow, npow)
        yield
        t_mat = t_mat + _bdot(t_mat, npow)
        yield
        span *= 2

    eg = jnp.exp(gcum)
    t_b = t_mat.astype(BF16)
    u = _bdot(t_b, v_s * beta)
    w = _bdot(t_b, kb * eg)
    qg = q_s * eg
    attn = jnp.where(incl, _bdot_nt(q_s, k_s) * dec, 0.0)
    yield
    ws = []
    qs_ = []
    for h in range(H):
        sh_b = st[h].astype(BF16)
        ws.append(_bdot(w[h * L:(h + 1) * L], sh_b))
        qs_.append(_bdot(qg[h * L:(h + 1) * L], sh_b))
    v_new = u - pad_s(jnp.concatenate(ws, axis=0))
    yield
    o = pad_s(jnp.concatenate(qs_, axis=0)) + _bdot(attn, v_new)

    kd = k_s * jnp.exp(glast - gcum)
    kd_t = jnp.transpose(kd).astype(BF16)
    row_head = lax.broadcasted_iota(jnp.int32, (S, D), 0)
    eg_last = jnp.exp(glast)
    for h in range(H):
        vm = jnp.where((row_head >= h * L) & (row_head < (h + 1) * L), v_new, 0.0)
        st[h] = st[h] * eg_last[h * L:h * L + 1, :] + _bdot(kd_t, vm)
    yield

    ms = jnp.mean(o * o, -1, keepdims=True)
    o = o * lax.rsqrt(ms + RMS_EPS) * nw_ref[...] * _silu(z_s)
    y_ref[...] = jnp.concatenate([o[h * L:(h + 1) * L] for h in range(H)], axis=1).astype(y_ref.dtype)


def _gdn_kernel(p_ref, cprev_ref, s0_ref, cw_ref, bias_ref, alog_ref, nw_ref,
                y_ref, cnew_ref, sout_ref, xbuf, st, *, L, tv, nbat):
    c = pl.program_id(1)
    nc = pl.num_programs(1)

    @pl.when(c == 0)
    def _():
        xbuf[:, 0:SUBLANES, :] = cprev_ref[...]
        st[...] = s0_ref[...]

    live = [_gdn_chunk(p_ref.at[i], cw_ref, bias_ref, alog_ref, nw_ref, y_ref.at[i], cnew_ref.at[i],
                       xbuf.at[i], st.at[i], L=L, tv=tv) for i in range(nbat)]
    while live:
        live = [g for g in live if next(g, StopIteration) is not StopIteration]

    @pl.when(c == nc - 1)
    def _():
        sout_ref[...] = st[...]


def _gdn_scan(p_gdn, conv_prev8, s0, cw, bias, alog, nw, *, L, tv):
    b, t, _ = p_gdn.shape
    nc = t // L
    nbat = GDN_BATCH_PER_STEP
    assert b % nbat == 0
    kern = functools.partial(_gdn_kernel, L=L, tv=tv, nbat=nbat)
    sshape = (nbat, GDN_HEADS, GDN_HEAD_DIM, GDN_HEAD_DIM)
    return pl.pallas_call(
        kern, grid=(b // nbat, nc),
        in_specs=[pl.BlockSpec((nbat, L, P_GDN), lambda i, c: (i, c, 0)),
                  pl.BlockSpec((nbat, SUBLANES, GDN_CONV_DIM), lambda i, c: (i, 0, 0)),
                  pl.BlockSpec(sshape, lambda i, c: (i, 0, 0, 0)),
                  _const_spec((CONV_WIDTH, GDN_CONV_DIM)),
                  _const_spec((1, LANES)), _const_spec((1, LANES)), _const_spec((1, GDN_HEAD_DIM))],
        out_specs=[pl.BlockSpec((nbat, L, GDN_WIDTH), lambda i, c: (i, c, 0)),
                   pl.BlockSpec((nbat, SUBLANES, GDN_CONV_DIM), lambda i, c: (i, 0, 0)),
                   pl.BlockSpec(sshape, lambda i, c: (i, 0, 0, 0))],
        out_shape=[jax.ShapeDtypeStruct((b, t, GDN_WIDTH), BF16),
                   jax.ShapeDtypeStruct((b, SUBLANES, GDN_CONV_DIM), F32),
                   jax.ShapeDtypeStruct((b,) + sshape[1:], F32)],
        scratch_shapes=[pltpu.VMEM((nbat, L + SUBLANES, GDN_CONV_DIM), F32),
                        pltpu.VMEM(sshape, F32)],
        compiler_params=_cparams(("parallel", "arbitrary")), name="gdn_scan",
    )(p_gdn, conv_prev8, s0, cw, bias, alog, nw)


def _mla_prep_kernel(p_ref, cos_ref, sin_ref, qnw_ref, kvnw_ref, wuq_ref, wuk_ref,
                     q_ref, kcat_ref, kt_ref, ckv_ref, kr_ref, gate_ref, *, nb, tt, tq_t):
    cq = p_ref[:, 0:MLA_Q_RANK]
    ckv = p_ref[:, MLA_Q_RANK:MLA_Q_RANK + MLA_KV_RANK]
    kr = p_ref[:, MLA_Q_RANK + MLA_KV_RANK:MLA_Q_RANK + MLA_KV_RANK + 2 * LANES]
    gate = p_ref[:, MLA_Q_RANK + MLA_KV_RANK + 2 * LANES:P_MLA]
    cos = cos_ref[...]
    sin = sin_ref[...]

    cqn = cq * lax.rsqrt(jnp.mean(cq * cq, -1, keepdims=True) + RMS_EPS) * qnw_ref[...]
    q = _bdot(cqn, wuq_ref[...])
    nope_w = MLA_HEADS * MLA_NOPE
    x1 = q[:, nope_w:nope_w + LANES]
    x2 = q[:, nope_w + LANES:nope_w + 2 * LANES]
    r_all = jnp.concatenate([x1 * cos - x2 * sin, x2 * cos + x1 * sin], axis=1) * MLA_SCALE

    ckvn = ckv * lax.rsqrt(jnp.mean(ckv * ckv, -1, keepdims=True) + RMS_EPS) * kvnw_ref[...]
    k1 = kr[:, 0:LANES]
    k2 = kr[:, LANES:2 * LANES]
    kr_rot = jnp.concatenate([k1 * cos - k2 * sin, k2 * cos + k1 * sin], axis=1)
    ckv_ref[...] = ckvn
    kr_ref[...] = kr_rot
    kcat_ref[...] = jnp.concatenate([ckvn, kr_rot], axis=1).astype(BF16)
    kt_ref[...] = jnp.transpose(ckvn).astype(BF16)
    gate_ref[...] = _silu(gate)

    tm = q.shape[0]
    if tq_t is None:
        lane = lax.broadcasted_iota(jnp.int32, (1, 2 * LANES), 1)
        head_of_lane = (lane & (LANES - 1)) >> 4
    else:
        r_all_t = jnp.transpose(r_all)
        row_i = lax.broadcasted_iota(jnp.int32, (2 * LANES, 1), 0)
        head_of_row = (row_i & (LANES - 1)) >> 4
    for pair in range(MLA_HEADS // 2):
        qn = q[:, pair * LANES:(pair + 1) * LANES] * MLA_SCALE
        qlat = _bdot(qn, wuk_ref[pair])
        for i in range(2):
            h = 2 * pair + i
            ql = qlat[:, i * MLA_KV_RANK:(i + 1) * MLA_KV_RANK]
            if tq_t is None:
                qr = jnp.where(head_of_lane == h, r_all, 0.0).astype(BF16)
                q_ref[:, h, :, 0:MLA_KV_RANK] = ql.astype(BF16).reshape(nb, tt, MLA_KV_RANK)
                q_ref[:, h, :, MLA_KV_RANK:QK_WIDTH] = qr.reshape(nb, tt, 2 * LANES)
            else:
                ql_t = jnp.transpose(ql).astype(BF16)
                qr_t = jnp.where(head_of_row == h, r_all_t, 0.0).astype(BF16)
                for qb in range(tm // tq_t):
                    cols = slice(h * tq_t, (h + 1) * tq_t)
                    toks = slice(qb * tq_t, (qb + 1) * tq_t)
                    q_ref[0, qb, 0:MLA_KV_RANK, cols] = ql_t[:, toks]
                    q_ref[0, qb, MLA_KV_RANK:QK_WIDTH, cols] = qr_t[:, toks]


def _head_major_block(t, tm):
    if tm <= t:
        npb = t // tm
        return 1, tm, (lambda i: (i // npb, 0, i % npb, 0))
    return tm // t, t, (lambda i: (i, 0, 0, 0))


def _mla_prep(p_mla, cos_t, sin_t, qnw, kvnw, wuq, wuk_pairs, b, t, tm, tq_t=None):
    m = p_mla.shape[0]
    nt = cos_t.shape[0] // tm
    row = lambda i: (i, 0)
    tab = lambda i: (i % nt, 0)
    nb, tt, hm_map = _head_major_block(t, tm)
    kern = functools.partial(_mla_prep_kernel, nb=nb, tt=tt, tq_t=tq_t)
    if tq_t is None:
        q_spec = pl.BlockSpec((nb, MLA_HEADS, tt, QK_WIDTH), hm_map)
        q_shape = jax.ShapeDtypeStruct((b, MLA_HEADS, t, QK_WIDTH), BF16)
    else:
        npb = t // tm
        q_spec = pl.BlockSpec((1, tm // tq_t, QK_WIDTH, MLA_HEADS * tq_t), lambda i: (i // npb, i % npb, 0, 0))
        q_shape = jax.ShapeDtypeStruct((b, t // tq_t, QK_WIDTH, MLA_HEADS * tq_t), BF16)
    return pl.pallas_call(
        kern, grid=(m // tm,),
        in_specs=[pl.BlockSpec((tm, P_MLA), row), pl.BlockSpec((tm, LANES), tab), pl.BlockSpec((tm, LANES), tab),
                  _const_spec((1, MLA_Q_RANK)), _const_spec((1, MLA_KV_RANK)),
                  _const_spec(wuq.shape), _const_spec(wuk_pairs.shape)],
        out_specs=[q_spec, pl.BlockSpec((tm, QK_WIDTH), row),
                   pl.BlockSpec((MLA_KV_RANK, tm), lambda i: (0, i)),
                   pl.BlockSpec((tm, MLA_KV_RANK), row), pl.BlockSpec((tm, 2 * LANES), row),
                   pl.BlockSpec((tm, MLA_WIDTH), row)],
        out_shape=[q_shape,
                   jax.ShapeDtypeStruct((m, QK_WIDTH), BF16), jax.ShapeDtypeStruct((MLA_KV_RANK, m), BF16),
                   jax.ShapeDtypeStruct((m, MLA_KV_RANK), F32), jax.ShapeDtypeStruct((m, 2 * LANES), F32),
                   jax.ShapeDtypeStruct((m, MLA_WIDTH), F32)],
        compiler_params=_cparams(("parallel",)), name="mla_prep",
    )(p_mla, cos_t, sin_t, qnw, kvnw, wuq, wuk_pairs)


def _softmax_rows(s, m_scr, l_scr):
    n = s.shape[1]
    m_prev = m_scr[...]
    m_new = jnp.maximum(m_prev, jnp.max(s, -1, keepdims=True))
    alpha = jnp.exp(m_prev - m_new)
    m_wide = jnp.concatenate([m_new] * (n // LANES), axis=1) if n >= LANES else m_new[:, 0:n]
    p = jnp.exp(s - m_wide)
    l_scr[...] = alpha * l_scr[...] + jnp.sum(p, -1, keepdims=True)
    m_scr[...] = m_new
    return p, alpha


def _softmax_init(m_scr, l_scr, acc_scr):
    m_scr[...] = jnp.full(m_scr.shape, -jnp.inf, F32)
    l_scr[...] = jnp.zeros(l_scr.shape, F32)
    acc_scr[...] = jnp.zeros(acc_scr.shape, F32)


def _flash_kernel(qi_ref, ki_ref, q_ref, k_ref, kt_ref, o_ref, m_scr, l_scr, acc_scr, *, tq, tk):
    step_i = pl.program_id(1)
    qi = qi_ref[step_i]
    ki = ki_ref[step_i]
    rows = tq * MLA_HEADS

    @pl.when(ki == 0)
    def _():
        _softmax_init(m_scr, l_scr, acc_scr)

    def step(masked):
        s_t = jnp.dot(k_ref[0], q_ref[0, 0], preferred_element_type=F32)
        if masked:
            k_pos = ki * tk + lax.broadcasted_iota(jnp.int32, (tk, rows), 0)
            q_pos = qi * tq + (lax.broadcasted_iota(jnp.int32, (tk, rows), 1) & (tq - 1))
            s_t = jnp.where(k_pos <= q_pos, s_t, -jnp.inf)
        m_prev = m_scr[...]
        m_new = jnp.maximum(m_prev, jnp.max(s_t, axis=0, keepdims=True))
        alpha = jnp.exp(m_prev - m_new)
        p_t = jnp.exp(s_t - m_new)
        l_scr[...] = alpha * l_scr[...] + jnp.sum(p_t, axis=0, keepdims=True)
        m_scr[...] = m_new
        acc_scr[...] = acc_scr[...] * alpha + jnp.dot(kt_ref[...], p_t.astype(BF16), preferred_element_type=F32)

    crosses = (ki + 1) * tk > qi * tq + 1

    @pl.when(crosses)
    def _():
        step(True)

    @pl.when(jnp.logical_not(crosses))
    def _():
        step(False)

    @pl.when(ki == ((qi + 1) * tq - 1) // tk)
    def _():
        o_t = acc_scr[...] * (1.0 / l_scr[...])
        o_ref[0] = jnp.transpose(o_t).astype(o_ref.dtype).reshape(MLA_HEADS, tq, MLA_KV_RANK)


def _flash_attention(q_t, kcat, kt, *, tq, tk):
    b, nq, _, rows = q_t.shape
    t = nq * tq
    nkb = t // tk
    kern = functools.partial(_flash_kernel, tq=tq, tk=tk)
    pairs = [(qi, ki) for qi in range(nq) for ki in range(((qi + 1) * tq - 1) // tk + 1)]
    qi_arr = jnp.asarray([p[0] for p in pairs], jnp.int32)
    ki_arr = jnp.asarray([p[1] for p in pairs], jnp.int32)
    grid_spec = pltpu.PrefetchScalarGridSpec(
        num_scalar_prefetch=2, grid=(b, len(pairs)),
        in_specs=[pl.BlockSpec((1, 1, QK_WIDTH, rows), lambda i, s, qa, ka: (i, qa[s], 0, 0)),
                  pl.BlockSpec((1, tk, QK_WIDTH), lambda i, s, qa, ka: (i, ka[s], 0)),
                  pl.BlockSpec((MLA_KV_RANK, tk), lambda i, s, qa, ka: (0, i * nkb + ka[s]))],
        out_specs=pl.BlockSpec((1, MLA_HEADS, tq, MLA_KV_RANK), lambda i, s, qa, ka: (i, 0, qa[s], 0)),
        scratch_shapes=[pltpu.VMEM((1, rows), F32), pltpu.VMEM((1, rows), F32),
                        pltpu.VMEM((MLA_KV_RANK, rows), F32)])
    return pl.pallas_call(
        kern, grid_spec=grid_spec,
        out_shape=jax.ShapeDtypeStruct((b, MLA_HEADS, t, MLA_KV_RANK), BF16),
        compiler_params=_cparams(("parallel", "arbitrary")), name="mla_flash",
    )(qi_arr, ki_arr, q_t, kcat, kt)


def _paged_kernel(pt_ref, q_ref, knew_ref, sel_ref, rowsel_ref, rowselt_ref, lat_hbm, rope_hbm, o_ref,
                  latbuf, ropebuf, sems, m_scr, l_scr, acc_scr, *, tpad, tv, layer, ngroups):
    bi = pl.program_id(0)
    nb = pl.num_programs(0)
    npg = PAGES_PER_GROUP
    rows = tpad * MLA_HEADS
    rows_v = tv * MLA_HEADS
    reps = MLA_KV_RANK // LANES

    def copies(seq, grp, slot):
        out = []
        for i in range(npg):
            page = pt_ref[seq, grp * npg + i]
            dst = pl.ds(i * PAGE_SIZE, PAGE_SIZE)
            out.append(pltpu.make_async_copy(lat_hbm.at[layer, page], latbuf.at[slot, dst], sems.at[0, slot, i]))
            out.append(pltpu.make_async_copy(rope_hbm.at[layer, page], ropebuf.at[slot, :, dst], sems.at[1, slot, i]))
        return out

    def start(seq, grp, slot):
        for cp in copies(seq, grp, slot):
            cp.start()

    def wait(seq, grp, slot):
        for cp in copies(seq, grp, slot):
            cp.wait()

    @pl.when(bi == 0)
    def _():
        start(0, 0, 0)

    _softmax_init(m_scr, l_scr, acc_scr)
    q_v = jnp.dot(rowsel_ref[...], q_ref[0].reshape(rows, QK_WIDTH), preferred_element_type=F32).astype(BF16)
    q_lat = q_v[:, 0:MLA_KV_RANK]
    q_rd = jnp.dot(q_v[:, MLA_KV_RANK:QK_WIDTH], sel_ref[...], preferred_element_type=F32).astype(BF16)

    nt = (((1,), (1,)), ((), ()))

    def accumulate(s, v_b):
        p, alpha = _softmax_rows(s, m_scr, l_scr)
        acc_scr[...] = (acc_scr[...] * jnp.concatenate([alpha] * reps, axis=1)
                        + jnp.dot(p.astype(BF16), v_b, preferred_element_type=F32))

    def consume(grp, slot):
        wait(bi, grp, slot)
        lat_b = latbuf[slot].astype(BF16)
        rope_b = ropebuf[slot].astype(BF16)
        s = (lax.dot_general(q_lat, lat_b, nt, preferred_element_type=F32)
             + jnp.dot(q_rd, rope_b, preferred_element_type=F32))
        accumulate(s, lat_b)

    def pair(j, carry):
        g0 = 2 * j
        start(bi, g0 + 1, 1)
        consume(g0, 0)

        @pl.when(g0 + 2 < ngroups)
        def _():
            start(bi, g0 + 2, 0)

        @pl.when((g0 + 2 == ngroups) & (bi + 1 < nb))
        def _():
            start(bi + 1, 0, 0)

        consume(g0 + 1, 1)
        return carry

    lax.fori_loop(0, ngroups // 2, pair, 0)

    k_b = knew_ref[0]
    s2 = lax.dot_general(q_v, k_b, nt, preferred_element_type=F32)
    q_pos = lax.rem(lax.broadcasted_iota(jnp.int32, (rows_v, tpad), 0), tv)
    k_pos = lax.broadcasted_iota(jnp.int32, (rows_v, tpad), 1)
    accumulate(jnp.where(k_pos <= q_pos, s2, -jnp.inf), k_b[:, 0:MLA_KV_RANK])
    o_v = (acc_scr[...] * jnp.concatenate([1.0 / l_scr[...]] * reps, axis=1)).astype(BF16)
    o = jnp.dot(rowselt_ref[...], o_v, preferred_element_type=F32)
    o_ref[0] = o.astype(o_ref.dtype).reshape(MLA_HEADS, tpad, MLA_KV_RANK)


def _row_select_matrix(tpad, tv):
    s = np.zeros((MLA_HEADS * tv, MLA_HEADS * tpad), np.float32)
    for h in range(MLA_HEADS):
        for t in range(tv):
            s[h * tv + t, h * tpad + t] = 1.0
    return s


def _paged_attention(q, knew, cache_lat, cache_rope_t, page_table, sel, *, tpad, tv, layer):
    b = q.shape[0]
    n_pages = page_table.shape[1]
    npg = PAGES_PER_GROUP
    ngroups = n_pages // npg
    assert n_pages % (2 * npg) == 0, "page groups are consumed in slot pairs"
    rows = tpad * MLA_HEADS
    rows_v = tv * MLA_HEADS
    gk = npg * PAGE_SIZE
    rowsel = _row_select_matrix(tpad, tv)
    kern = functools.partial(_paged_kernel, tpad=tpad, tv=tv, layer=layer, ngroups=ngroups)
    stat = pltpu.VMEM((rows_v, LANES), F32)
    grid_spec = pltpu.PrefetchScalarGridSpec(
        num_scalar_prefetch=1, grid=(b,),
        in_specs=[pl.BlockSpec((1, MLA_HEADS, tpad, QK_WIDTH), lambda bi, pt: (bi, 0, 0, 0)),
                  pl.BlockSpec((1, tpad, QK_WIDTH), lambda bi, pt: (bi, 0, 0)),
                  pl.BlockSpec(sel.shape, lambda bi, pt: (0, 0)),
                  pl.BlockSpec((rows_v, rows), lambda bi, pt: (0, 0)),
                  pl.BlockSpec((rows, rows_v), lambda bi, pt: (0, 0)),
                  pl.BlockSpec(memory_space=pl.ANY), pl.BlockSpec(memory_space=pl.ANY)],
        out_specs=pl.BlockSpec((1, MLA_HEADS, tpad, MLA_KV_RANK), lambda bi, pt: (bi, 0, 0, 0)),
        scratch_shapes=[pltpu.VMEM((2, gk, MLA_KV_RANK), F32), pltpu.VMEM((2, MLA_ROPE, gk), F32),
                        pltpu.SemaphoreType.DMA((2, 2, npg)),
                        stat, stat, pltpu.VMEM((rows_v, MLA_KV_RANK), F32)])
    return pl.pallas_call(
        kern, grid_spec=grid_spec,
        out_shape=jax.ShapeDtypeStruct((b, MLA_HEADS, tpad, MLA_KV_RANK), BF16),
        compiler_params=_cparams(("arbitrary",)), name="mla_paged",
    )(page_table, q, knew, sel, jnp.asarray(rowsel, BF16), jnp.asarray(rowsel.T, BF16), cache_lat, cache_rope_t)


def _out_kernel(x_ref, yssd_ref, olat_ref, gate_ref, ygdn_ref, wuv_ref, wout_ref, g_ref, b_ref, o_ref, *, tm):
    y_mla = []
    for pair in range(MLA_HEADS // 2):
        o_pair = None
        for i in range(2):
            o_h = olat_ref[:, 2 * pair + i].reshape(tm, MLA_KV_RANK)
            part = jnp.dot(o_h, wuv_ref[pair, i * MLA_KV_RANK:(i + 1) * MLA_KV_RANK, :], preferred_element_type=F32)
            o_pair = part if o_pair is None else o_pair + part
        y_mla.append(o_pair * gate_ref[:, pair * LANES:(pair + 1) * LANES])
    y_mla = jnp.concatenate(y_mla, axis=1)
    mm = (_bdot(yssd_ref[...], wout_ref[0:SSD_WIDTH, :])
          + _bdot(y_mla, wout_ref[SSD_WIDTH:SSD_WIDTH + MLA_WIDTH, :])
          + _bdot(ygdn_ref[...], wout_ref[SSD_WIDTH + MLA_WIDTH:MIX_WIDTH, :]))
    v = DEEPNORM_ALPHA * x_ref[...] + mm
    mu = jnp.mean(v, -1, keepdims=True)
    vc = v - mu
    var = jnp.mean(vc * vc, -1, keepdims=True)
    o_ref[...] = vc * lax.rsqrt(var + LN_EPS) * g_ref[...] + b_ref[...]


def _out_proj(x, y_ssd, o_lat, gate, y_gdn, wuv_pairs, wout, g, b, tm):
    m = x.shape[0]
    t = o_lat.shape[2]
    row = lambda i: (i, 0)
    nb, tt, hm_map = _head_major_block(t, tm)
    return pl.pallas_call(
        functools.partial(_out_kernel, tm=tm), grid=(m // tm,),
        in_specs=[pl.BlockSpec((tm, D_MODEL), row), pl.BlockSpec((tm, SSD_WIDTH), row),
                  pl.BlockSpec((nb, MLA_HEADS, tt, MLA_KV_RANK), hm_map), pl.BlockSpec((tm, MLA_WIDTH), row),
                  pl.BlockSpec((tm, GDN_WIDTH), row),
                  _const_spec(wuv_pairs.shape), _const_spec(wout.shape),
                  _const_spec((1, D_MODEL)), _const_spec((1, D_MODEL))],
        out_specs=pl.BlockSpec((tm, D_MODEL), row),
        out_shape=jax.ShapeDtypeStruct((m, D_MODEL), F32),
        compiler_params=_cparams(("parallel",)), name="out_proj",
    )(x, y_ssd, o_lat, gate, y_gdn, wuv_pairs, wout, g, b)


def _pad_lanes(v, n=LANES):
    return jnp.pad(v, ((0, 0), (0, n - v.shape[1])))


def _prep_layer_weights(l, w_in, ssd_conv_w, ssd_conv_b, ssd_dt_bias, ssd_a_log, ssd_d, ssd_norm_w,
                        mla_q_norm_w, mla_w_uq, mla_kv_norm_w, mla_w_uk, mla_w_uv,
                        gdn_conv_w, gdn_dt_bias, gdn_a_log, gdn_norm_w, w_out, ln_g, ln_b):
    (w_z, w_xbc, w_dt, w_cq, w_ckv, w_kr, w_gate, w_qkv, w_gz, w_gb, w_ga) = jnp.split(w_in[l], IN_OFFSETS, axis=1)
    w_ssd = jnp.concatenate([w_z, w_xbc, _pad_lanes(w_dt)], axis=1).astype(BF16)
    kr_tiled = jnp.concatenate([jnp.tile(w_kr[:, :ROPE_HALF], (1, MLA_HEADS)),
                                jnp.tile(w_kr[:, ROPE_HALF:], (1, MLA_HEADS))], axis=1)
    w_mla = jnp.concatenate([w_cq, w_ckv, kr_tiled, w_gate], axis=1).astype(BF16)
    w_gdn = jnp.concatenate([w_qkv, w_gz, _pad_lanes(jnp.concatenate([w_gb, w_ga], axis=1))], axis=1).astype(BF16)

    uq = mla_w_uq[l].reshape(MLA_Q_RANK, MLA_HEADS, MLA_NOPE + MLA_ROPE)
    wuq = jnp.concatenate([uq[:, :, :MLA_NOPE].reshape(MLA_Q_RANK, -1),
                           uq[:, :, MLA_NOPE:MLA_NOPE + ROPE_HALF].reshape(MLA_Q_RANK, -1),
                           uq[:, :, MLA_NOPE + ROPE_HALF:].reshape(MLA_Q_RANK, -1)], axis=1).astype(BF16)
    uk = jnp.transpose(mla_w_uk[l], (1, 2, 0))
    uv = jnp.transpose(mla_w_uv[l], (1, 0, 2))
    zk = jnp.zeros((MLA_NOPE, MLA_KV_RANK), F32)
    zv = jnp.zeros((MLA_KV_RANK, MLA_V), F32)
    wuk_pairs = jnp.stack([jnp.block([[uk[2 * p], zk], [zk, uk[2 * p + 1]]]) for p in range(MLA_HEADS // 2)]).astype(BF16)
    wuv_pairs = jnp.stack([jnp.block([[uv[2 * p], zv], [zv, uv[2 * p + 1]]]) for p in range(MLA_HEADS // 2)]).astype(BF16)

    gdn_bias = jnp.pad(gdn_dt_bias[l][None, :], ((0, 0), (GDN_HEADS, LANES - 2 * GDN_HEADS)))
    gdn_alog = jnp.pad(gdn_a_log[l][None, :], ((0, 0), (GDN_HEADS, LANES - 2 * GDN_HEADS)))
    return dict(
        w_ssd=w_ssd, w_mla=w_mla, w_gdn=w_gdn,
        ssd_cw=ssd_conv_w[l], ssd_cb=ssd_conv_b[l][None, :],
        ssd_dtb=_pad_lanes(ssd_dt_bias[l][None, :]), ssd_alog=_pad_lanes(ssd_a_log[l][None, :]),
        ssd_dexp=jnp.repeat(ssd_d[l], SSD_HEAD_DIM)[None, :], ssd_nw=ssd_norm_w[l][None, :],
        qnw=mla_q_norm_w[l][None, :], kvnw=mla_kv_norm_w[l][None, :], wuq=wuq,
        wuk_pairs=wuk_pairs, wuv_pairs=wuv_pairs,
        gdn_cw=gdn_conv_w[l], gdn_bias=gdn_bias, gdn_alog=gdn_alog, gdn_nw=gdn_norm_w[l][None, :],
        w_out=w_out[l].astype(BF16), ln_g=ln_g[l][None, :], ln_b=ln_b[l][None, :])


def _rope_tables(pos):
    inv = ROPE_THETA ** (-jnp.arange(ROPE_HALF, dtype=F32) / ROPE_HALF)
    ang = pos.astype(F32)[:, None] * inv[None, :]
    return jnp.tile(jnp.cos(ang), (1, MLA_HEADS)), jnp.tile(jnp.sin(ang), (1, MLA_HEADS))


def _head_expand_matrix():
    e = np.zeros((LANES, SSD_WIDTH), np.float32)
    for h in range(SSD_HEADS):
        e[h, h * SSD_HEAD_DIM:(h + 1) * SSD_HEAD_DIM] = 1.0
    return jnp.asarray(e, BF16)


def _rope_select_matrix():
    s = np.zeros((2 * LANES, MLA_ROPE), np.float32)
    for j in range(2 * LANES):
        s[j, (j // LANES) * ROPE_HALF + (j % ROPE_HALF)] = 1.0
    return jnp.asarray(s, BF16)


def _tail8(conv_state):
    return jnp.pad(conv_state, ((0, 0), (SUBLANES - (CONV_WIDTH - 1), 0), (0, 0)))


def _ssd_state_to_kernel(s):
    b = s.shape[0]
    hpg = SSD_HEADS // SSD_GROUPS
    s = s.reshape(b, SSD_GROUPS, hpg, SSD_HEAD_DIM, SSD_STATE)
    return jnp.transpose(s, (0, 1, 4, 2, 3)).reshape(b, SSD_GROUPS, SSD_STATE, hpg * SSD_HEAD_DIM)


def _ssd_state_from_kernel(s):
    b = s.shape[0]
    hpg = SSD_HEADS // SSD_GROUPS
    s = s.reshape(b, SSD_GROUPS, SSD_STATE, hpg, SSD_HEAD_DIM)
    return jnp.transpose(s, (0, 1, 3, 4, 2)).reshape(b, SSD_HEADS, SSD_HEAD_DIM, SSD_STATE)


def _trunk(x, pos, tv, ssd_conv, ssd_state, gdn_conv, gdn_state, emb_g, emb_b, weights, cfg, paged=None):
    b, t, _ = x.shape
    m = b * t
    tm = cfg["tm"]
    cos_t, sin_t = _rope_tables(pos)
    if cos_t.shape[0] < tm:
        reps = tm // cos_t.shape[0]
        cos_t, sin_t = jnp.tile(cos_t, (reps, 1)), jnp.tile(sin_t, (reps, 1))
    emat = _head_expand_matrix()
    sel = _rope_select_matrix()
    h = _layer_norm(x.reshape(m, D_MODEL), emb_g, emb_b, tm)
    new_states = []
    for l in range(DEPTH):
        w = weights[l]
        tmp = cfg["tm_proj"]
        p_ssd = _project(h, w["w_ssd"], tmp, "proj_ssd").reshape(b, t, P_SSD)
        p_mla = _project(h, w["w_mla"], tmp, "proj_mla")
        p_gdn = _project(h, w["w_gdn"], tmp, "proj_gdn").reshape(b, t, P_GDN)

        y_ssd, ssd_c8, ssd_ht = _ssd_scan(
            p_ssd, _tail8(ssd_conv[l]), _ssd_state_to_kernel(ssd_state[l]),
            w["ssd_cw"], w["ssd_cb"], w["ssd_dtb"], w["ssd_alog"], w["ssd_dexp"], w["ssd_nw"], emat,
            L=cfg["ssd_chunk"], tv=min(tv, cfg["ssd_chunk"]))
        y_gdn, gdn_c8, gdn_s = _gdn_scan(
            p_gdn, _tail8(gdn_conv[l]), gdn_state[l], w["gdn_cw"], w["gdn_bias"], w["gdn_alog"], w["gdn_nw"],
            L=cfg["gdn_chunk"], tv=min(tv, cfg["gdn_chunk"]))

        q, kcat, kt, ckv, kr_t, gate = _mla_prep(p_mla, cos_t, sin_t, w["qnw"], w["kvnw"], w["wuq"], w["wuk_pairs"],
                                             b, t, tm, tq_t=cfg["tq"] if paged is None else None)
        kcat = kcat.reshape(b, t, QK_WIDTH)
        if paged is None:
            o_lat = _flash_attention(q, kcat, kt, tq=cfg["tq"], tk=cfg["tk"])
        else:
            cache_lat, cache_rope_t, page_table = paged
            o_lat = _paged_attention(q, kcat, cache_lat, cache_rope_t, page_table, sel, tpad=t, tv=tv, layer=l)

        h = _out_proj(h, y_ssd.reshape(m, SSD_WIDTH), o_lat, gate, y_gdn.reshape(m, GDN_WIDTH),
                      w["wuv_pairs"], w["w_out"], w["ln_g"], w["ln_b"], tm)

        kr = jnp.concatenate([kr_t[:, 0:ROPE_HALF], kr_t[:, LANES:LANES + ROPE_HALF]], axis=1)
        sl = slice(SUBLANES - (CONV_WIDTH - 1), SUBLANES)
        new_states.append((ckv.reshape(b, t, MLA_KV_RANK), kr.reshape(b, t, MLA_ROPE),
                           ssd_c8[:, sl], _ssd_state_from_kernel(ssd_ht), gdn_c8[:, sl], gdn_s))
    return h.reshape(b, t, D_MODEL), tuple(jnp.stack(s) for s in zip(*new_states))


def kernel(x_prompt, x_sample, cache_kv_latent, cache_k_rope, state_ssd_conv, state_ssd, state_gdn_conv, state_gdn, page_table, emb_ln_g, emb_ln_b, w_in, ssd_conv_w, ssd_conv_b, ssd_dt_bias, ssd_a_log, ssd_d, ssd_norm_w, mla_q_norm_w, mla_w_uq, mla_kv_norm_w, mla_w_uk, mla_w_uv, gdn_conv_w, gdn_dt_bias, gdn_a_log, gdn_norm_w, w_out, ln_g, ln_b):
    weights = [_prep_layer_weights(l, w_in, ssd_conv_w, ssd_conv_b, ssd_dt_bias, ssd_a_log, ssd_d, ssd_norm_w,
                                   mla_q_norm_w, mla_w_uq, mla_kv_norm_w, mla_w_uk, mla_w_uv,
                                   gdn_conv_w, gdn_dt_bias, gdn_a_log, gdn_norm_w, w_out, ln_g, ln_b)
               for l in range(DEPTH)]

    bp, tp, _ = x_prompt.shape
    zeros = lambda *s: jnp.zeros(s, F32)
    ssd_chunk = SSD_CHUNK if tp % SSD_CHUNK == 0 else tp
    gdn_chunk = GDN_CHUNK if tp % GDN_CHUNK == 0 else tp
    cfg_p = dict(tm=min(512, bp * tp), tm_proj=min(1024, bp * tp), ssd_chunk=ssd_chunk, gdn_chunk=gdn_chunk,
                 tq=min(128, tp), tk=min(512, tp))
    y_prompt, st_p = _trunk(
        x_prompt, jnp.arange(tp), tp,
        zeros(DEPTH, bp, CONV_WIDTH - 1, SSD_CONV_DIM), zeros(DEPTH, bp, SSD_HEADS, SSD_HEAD_DIM, SSD_STATE),
        zeros(DEPTH, bp, CONV_WIDTH - 1, GDN_CONV_DIM), zeros(DEPTH, bp, GDN_HEADS, GDN_HEAD_DIM, GDN_HEAD_DIM),
        emb_ln_g, emb_ln_b, weights, cfg_p)

    bs, ts, _ = x_sample.shape
    tpad = -(-ts // BF16_ROWS) * BF16_ROWS
    past_len = page_table.shape[1] * PAGE_SIZE
    xs = jnp.pad(x_sample, ((0, 0), (0, tpad - ts), (0, 0)))
    cfg_s = dict(tm=min(512, bs * tpad), tm_proj=min(512, bs * tpad), ssd_chunk=tpad, gdn_chunk=tpad,
                 tq=tpad, tk=tpad)
    y_s, st_s = _trunk(
        xs, past_len + jnp.arange(tpad), ts, state_ssd_conv, state_ssd, state_gdn_conv, state_gdn,
        emb_ln_g, emb_ln_b, weights, cfg_s,
        paged=(cache_kv_latent, jnp.swapaxes(cache_k_rope, 2, 3), page_table))
    y_sample = y_s[:, :ts]
    s_lat, s_rope, s_ssd_conv, s_ssd, s_gdn_conv, s_gdn = st_s
    return (y_prompt, y_sample) + tuple(st_p) + (s_lat[:, :, :ts], s_rope[:, :, :ts], s_ssd_conv, s_ssd, s_gdn_conv, s_gdn)
```

```python
import functools
import math

import jax
import jax.numpy as jnp
import numpy as np
from jax import lax
from jax.experimental import pallas as pl
from jax.experimental.pallas import tpu as pltpu

F32 = jnp.float32
BF16 = jnp.bfloat16

D_MODEL = 1024
CONV_WIDTH = 4
SSD_HEADS = 16
SSD_HEAD_DIM = 64
SSD_WIDTH = SSD_HEADS * SSD_HEAD_DIM
SSD_GROUPS = 2
SSD_STATE = 128
SSD_CONV_DIM = SSD_WIDTH + 2 * SSD_GROUPS * SSD_STATE
SSD_CHUNK = 128
MLA_HEADS = 8
MLA_NOPE = 64
MLA_ROPE = 32
MLA_V = 64
MLA_WIDTH = MLA_HEADS * MLA_V
MLA_Q_RANK = 384
MLA_KV_RANK = 256
MLA_SCALE = (MLA_NOPE + MLA_ROPE) ** -0.5
ROPE_THETA = 10000.0
GDN_HEADS = 4
GDN_HEAD_DIM = 128
GDN_WIDTH = GDN_HEADS * GDN_HEAD_DIM
GDN_CONV_DIM = 3 * GDN_WIDTH
GDN_CHUNK = 64
MIX_WIDTH = SSD_WIDTH + MLA_WIDTH + GDN_WIDTH
IN_SIZES = (SSD_WIDTH, SSD_CONV_DIM, SSD_HEADS, MLA_Q_RANK, MLA_KV_RANK, MLA_ROPE, MLA_WIDTH,
            GDN_CONV_DIM, GDN_WIDTH, GDN_HEADS, GDN_HEADS)
IN_OFFSETS = tuple(int(o) for o in np.cumsum(IN_SIZES)[:-1])
DEPTH = 2
DEEPNORM_ALPHA = (2 * DEPTH) ** 0.25
LN_EPS = 1e-5
RMS_EPS = 1e-6
L2_EPS = 1e-6
PAGE_SIZE = 128

LANES = 128
SUBLANES = 8
BF16_ROWS = 16
ROPE_HALF = MLA_ROPE // 2
P_SSD = SSD_WIDTH + SSD_CONV_DIM + LANES
P_MLA = MLA_Q_RANK + MLA_KV_RANK + 2 * LANES + MLA_WIDTH
P_GDN = GDN_CONV_DIM + GDN_WIDTH + LANES
QK_WIDTH = MLA_KV_RANK + 2 * LANES
Q_SCALE = MLA_SCALE * math.log2(math.e)
VMEM_LIMIT = 56 * 1024 * 1024
PAGES_PER_GROUP = 32
GDN_BATCH_PER_STEP = 4

def _cparams(sem):
    return pltpu.CompilerParams(dimension_semantics=sem, vmem_limit_bytes=VMEM_LIMIT)


def _bdot(a, b):
    return jnp.dot(a.astype(BF16), b.astype(BF16), preferred_element_type=F32)


def _bdot_nt(a, b):
    return lax.dot_general(a.astype(BF16), b.astype(BF16), (((1,), (1,)), ((), ())),
                           preferred_element_type=F32)


def _fdot(a, b):
    return jnp.dot(a, b, precision=lax.Precision.HIGHEST, preferred_element_type=F32)


def _silu(x):
    return x * (1.0 / (1.0 + jnp.exp(-x)))


def _softplus(x):
    return jnp.maximum(x, 0.0) + jnp.log1p(jnp.exp(-jnp.abs(x)))


def _const_spec(shape):
    nd = len(shape)
    return pl.BlockSpec(shape, lambda *_: (0,) * nd)


def _ln_kernel(x_ref, g_ref, b_ref, o_ref):
    x = x_ref[...]
    mu = jnp.mean(x, -1, keepdims=True)
    xc = x - mu
    var = jnp.mean(xc * xc, -1, keepdims=True)
    o_ref[...] = xc * lax.rsqrt(var + LN_EPS) * g_ref[...] + b_ref[...]


def _layer_norm(x, g, b, tm):
    m, d = x.shape
    return pl.pallas_call(
        _ln_kernel, grid=(m // tm,),
        in_specs=[pl.BlockSpec((tm, d), lambda i: (i, 0)), _const_spec((1, d)), _const_spec((1, d))],
        out_specs=pl.BlockSpec((tm, d), lambda i: (i, 0)),
        out_shape=jax.ShapeDtypeStruct((m, d), F32),
        compiler_params=_cparams(("parallel",)), name="emb_ln")(x, g.reshape(1, d), b.reshape(1, d))


def _proj_kernel(x_ref, w_ref, o_ref):
    o_ref[...] = _bdot(x_ref[...], w_ref[...])


def _project(x, w, tm, name):
    m, k = x.shape
    n = w.shape[1]
    return pl.pallas_call(
        _proj_kernel, grid=(m // tm,),
        in_specs=[pl.BlockSpec((tm, k), lambda i: (i, 0)), _const_spec((k, n))],
        out_specs=pl.BlockSpec((tm, n), lambda i: (i, 0)),
        out_shape=jax.ShapeDtypeStruct((m, n), F32),
        compiler_params=_cparams(("parallel",)), name=name)(x, w)


def _conv_chunk(xbuf, x_new, cw_ref, L):
    xbuf[SUBLANES:SUBLANES + L, :] = x_new
    y = cw_ref[CONV_WIDTH - 1:CONV_WIDTH, :] * x_new
    for k in range(CONV_WIDTH - 1):
        off = SUBLANES - (CONV_WIDTH - 1) + k
        y = y + cw_ref[k:k + 1, :] * xbuf[off:off + L, :]
    return y


def _ssd_kernel(p_ref, cprev_ref, h0_ref, cw_ref, cb_ref, dtb_ref, alog_ref, dexp_ref, nw_ref, e_ref,
                y_ref, cnew_ref, hout_ref, xbuf, ht, *, L, tv):
    c = pl.program_id(1)
    nc = pl.num_programs(1)

    @pl.when(c == 0)
    def _():
        xbuf[0:SUBLANES, :] = cprev_ref[0]
        ht[...] = h0_ref[0]

    z = p_ref[0, :, 0:SSD_WIDTH]
    xbc_raw = p_ref[0, :, SSD_WIDTH:SSD_WIDTH + SSD_CONV_DIM]
    dt_raw = p_ref[0, :, SSD_WIDTH + SSD_CONV_DIM:P_SSD]

    xbc = _silu(_conv_chunk(xbuf, xbc_raw, cw_ref, L) + cb_ref[...])
    cnew_ref[0] = xbuf[tv:tv + SUBLANES, :]
    xbuf[0:SUBLANES, :] = xbuf[L:L + SUBLANES, :]

    xs = xbc[:, 0:SSD_WIDTH]
    gs = SSD_GROUPS * SSD_STATE
    bm = xbc[:, SSD_WIDTH:SSD_WIDTH + gs]
    cm = xbc[:, SSD_WIDTH + gs:SSD_WIDTH + 2 * gs]

    dt = _softplus(dt_raw + dtb_ref[...])
    if tv < L:
        rows = lax.broadcasted_iota(jnp.int32, (L, LANES), 0)
        dt = jnp.where(rows < tv, dt, 0.0)
    a = -jnp.exp(alog_ref[...])
    da = dt * a
    r_i = lax.broadcasted_iota(jnp.int32, (L, L), 0)
    c_i = lax.broadcasted_iota(jnp.int32, (L, L), 1)
    causal = c_i <= r_i
    tri = jnp.where(causal, 1.0, 0.0).astype(F32)
    acum = _fdot(tri, da)
    last = acum[L - 1:L, :]
    ea = jnp.exp(acum)
    wdec = jnp.exp(last - acum)

    def pad_rows(v):
        if L == LANES:
            return v
        return jnp.concatenate([v, jnp.zeros((LANES - L, v.shape[1]), v.dtype)], axis=0)

    acum_t = jnp.transpose(pad_rows(acum))

    def hilo(v):
        hi = v.astype(BF16)
        lo = (v - hi.astype(F32)).astype(BF16)
        return hi, lo

    parts = []
    for v in (dt, ea, wdec):
        parts.extend(hilo(v))
    stacked = jnp.concatenate(parts, axis=0)
    expanded = jnp.dot(stacked, e_ref[...], preferred_element_type=F32)
    dt_e = expanded[0:L] + expanded[L:2 * L]
    ea_e = expanded[2 * L:3 * L] + expanded[3 * L:4 * L]
    wd_e = expanded[4 * L:5 * L] + expanded[5 * L:6 * L]

    xdt = xs * dt_e
    xdt_b = xdt.astype(BF16)
    lane = lax.broadcasted_iota(jnp.int32, (L, LANES), 1)
    hpg = SSD_HEADS // SSD_GROUPS

    y_parts = []
    for g in range(SSD_GROUPS):
        bg = bm[:, g * SSD_STATE:(g + 1) * SSD_STATE]
        cg = cm[:, g * SSD_STATE:(g + 1) * SSD_STATE]
        cb = _bdot_nt(cg, bg)
        for j in range(hpg // 2):
            pair = g * (hpg // 2) + j
            xp = xdt_b[:, pair * LANES:(pair + 1) * LANES]
            ys = []
            for h in (2 * pair, 2 * pair + 1):
                seg = acum[:, h:h + 1] - acum_t[h:h + 1, 0:L]
                dec = jnp.exp(jnp.where(causal, seg, -jnp.inf))
                ys.append(_bdot(cb * dec, xp))
            y_parts.append(jnp.where(lane < SSD_HEAD_DIM, ys[0], ys[1]))
    y_in = jnp.concatenate(y_parts, axis=1)

    gw = hpg * SSD_HEAD_DIM
    y_st_parts = []
    for g in range(SSD_GROUPS):
        cg = cm[:, g * SSD_STATE:(g + 1) * SSD_STATE]
        y_st_parts.append(_bdot(cg, ht[g]))
    y_st = jnp.concatenate(y_st_parts, axis=1) * ea_e

    xw = (xdt * wd_e)
    for g in range(SSD_GROUPS):
        bg_t = jnp.transpose(pad_rows(bm[:, g * SSD_STATE:(g + 1) * SSD_STATE]))
        xw_g = pad_rows(xw[:, g * gw:(g + 1) * gw])
        ht[g] = ht[g] * ea_e[L - 1:L, g * gw:(g + 1) * gw] + _bdot(bg_t, xw_g)

    y = (y_in + y_st + dexp_ref[...] * xs) * _silu(z)
    outs = []
    for g in range(SSD_GROUPS):
        yg = y[:, g * gw:(g + 1) * gw]
        ms = jnp.mean(yg * yg, -1, keepdims=True)
        outs.append(yg * lax.rsqrt(ms + RMS_EPS) * nw_ref[:, g * gw:(g + 1) * gw])
    y_ref[0] = jnp.concatenate(outs, axis=1).astype(y_ref.dtype)

    @pl.when(c == nc - 1)
    def _():
        hout_ref[0] = ht[...]


def _ssd_scan(p_ssd, conv_prev8, h0t, cw, cb, dtb, alog, dexp, nw, emat, *, L, tv):
    b, t, _ = p_ssd.shape
    nc = t // L
    gw = SSD_WIDTH // SSD_GROUPS
    kern = functools.partial(_ssd_kernel, L=L, tv=tv)
    return pl.pallas_call(
        kern, grid=(b, nc),
        in_specs=[pl.BlockSpec((1, L, P_SSD), lambda i, c: (i, c, 0)),
                  pl.BlockSpec((1, SUBLANES, SSD_CONV_DIM), lambda i, c: (i, 0, 0)),
                  pl.BlockSpec((1, SSD_GROUPS, SSD_STATE, gw), lambda i, c: (i, 0, 0, 0)),
                  _const_spec((CONV_WIDTH, SSD_CONV_DIM)), _const_spec((1, SSD_CONV_DIM)),
                  _const_spec((1, LANES)), _const_spec((1, LANES)),
                  _const_spec((1, SSD_WIDTH)), _const_spec((1, SSD_WIDTH)),
                  _const_spec((LANES, SSD_WIDTH))],
        out_specs=[pl.BlockSpec((1, L, SSD_WIDTH), lambda i, c: (i, c, 0)),
                   pl.BlockSpec((1, SUBLANES, SSD_CONV_DIM), lambda i, c: (i, 0, 0)),
                   pl.BlockSpec((1, SSD_GROUPS, SSD_STATE, gw), lambda i, c: (i, 0, 0, 0))],
        out_shape=[jax.ShapeDtypeStruct((b, t, SSD_WIDTH), BF16),
                   jax.ShapeDtypeStruct((b, SUBLANES, SSD_CONV_DIM), F32),
                   jax.ShapeDtypeStruct((b, SSD_GROUPS, SSD_STATE, gw), F32)],
        scratch_shapes=[pltpu.VMEM((L + SUBLANES, SSD_CONV_DIM), F32),
                        pltpu.VMEM((SSD_GROUPS, SSD_STATE, gw), F32)],
        compiler_params=_cparams(("parallel", "arbitrary")), name="ssd_scan",
    )(p_ssd, conv_prev8, h0t, cw, cb, dtb, alog, dexp, nw, emat)


def _gdn_chunk(p_ref, cw_ref, bias_ref, alog_ref, nw_ref, y_ref, cnew_ref, xbuf, st, *, L, tv):
    H = GDN_HEADS
    D = GDN_HEAD_DIM
    R = H * L
    S = max(R, LANES)

    qkv_raw = p_ref[:, 0:GDN_CONV_DIM]
    z = p_ref[:, GDN_CONV_DIM:GDN_CONV_DIM + GDN_WIDTH]
    ba = p_ref[:, GDN_CONV_DIM + GDN_WIDTH:P_GDN]

    qkv = _silu(_conv_chunk(xbuf, qkv_raw, cw_ref, L))
    cnew_ref[...] = xbuf[tv:tv + SUBLANES, :]
    xbuf[0:SUBLANES, :] = xbuf[L:L + SUBLANES, :]
    yield

    beta_f = 1.0 / (1.0 + jnp.exp(-ba))
    g_f = -jnp.exp(alog_ref[...]) * _softplus(ba + bias_ref[...])
    if tv < L:
        rows = lax.broadcasted_iota(jnp.int32, (L, LANES), 0)
        beta_f = jnp.where(rows < tv, beta_f, 0.0)
        g_f = jnp.where(rows < tv, g_f, 0.0)
    r_i = lax.broadcasted_iota(jnp.int32, (L, L), 0)
    c_i = lax.broadcasted_iota(jnp.int32, (L, L), 1)
    tri = jnp.where(c_i <= r_i, 1.0, 0.0).astype(F32)
    gcum_f = _fdot(tri, g_f)
    glast_f = jnp.broadcast_to(gcum_f[L - 1:L, :], (L, LANES))

    def pad_s(v):
        if R == S:
            return v
        return jnp.concatenate([v, jnp.zeros((S - R, v.shape[1]), v.dtype)], axis=0)

    def stack(v):
        return pad_s(jnp.concatenate([v[:, h * D:(h + 1) * D] for h in range(H)], axis=0))

    def col(v, off):
        return pad_s(jnp.concatenate([v[:, off + h:off + h + 1] for h in range(H)], axis=0))

    q_s = stack(qkv[:, 0:GDN_WIDTH])
    k_s = stack(qkv[:, GDN_WIDTH:2 * GDN_WIDTH])
    v_s = stack(qkv[:, 2 * GDN_WIDTH:3 * GDN_WIDTH])
    z_s = stack(z)
    q_s = q_s * lax.rsqrt(jnp.sum(q_s * q_s, -1, keepdims=True) + L2_EPS) * (D ** -0.5)
    k_s = k_s * lax.rsqrt(jnp.sum(k_s * k_s, -1, keepdims=True) + L2_EPS)
    beta = col(beta_f, 0)
    gcum = col(gcum_f, H)
    glast = col(glast_f, H)
    yield

    cmat = jnp.broadcast_to(gcum, (S, S))
    diff = cmat - jnp.transpose(cmat)
    rs = lax.broadcasted_iota(jnp.int32, (S, S), 0)
    cs = lax.broadcasted_iota(jnp.int32, (S, S), 1)
    if L & (L - 1) == 0:
        sh = L.bit_length() - 1
        same = (rs >> sh) == (cs >> sh)
    else:
        same = (rs // L) == (cs // L)
    incl = same & (cs <= rs)
    strict = same & (cs < rs)
    dec = jnp.exp(jnp.where(incl, diff, -jnp.inf))

    kb = k_s * beta
    n_mat = -jnp.where(strict, _bdot_nt(kb, k_s) * dec, 0.0)
    eye = jnp.where(rs == cs, 1.0, 0.0).astype(F32)
    t_mat = eye + n_mat
    npow = n_mat
    yield
    span = 2
    while span < L:
        npow = _bdot(npow, npow)
        yield
        t_mat = t_mat + _bdot(t_mat, npow)
        yield
        span *= 2

    eg = jnp.exp(gcum)
    t_b = t_mat.astype(BF16)
    u = _bdot(t_b, v_s * beta)
    w = _bdot(t_b, kb * eg)
    qg = q_s * eg
    attn = jnp.where(incl, _bdot_nt(q_s, k_s) * dec, 0.0)
    yield
    ws = []
    qs_ = []
    for h in range(H):
        sh_b = st[h].astype(BF16)
        ws.append(_bdot(w[h * L:(h + 1) * L], sh_b))
        qs_.append(_bdot(qg[h * L:(h + 1) * L], sh_b))
    v_new = u - pad_s(jnp.concatenate(ws, axis=0))
    yield
    o = pad_s(jnp.concatenate(qs_, axis=0)) + _bdot(attn, v_new)

    kd = k_s * jnp.exp(glast - gcum)
    kd_t = jnp.transpose(kd).astype(BF16)
    row_head = lax.broadcasted_iota(jnp.int32, (S, D), 0)
    eg_last = jnp.exp(glast)
    for h in range(H):
        vm = jnp.where((row_head >= h * L) & (row_head < (h + 1) * L), v_new, 0.0)
        st[h] = st[h] * eg_last[h * L:h * L + 1, :] + _bdot(kd_t, vm)
    yield

    ms = jnp.mean(o * o, -1, keepdims=True)
    o = o * lax.rsqrt(ms + RMS_EPS) * nw_ref[...] * _silu(z_s)
    y_ref[...] = jnp.concatenate([o[h * L:(h + 1) * L] for h in range(H)], axis=1).astype(y_ref.dtype)


def _gdn_kernel(p_ref, cprev_ref, s0_ref, cw_ref, bias_ref, alog_ref, nw_ref,
                y_ref, cnew_ref, sout_ref, xbuf, st, *, L, tv, nbat):
    c = pl.program_id(1)
    nc = pl.num_programs(1)

    @pl.when(c == 0)
    def _():
        xbuf[:, 0:SUBLANES, :] = cprev_ref[...]
        st[...] = s0_ref[...]

    live = [_gdn_chunk(p_ref.at[i], cw_ref, bias_ref, alog_ref, nw_ref, y_ref.at[i], cnew_ref.at[i],
                       xbuf.at[i], st.at[i], L=L, tv=tv) for i in range(nbat)]
    while live:
        live = [g for g in live if next(g, StopIteration) is not StopIteration]

    @pl.when(c == nc - 1)
    def _():
        sout_ref[...] = st[...]


def _gdn_scan(p_gdn, conv_prev8, s0, cw, bias, alog, nw, *, L, tv):
    b, t, _ = p_gdn.shape
    nc = t // L
    nbat = math.gcd(b, GDN_BATCH_PER_STEP)
    kern = functools.partial(_gdn_kernel, L=L, tv=tv, nbat=nbat)
    sshape = (nbat, GDN_HEADS, GDN_HEAD_DIM, GDN_HEAD_DIM)
    return pl.pallas_call(
        kern, grid=(b // nbat, nc),
        in_specs=[pl.BlockSpec((nbat, L, P_GDN), lambda i, c: (i, c, 0)),
                  pl.BlockSpec((nbat, SUBLANES, GDN_CONV_DIM), lambda i, c: (i, 0, 0)),
                  pl.BlockSpec(sshape, lambda i, c: (i, 0, 0, 0)),
                  _const_spec((CONV_WIDTH, GDN_CONV_DIM)),
                  _const_spec((1, LANES)), _const_spec((1, LANES)), _const_spec((1, GDN_HEAD_DIM))],
        out_specs=[pl.BlockSpec((nbat, L, GDN_WIDTH), lambda i, c: (i, c, 0)),
                   pl.BlockSpec((nbat, SUBLANES, GDN_CONV_DIM), lambda i, c: (i, 0, 0)),
                   pl.BlockSpec(sshape, lambda i, c: (i, 0, 0, 0))],
        out_shape=[jax.ShapeDtypeStruct((b, t, GDN_WIDTH), BF16),
                   jax.ShapeDtypeStruct((b, SUBLANES, GDN_CONV_DIM), F32),
                   jax.ShapeDtypeStruct((b,) + sshape[1:], F32)],
        scratch_shapes=[pltpu.VMEM((nbat, L + SUBLANES, GDN_CONV_DIM), F32),
                        pltpu.VMEM(sshape, F32)],
        compiler_params=_cparams(("parallel", "arbitrary")), name="gdn_scan",
    )(p_gdn, conv_prev8, s0, cw, bias, alog, nw)


def _mla_prep_kernel(p_ref, cos_ref, sin_ref, qnw_ref, kvnw_ref, wuq_ref, wuk_ref,
                     q_ref, kcat_ref, kt_ref, ckv_ref, kr_ref, gate_ref, *, nb, tt, tq_t):
    cq = p_ref[:, 0:MLA_Q_RANK]
    ckv = p_ref[:, MLA_Q_RANK:MLA_Q_RANK + MLA_KV_RANK]
    kr = p_ref[:, MLA_Q_RANK + MLA_KV_RANK:MLA_Q_RANK + MLA_KV_RANK + 2 * LANES]
    gate = p_ref[:, MLA_Q_RANK + MLA_KV_RANK + 2 * LANES:P_MLA]
    cos = cos_ref[...]
    sin = sin_ref[...]

    cqn = cq * lax.rsqrt(jnp.mean(cq * cq, -1, keepdims=True) + RMS_EPS) * qnw_ref[...]
    q = _bdot(cqn, wuq_ref[...])
    nope_w = MLA_HEADS * MLA_NOPE
    x1 = q[:, nope_w:nope_w + LANES]
    x2 = q[:, nope_w + LANES:nope_w + 2 * LANES]
    r_all = jnp.concatenate([x1 * cos - x2 * sin, x2 * cos + x1 * sin], axis=1) * Q_SCALE

    ckvn = ckv * lax.rsqrt(jnp.mean(ckv * ckv, -1, keepdims=True) + RMS_EPS) * kvnw_ref[...]
    k1 = kr[:, 0:LANES]
    k2 = kr[:, LANES:2 * LANES]
    kr_rot = jnp.concatenate([k1 * cos - k2 * sin, k2 * cos + k1 * sin], axis=1)
    ckv_ref[...] = ckvn
    kr_ref[...] = kr_rot
    kcat_ref[...] = jnp.concatenate([ckvn, kr_rot], axis=1).astype(BF16)
    kt_ref[...] = jnp.transpose(ckvn).astype(BF16)
    gate_ref[...] = _silu(gate)

    tm = q.shape[0]
    if tq_t is None:
        lane = lax.broadcasted_iota(jnp.int32, (1, 2 * LANES), 1)
        head_of_lane = (lane & (LANES - 1)) >> 4
    else:
        r_all_t = jnp.transpose(r_all)
        row_i = lax.broadcasted_iota(jnp.int32, (2 * LANES, 1), 0)
        head_of_row = (row_i & (LANES - 1)) >> 4
    for pair in range(MLA_HEADS // 2):
        qn = q[:, pair * LANES:(pair + 1) * LANES] * Q_SCALE
        qlat = _bdot(qn, wuk_ref[pair])
        for i in range(2):
            h = 2 * pair + i
            ql = qlat[:, i * MLA_KV_RANK:(i + 1) * MLA_KV_RANK]
            if tq_t is None:
                qr = jnp.where(head_of_lane == h, r_all, 0.0).astype(BF16)
                q_ref[:, h, :, 0:MLA_KV_RANK] = ql.astype(BF16).reshape(nb, tt, MLA_KV_RANK)
                q_ref[:, h, :, MLA_KV_RANK:QK_WIDTH] = qr.reshape(nb, tt, 2 * LANES)
            else:
                ql_t = jnp.transpose(ql).astype(BF16)
                qr_t = jnp.where(head_of_row == h, r_all_t, 0.0).astype(BF16)
                for qb in range(tm // tq_t):
                    cols = slice(h * tq_t, (h + 1) * tq_t)
                    toks = slice(qb * tq_t, (qb + 1) * tq_t)
                    q_ref[0, qb, 0:MLA_KV_RANK, cols] = ql_t[:, toks]
                    q_ref[0, qb, MLA_KV_RANK:QK_WIDTH, cols] = qr_t[:, toks]


def _head_major_block(t, tm):
    if tm <= t:
        npb = t // tm
        return 1, tm, (lambda i: (i // npb, 0, i % npb, 0))
    return tm // t, t, (lambda i: (i, 0, 0, 0))


def _mla_prep(p_mla, cos_t, sin_t, qnw, kvnw, wuq, wuk_pairs, b, t, tm, tq_t=None):
    m = p_mla.shape[0]
    nt = cos_t.shape[0] // tm
    row = lambda i: (i, 0)
    tab = lambda i: (i % nt, 0)
    nb, tt, hm_map = _head_major_block(t, tm)
    kern = functools.partial(_mla_prep_kernel, nb=nb, tt=tt, tq_t=tq_t)
    if tq_t is None:
        q_spec = pl.BlockSpec((nb, MLA_HEADS, tt, QK_WIDTH), hm_map)
        q_shape = jax.ShapeDtypeStruct((b, MLA_HEADS, t, QK_WIDTH), BF16)
    else:
        npb = t // tm
        q_spec = pl.BlockSpec((1, tm // tq_t, QK_WIDTH, MLA_HEADS * tq_t), lambda i: (i // npb, i % npb, 0, 0))
        q_shape = jax.ShapeDtypeStruct((b, t // tq_t, QK_WIDTH, MLA_HEADS * tq_t), BF16)
    return pl.pallas_call(
        kern, grid=(m // tm,),
        in_specs=[pl.BlockSpec((tm, P_MLA), row), pl.BlockSpec((tm, LANES), tab), pl.BlockSpec((tm, LANES), tab),
                  _const_spec((1, MLA_Q_RANK)), _const_spec((1, MLA_KV_RANK)),
                  _const_spec(wuq.shape), _const_spec(wuk_pairs.shape)],
        out_specs=[q_spec, pl.BlockSpec((tm, QK_WIDTH), row),
                   pl.BlockSpec((MLA_KV_RANK, tm), lambda i: (0, i)),
                   pl.BlockSpec((tm, MLA_KV_RANK), row), pl.BlockSpec((tm, 2 * LANES), row),
                   pl.BlockSpec((tm, MLA_WIDTH), row)],
        out_shape=[q_shape,
                   jax.ShapeDtypeStruct((m, QK_WIDTH), BF16), jax.ShapeDtypeStruct((MLA_KV_RANK, m), BF16),
                   jax.ShapeDtypeStruct((m, MLA_KV_RANK), F32), jax.ShapeDtypeStruct((m, 2 * LANES), F32),
                   jax.ShapeDtypeStruct((m, MLA_WIDTH), F32)],
        compiler_params=_cparams(("parallel",)), name="mla_prep",
    )(p_mla, cos_t, sin_t, qnw, kvnw, wuq, wuk_pairs)


def _softmax_rows(s, m_scr, l_scr):
    n = s.shape[1]
    m_prev = m_scr[...]
    m_new = jnp.maximum(m_prev, jnp.max(s, -1, keepdims=True))
    alpha = jnp.exp2(m_prev - m_new)
    m_wide = jnp.concatenate([m_new] * (n // LANES), axis=1) if n >= LANES else m_new[:, 0:n]
    p = jnp.exp2(s - m_wide)
    l_scr[...] = alpha * l_scr[...] + jnp.sum(p, -1, keepdims=True)
    m_scr[...] = m_new
    return p, alpha


def _softmax_init(m_scr, l_scr, acc_scr):
    m_scr[...] = jnp.full(m_scr.shape, -jnp.inf, F32)
    l_scr[...] = jnp.zeros(l_scr.shape, F32)
    acc_scr[...] = jnp.zeros(acc_scr.shape, F32)


def _flash_kernel(qi_ref, ki_ref, q_ref, k_ref, kt_ref, o_ref, m_scr, l_scr, acc_scr, *, tq, tk):
    step_i = pl.program_id(1)
    qi = qi_ref[step_i]
    ki = ki_ref[step_i]
    rows = tq * MLA_HEADS

    @pl.when(ki == 0)
    def _():
        _softmax_init(m_scr, l_scr, acc_scr)

    def step(masked):
        s_t = jnp.dot(k_ref[0], q_ref[0, 0], preferred_element_type=F32)
        if masked:
            k_pos = ki * tk + lax.broadcasted_iota(jnp.int32, (tk, rows), 0)
            q_pos = qi * tq + (lax.broadcasted_iota(jnp.int32, (tk, rows), 1) & (tq - 1))
            s_t = jnp.where(k_pos <= q_pos, s_t, -jnp.inf)
        m_prev = m_scr[...]
        m_new = jnp.maximum(m_prev, jnp.max(s_t, axis=0, keepdims=True))
        alpha = jnp.exp2(m_prev - m_new)
        p_t = jnp.exp2(s_t - m_new)
        l_scr[...] = alpha * l_scr[...] + jnp.sum(p_t, axis=0, keepdims=True)
        m_scr[...] = m_new
        acc_scr[...] = acc_scr[...] * alpha + jnp.dot(kt_ref[...], p_t.astype(BF16), preferred_element_type=F32)

    crosses = (ki + 1) * tk > qi * tq + 1

    @pl.when(crosses)
    def _():
        step(True)

    @pl.when(jnp.logical_not(crosses))
    def _():
        step(False)

    @pl.when(ki == ((qi + 1) * tq - 1) // tk)
    def _():
        o_t = acc_scr[...] * (1.0 / l_scr[...])
        o_ref[0] = jnp.transpose(o_t).astype(o_ref.dtype).reshape(MLA_HEADS, tq, MLA_KV_RANK)


def _flash_attention(q_t, kcat, kt, *, tq, tk):
    b, nq, _, rows = q_t.shape
    t = nq * tq
    nkb = t // tk
    kern = functools.partial(_flash_kernel, tq=tq, tk=tk)
    pairs = [(qi, ki) for qi in range(nq) for ki in range(((qi + 1) * tq - 1) // tk + 1)]
    qi_arr = jnp.asarray([p[0] for p in pairs], jnp.int32)
    ki_arr = jnp.asarray([p[1] for p in pairs], jnp.int32)
    grid_spec = pltpu.PrefetchScalarGridSpec(
        num_scalar_prefetch=2, grid=(b, len(pairs)),
        in_specs=[pl.BlockSpec((1, 1, QK_WIDTH, rows), lambda i, s, qa, ka: (i, qa[s], 0, 0)),
                  pl.BlockSpec((1, tk, QK_WIDTH), lambda i, s, qa, ka: (i, ka[s], 0)),
                  pl.BlockSpec((MLA_KV_RANK, tk), lambda i, s, qa, ka: (0, i * nkb + ka[s]))],
        out_specs=pl.BlockSpec((1, MLA_HEADS, tq, MLA_KV_RANK), lambda i, s, qa, ka: (i, 0, qa[s], 0)),
        scratch_shapes=[pltpu.VMEM((1, rows), F32), pltpu.VMEM((1, rows), F32),
                        pltpu.VMEM((MLA_KV_RANK, rows), F32)])
    return pl.pallas_call(
        kern, grid_spec=grid_spec,
        out_shape=jax.ShapeDtypeStruct((b, MLA_HEADS, t, MLA_KV_RANK), BF16),
        compiler_params=_cparams(("parallel", "arbitrary")), name="mla_flash",
    )(qi_arr, ki_arr, q_t, kcat, kt)


def _paged_kernel(pt_ref, q_ref, knew_ref, sel_ref, rowsel_ref, rowselt_ref, lat_hbm, rope_hbm, o_ref,
                  latbuf, ropebuf, sems, m_scr, l_scr, acc_scr, *, tpad, tv, layer, ngroups):
    bi = pl.program_id(0)
    nb = pl.num_programs(0)
    npg = PAGES_PER_GROUP
    rows = tpad * MLA_HEADS
    rows_v = tv * MLA_HEADS
    reps = MLA_KV_RANK // LANES

    def copies(seq, grp, slot):
        out = []
        for i in range(npg):
            page = pt_ref[seq, grp * npg + i]
            dst = pl.ds(i * PAGE_SIZE, PAGE_SIZE)
            out.append(pltpu.make_async_copy(lat_hbm.at[layer, page], latbuf.at[slot, dst], sems.at[0, slot, i]))
            out.append(pltpu.make_async_copy(rope_hbm.at[layer, page], ropebuf.at[slot, :, dst], sems.at[1, slot, i]))
        return out

    def start(seq, grp, slot):
        for cp in copies(seq, grp, slot):
            cp.start()

    def wait(seq, grp, slot):
        for cp in copies(seq, grp, slot):
            cp.wait()

    @pl.when(bi == 0)
    def _():
        for g in range(ngroups):
            start(0, g, g)

    for a in range(2):
        _softmax_init(m_scr.at[a], l_scr.at[a], acc_scr.at[a])
    q_v = jnp.dot(rowsel_ref[...], q_ref[0].reshape(rows, QK_WIDTH), preferred_element_type=F32).astype(BF16)
    q_lat = q_v[:, 0:MLA_KV_RANK]
    q_rd = jnp.dot(q_v[:, MLA_KV_RANK:QK_WIDTH], sel_ref[...], preferred_element_type=F32).astype(BF16)

    nt = (((1,), (1,)), ((), ()))

    def accumulate(a, s, v_b):
        p, alpha = _softmax_rows(s, m_scr.at[a], l_scr.at[a])
        yield
        acc_scr[a] = (acc_scr[a] * jnp.concatenate([alpha] * reps, axis=1)
                      + jnp.dot(p.astype(BF16), v_b, preferred_element_type=F32))

    def consume(a, slot):
        lat_b = latbuf[slot].astype(BF16)
        rope_b = ropebuf[slot].astype(BF16)
        yield
        s = (lax.dot_general(q_lat, lat_b, nt, preferred_element_type=F32)
             + jnp.dot(q_rd, rope_b, preferred_element_type=F32))
        yield
        yield from accumulate(a, s, lat_b)

    def lockstep(gens):
        while gens:
            gens = [g for g in gens if next(g, StopIteration) is not StopIteration]

    k_b = knew_ref[0]
    s2 = lax.dot_general(q_v, k_b, nt, preferred_element_type=F32)
    q_pos = lax.rem(lax.broadcasted_iota(jnp.int32, (rows_v, tpad), 0), tv)
    k_pos = lax.broadcasted_iota(jnp.int32, (rows_v, tpad), 1)
    lockstep([accumulate(0, jnp.where(k_pos <= q_pos, s2, -jnp.inf), k_b[:, 0:MLA_KV_RANK])])

    for g0 in range(0, ngroups, 2):
        wait(bi, g0, g0)
        wait(bi, g0 + 1, g0 + 1)
        lockstep([consume(0, g0), consume(1, g0 + 1)])

        @pl.when(bi + 1 < nb)
        def _():
            start(bi + 1, g0, g0)
            start(bi + 1, g0 + 1, g0 + 1)

    m0, m1 = m_scr[0], m_scr[1]
    m = jnp.maximum(m0, m1)
    a0, a1 = jnp.exp2(m0 - m), jnp.exp2(m1 - m)
    inv_l = 1.0 / (a0 * l_scr[0] + a1 * l_scr[1])
    wide = lambda v: jnp.concatenate([v] * reps, axis=1)
    o_v = ((acc_scr[0] * wide(a0) + acc_scr[1] * wide(a1)) * wide(inv_l)).astype(BF16)
    o = jnp.dot(rowselt_ref[...], o_v, preferred_element_type=F32)
    o_ref[0] = o.astype(o_ref.dtype).reshape(MLA_HEADS, tpad, MLA_KV_RANK)


def _row_select_matrix(tpad, tv):
    s = np.zeros((MLA_HEADS * tv, MLA_HEADS * tpad), np.float32)
    for h in range(MLA_HEADS):
        for t in range(tv):
            s[h * tv + t, h * tpad + t] = 1.0
    return s


def _paged_attention(q, knew, cache_lat, cache_rope_t, page_table, sel, *, tpad, tv, layer):
    b = q.shape[0]
    n_pages = page_table.shape[1]
    npg = PAGES_PER_GROUP
    ngroups = n_pages // npg
    assert n_pages % (2 * npg) == 0, "page groups are consumed in slot pairs"
    rows = tpad * MLA_HEADS
    rows_v = tv * MLA_HEADS
    gk = npg * PAGE_SIZE
    rowsel = _row_select_matrix(tpad, tv)
    kern = functools.partial(_paged_kernel, tpad=tpad, tv=tv, layer=layer, ngroups=ngroups)
    stat = pltpu.VMEM((2, rows_v, LANES), F32)
    grid_spec = pltpu.PrefetchScalarGridSpec(
        num_scalar_prefetch=1, grid=(b,),
        in_specs=[pl.BlockSpec((1, MLA_HEADS, tpad, QK_WIDTH), lambda bi, pt: (bi, 0, 0, 0)),
                  pl.BlockSpec((1, tpad, QK_WIDTH), lambda bi, pt: (bi, 0, 0)),
                  pl.BlockSpec(sel.shape, lambda bi, pt: (0, 0)),
                  pl.BlockSpec((rows_v, rows), lambda bi, pt: (0, 0)),
                  pl.BlockSpec((rows, rows_v), lambda bi, pt: (0, 0)),
                  pl.BlockSpec(memory_space=pl.ANY), pl.BlockSpec(memory_space=pl.ANY)],
        out_specs=pl.BlockSpec((1, MLA_HEADS, tpad, MLA_KV_RANK), lambda bi, pt: (bi, 0, 0, 0)),
        scratch_shapes=[pltpu.VMEM((ngroups, gk, MLA_KV_RANK), F32), pltpu.VMEM((ngroups, MLA_ROPE, gk), F32),
                        pltpu.SemaphoreType.DMA((2, ngroups, npg)),
                        stat, stat, pltpu.VMEM((2, rows_v, MLA_KV_RANK), F32)])
    return pl.pallas_call(
        kern, grid_spec=grid_spec,
        out_shape=jax.ShapeDtypeStruct((b, MLA_HEADS, tpad, MLA_KV_RANK), BF16),
        compiler_params=_cparams(("arbitrary",)), name="mla_paged",
    )(page_table, q, knew, sel, jnp.asarray(rowsel, BF16), jnp.asarray(rowsel.T, BF16), cache_lat, cache_rope_t)


def _out_kernel(x_ref, yssd_ref, olat_ref, gate_ref, ygdn_ref, wuv_ref, wout_ref, g_ref, b_ref, o_ref, *, tm):
    y_mla = []
    for pair in range(MLA_HEADS // 2):
        o_pair = None
        for i in range(2):
            o_h = olat_ref[:, 2 * pair + i].reshape(tm, MLA_KV_RANK)
            part = jnp.dot(o_h, wuv_ref[pair, i * MLA_KV_RANK:(i + 1) * MLA_KV_RANK, :], preferred_element_type=F32)
            o_pair = part if o_pair is None else o_pair + part
        y_mla.append(o_pair * gate_ref[:, pair * LANES:(pair + 1) * LANES])
    y_mla = jnp.concatenate(y_mla, axis=1)
    mm = (_bdot(yssd_ref[...], wout_ref[0:SSD_WIDTH, :])
          + _bdot(y_mla, wout_ref[SSD_WIDTH:SSD_WIDTH + MLA_WIDTH, :])
          + _bdot(ygdn_ref[...], wout_ref[SSD_WIDTH + MLA_WIDTH:MIX_WIDTH, :]))
    v = DEEPNORM_ALPHA * x_ref[...] + mm
    mu = jnp.mean(v, -1, keepdims=True)
    vc = v - mu
    var = jnp.mean(vc * vc, -1, keepdims=True)
    o_ref[...] = vc * lax.rsqrt(var + LN_EPS) * g_ref[...] + b_ref[...]


def _out_proj(x, y_ssd, o_lat, gate, y_gdn, wuv_pairs, wout, g, b, tm):
    m = x.shape[0]
    t = o_lat.shape[2]
    row = lambda i: (i, 0)
    nb, tt, hm_map = _head_major_block(t, tm)
    return pl.pallas_call(
        functools.partial(_out_kernel, tm=tm), grid=(m // tm,),
        in_specs=[pl.BlockSpec((tm, D_MODEL), row), pl.BlockSpec((tm, SSD_WIDTH), row),
                  pl.BlockSpec((nb, MLA_HEADS, tt, MLA_KV_RANK), hm_map), pl.BlockSpec((tm, MLA_WIDTH), row),
                  pl.BlockSpec((tm, GDN_WIDTH), row),
                  _const_spec(wuv_pairs.shape), _const_spec(wout.shape),
                  _const_spec((1, D_MODEL)), _const_spec((1, D_MODEL))],
        out_specs=pl.BlockSpec((tm, D_MODEL), row),
        out_shape=jax.ShapeDtypeStruct((m, D_MODEL), F32),
        compiler_params=_cparams(("parallel",)), name="out_proj",
    )(x, y_ssd, o_lat, gate, y_gdn, wuv_pairs, wout, g, b)


def _pad_lanes(v, n=LANES):
    return jnp.pad(v, ((0, 0), (0, n - v.shape[1])))


def _prep_layer_weights(l, w_in, ssd_conv_w, ssd_conv_b, ssd_dt_bias, ssd_a_log, ssd_d, ssd_norm_w,
                        mla_q_norm_w, mla_w_uq, mla_kv_norm_w, mla_w_uk, mla_w_uv,
                        gdn_conv_w, gdn_dt_bias, gdn_a_log, gdn_norm_w, w_out, ln_g, ln_b):
    (w_z, w_xbc, w_dt, w_cq, w_ckv, w_kr, w_gate, w_qkv, w_gz, w_gb, w_ga) = jnp.split(w_in[l], IN_OFFSETS, axis=1)
    w_ssd = jnp.concatenate([w_z, w_xbc, _pad_lanes(w_dt)], axis=1).astype(BF16)
    kr_tiled = jnp.concatenate([jnp.tile(w_kr[:, :ROPE_HALF], (1, MLA_HEADS)),
                                jnp.tile(w_kr[:, ROPE_HALF:], (1, MLA_HEADS))], axis=1)
    w_mla = jnp.concatenate([w_cq, w_ckv, kr_tiled, w_gate], axis=1).astype(BF16)
    w_gdn = jnp.concatenate([w_qkv, w_gz, _pad_lanes(jnp.concatenate([w_gb, w_ga], axis=1))], axis=1).astype(BF16)

    uq = mla_w_uq[l].reshape(MLA_Q_RANK, MLA_HEADS, MLA_NOPE + MLA_ROPE)
    wuq = jnp.concatenate([uq[:, :, :MLA_NOPE].reshape(MLA_Q_RANK, -1),
                           uq[:, :, MLA_NOPE:MLA_NOPE + ROPE_HALF].reshape(MLA_Q_RANK, -1),
                           uq[:, :, MLA_NOPE + ROPE_HALF:].reshape(MLA_Q_RANK, -1)], axis=1).astype(BF16)
    uk = jnp.transpose(mla_w_uk[l], (1, 2, 0))
    uv = jnp.transpose(mla_w_uv[l], (1, 0, 2))
    zk = jnp.zeros((MLA_NOPE, MLA_KV_RANK), F32)
    zv = jnp.zeros((MLA_KV_RANK, MLA_V), F32)
    wuk_pairs = jnp.stack([jnp.block([[uk[2 * p], zk], [zk, uk[2 * p + 1]]]) for p in range(MLA_HEADS // 2)]).astype(BF16)
    wuv_pairs = jnp.stack([jnp.block([[uv[2 * p], zv], [zv, uv[2 * p + 1]]]) for p in range(MLA_HEADS // 2)]).astype(BF16)

    gdn_bias = jnp.pad(gdn_dt_bias[l][None, :], ((0, 0), (GDN_HEADS, LANES - 2 * GDN_HEADS)))
    gdn_alog = jnp.pad(gdn_a_log[l][None, :], ((0, 0), (GDN_HEADS, LANES - 2 * GDN_HEADS)))
    return dict(
        w_ssd=w_ssd, w_mla=w_mla, w_gdn=w_gdn,
        ssd_cw=ssd_conv_w[l], ssd_cb=ssd_conv_b[l][None, :],
        ssd_dtb=_pad_lanes(ssd_dt_bias[l][None, :]), ssd_alog=_pad_lanes(ssd_a_log[l][None, :]),
        ssd_dexp=jnp.repeat(ssd_d[l], SSD_HEAD_DIM)[None, :], ssd_nw=ssd_norm_w[l][None, :],
        qnw=mla_q_norm_w[l][None, :], kvnw=mla_kv_norm_w[l][None, :], wuq=wuq,
        wuk_pairs=wuk_pairs, wuv_pairs=wuv_pairs,
        gdn_cw=gdn_conv_w[l], gdn_bias=gdn_bias, gdn_alog=gdn_alog, gdn_nw=gdn_norm_w[l][None, :],
        w_out=w_out[l].astype(BF16), ln_g=ln_g[l][None, :], ln_b=ln_b[l][None, :])


def _rope_tables(pos):
    inv = ROPE_THETA ** (-jnp.arange(ROPE_HALF, dtype=F32) / ROPE_HALF)
    ang = pos.astype(F32)[:, None] * inv[None, :]
    return jnp.tile(jnp.cos(ang), (1, MLA_HEADS)), jnp.tile(jnp.sin(ang), (1, MLA_HEADS))


def _head_expand_matrix():
    e = np.zeros((LANES, SSD_WIDTH), np.float32)
    for h in range(SSD_HEADS):
        e[h, h * SSD_HEAD_DIM:(h + 1) * SSD_HEAD_DIM] = 1.0
    return jnp.asarray(e, BF16)


def _rope_select_matrix():
    s = np.zeros((2 * LANES, MLA_ROPE), np.float32)
    for j in range(2 * LANES):
        s[j, (j // LANES) * ROPE_HALF + (j % ROPE_HALF)] = 1.0
    return jnp.asarray(s, BF16)


def _tail8(conv_state):
    return jnp.pad(conv_state, ((0, 0), (SUBLANES - (CONV_WIDTH - 1), 0), (0, 0)))


def _ssd_state_to_kernel(s):
    b = s.shape[0]
    hpg = SSD_HEADS // SSD_GROUPS
    s = s.reshape(b, SSD_GROUPS, hpg, SSD_HEAD_DIM, SSD_STATE)
    return jnp.transpose(s, (0, 1, 4, 2, 3)).reshape(b, SSD_GROUPS, SSD_STATE, hpg * SSD_HEAD_DIM)


def _ssd_state_from_kernel(s):
    b = s.shape[0]
    hpg = SSD_HEADS // SSD_GROUPS
    s = s.reshape(b, SSD_GROUPS, SSD_STATE, hpg, SSD_HEAD_DIM)
    return jnp.transpose(s, (0, 1, 3, 4, 2)).reshape(b, SSD_HEADS, SSD_HEAD_DIM, SSD_STATE)


def _trunk(x, pos, tv, ssd_conv, ssd_state, gdn_conv, gdn_state, emb_g, emb_b, weights, cfg, paged=None):
    b, t, _ = x.shape
    m = b * t
    tm = cfg["tm"]
    cos_t, sin_t = _rope_tables(pos)
    if cos_t.shape[0] < tm:
        reps = tm // cos_t.shape[0]
        cos_t, sin_t = jnp.tile(cos_t, (reps, 1)), jnp.tile(sin_t, (reps, 1))
    emat = _head_expand_matrix()
    sel = _rope_select_matrix()
    h = _layer_norm(x.reshape(m, D_MODEL), emb_g, emb_b, tm)
    new_states = []
    for l in range(DEPTH):
        w = weights[l]
        tmp = cfg["tm_proj"]
        p_ssd = _project(h, w["w_ssd"], tmp, "proj_ssd").reshape(b, t, P_SSD)
        p_mla = _project(h, w["w_mla"], tmp, "proj_mla")
        p_gdn = _project(h, w["w_gdn"], tmp, "proj_gdn").reshape(b, t, P_GDN)

        y_ssd, ssd_c8, ssd_ht = _ssd_scan(
            p_ssd, _tail8(ssd_conv[l]), _ssd_state_to_kernel(ssd_state[l]),
            w["ssd_cw"], w["ssd_cb"], w["ssd_dtb"], w["ssd_alog"], w["ssd_dexp"], w["ssd_nw"], emat,
            L=cfg["ssd_chunk"], tv=min(tv, cfg["ssd_chunk"]))
        y_gdn, gdn_c8, gdn_s = _gdn_scan(
            p_gdn, _tail8(gdn_conv[l]), gdn_state[l], w["gdn_cw"], w["gdn_bias"], w["gdn_alog"], w["gdn_nw"],
            L=cfg["gdn_chunk"], tv=min(tv, cfg["gdn_chunk"]))

        q, kcat, kt, ckv, kr_t, gate = _mla_prep(p_mla, cos_t, sin_t, w["qnw"], w["kvnw"], w["wuq"], w["wuk_pairs"],
                                             b, t, tm, tq_t=cfg["tq"] if paged is None else None)
        kcat = kcat.reshape(b, t, QK_WIDTH)
        if paged is None:
            o_lat = _flash_attention(q, kcat, kt, tq=cfg["tq"], tk=cfg["tk"])
        else:
            cache_lat, cache_rope_t, page_table = paged
            o_lat = _paged_attention(q, kcat, cache_lat, cache_rope_t, page_table, sel, tpad=t, tv=tv, layer=l)

        h = _out_proj(h, y_ssd.reshape(m, SSD_WIDTH), o_lat, gate, y_gdn.reshape(m, GDN_WIDTH),
                      w["wuv_pairs"], w["w_out"], w["ln_g"], w["ln_b"], tm)

        kr = jnp.concatenate([kr_t[:, 0:ROPE_HALF], kr_t[:, LANES:LANES + ROPE_HALF]], axis=1)
        sl = slice(SUBLANES - (CONV_WIDTH - 1), SUBLANES)
        new_states.append((ckv.reshape(b, t, MLA_KV_RANK), kr.reshape(b, t, MLA_ROPE),
                           ssd_c8[:, sl], _ssd_state_from_kernel(ssd_ht), gdn_c8[:, sl], gdn_s))
    return h.reshape(b, t, D_MODEL), tuple(jnp.stack(s) for s in zip(*new_states))


def kernel(x_prompt, x_sample, cache_kv_latent, cache_k_rope, state_ssd_conv, state_ssd, state_gdn_conv, state_gdn, page_table, emb_ln_g, emb_ln_b, w_in, ssd_conv_w, ssd_conv_b, ssd_dt_bias, ssd_a_log, ssd_d, ssd_norm_w, mla_q_norm_w, mla_w_uq, mla_kv_norm_w, mla_w_uk, mla_w_uv, gdn_conv_w, gdn_dt_bias, gdn_a_log, gdn_norm_w, w_out, ln_g, ln_b):
    weights = [_prep_layer_weights(l, w_in, ssd_conv_w, ssd_conv_b, ssd_dt_bias, ssd_a_log, ssd_d, ssd_norm_w,
                                   mla_q_norm_w, mla_w_uq, mla_kv_norm_w, mla_w_uk, mla_w_uv,
                                   gdn_conv_w, gdn_dt_bias, gdn_a_log, gdn_norm_w, w_out, ln_g, ln_b)
               for l in range(DEPTH)]

    bp, tp, _ = x_prompt.shape
    zeros = lambda *s: jnp.zeros(s, F32)
    ssd_chunk = SSD_CHUNK if tp % SSD_CHUNK == 0 else tp
    gdn_chunk = GDN_CHUNK if tp % GDN_CHUNK == 0 else tp
    cfg_p = dict(tm=min(512, bp * tp), tm_proj=min(1024, bp * tp), ssd_chunk=ssd_chunk, gdn_chunk=gdn_chunk,
                 tq=min(256, tp), tk=min(512, tp))
    y_prompt, st_p = _trunk(
        x_prompt, jnp.arange(tp), tp,
        zeros(DEPTH, bp, CONV_WIDTH - 1, SSD_CONV_DIM), zeros(DEPTH, bp, SSD_HEADS, SSD_HEAD_DIM, SSD_STATE),
        zeros(DEPTH, bp, CONV_WIDTH - 1, GDN_CONV_DIM), zeros(DEPTH, bp, GDN_HEADS, GDN_HEAD_DIM, GDN_HEAD_DIM),
        emb_ln_g, emb_ln_b, weights, cfg_p)

    bs, ts, _ = x_sample.shape
    tpad = -(-ts // BF16_ROWS) * BF16_ROWS
    past_len = page_table.shape[1] * PAGE_SIZE
    xs = jnp.pad(x_sample, ((0, 0), (0, tpad - ts), (0, 0)))
    cfg_s = dict(tm=min(512, bs * tpad), tm_proj=min(512, bs * tpad), ssd_chunk=tpad, gdn_chunk=tpad,
                 tq=tpad, tk=tpad)
    y_s, st_s = _trunk(
        xs, past_len + jnp.arange(tpad), ts, state_ssd_conv, state_ssd, state_gdn_conv, state_gdn,
        emb_ln_g, emb_ln_b, weights, cfg_s,
        paged=(cache_kv_latent, jnp.swapaxes(cache_k_rope, 2, 3), page_table))
    y_sample = y_s[:, :ts]
    s_lat, s_rope, s_ssd_conv, s_ssd, s_gdn_conv, s_gdn = st_s
    return (y_prompt, y_sample) + tuple(st_p) + (s_lat[:, :, :ts], s_rope[:, :, :ts], s_ssd_conv, s_ssd, s_gdn_conv, s_gdn)
```

```python
import functools
import math

import jax
import jax.numpy as jnp
import numpy as np
from jax import lax
from jax.experimental import pallas as pl
from jax.experimental.pallas import tpu as pltpu

F32 = jnp.float32
BF16 = jnp.bfloat16

D_MODEL = 1024
CONV_WIDTH = 4
SSD_HEADS = 16
SSD_HEAD_DIM = 64
SSD_WIDTH = SSD_HEADS * SSD_HEAD_DIM
SSD_GROUPS = 2
SSD_STATE = 128
SSD_CONV_DIM = SSD_WIDTH + 2 * SSD_GROUPS * SSD_STATE
SSD_CHUNK = 128
MLA_HEADS = 8
MLA_NOPE = 64
MLA_ROPE = 32
MLA_V = 64
MLA_WIDTH = MLA_HEADS * MLA_V
MLA_Q_RANK = 384
MLA_KV_RANK = 256
MLA_SCALE = (MLA_NOPE + MLA_ROPE) ** -0.5
ROPE_THETA = 10000.0
GDN_HEADS = 4
GDN_HEAD_DIM = 128
GDN_WIDTH = GDN_HEADS * GDN_HEAD_DIM
GDN_CONV_DIM = 3 * GDN_WIDTH
GDN_CHUNK = 64
MIX_WIDTH = SSD_WIDTH + MLA_WIDTH + GDN_WIDTH
IN_SIZES = (SSD_WIDTH, SSD_CONV_DIM, SSD_HEADS, MLA_Q_RANK, MLA_KV_RANK, MLA_ROPE, MLA_WIDTH,
            GDN_CONV_DIM, GDN_WIDTH, GDN_HEADS, GDN_HEADS)
IN_OFFSETS = tuple(int(o) for o in np.cumsum(IN_SIZES)[:-1])
DEPTH = 2
DEEPNORM_ALPHA = (2 * DEPTH) ** 0.25
LN_EPS = 1e-5
RMS_EPS = 1e-6
L2_EPS = 1e-6
PAGE_SIZE = 128

LANES = 128
SUBLANES = 8
BF16_ROWS = 16
ROPE_HALF = MLA_ROPE // 2
P_SSD = SSD_WIDTH + SSD_CONV_DIM + LANES
P_MLA = MLA_Q_RANK + MLA_KV_RANK + 2 * LANES + MLA_WIDTH
P_GDN = GDN_CONV_DIM + GDN_WIDTH + LANES
QK_WIDTH = MLA_KV_RANK + 2 * LANES
Q_SCALE = MLA_SCALE * math.log2(math.e)
VMEM_LIMIT = 56 * 1024 * 1024
PAGES_PER_GROUP = 32
GDN_BATCH_PER_STEP = 4

def _cparams(sem):
    return pltpu.CompilerParams(dimension_semantics=sem, vmem_limit_bytes=VMEM_LIMIT)


def _bdot(a, b):
    return jnp.dot(a.astype(BF16), b.astype(BF16), preferred_element_type=F32)


def _bdot_nt(a, b):
    return lax.dot_general(a.astype(BF16), b.astype(BF16), (((1,), (1,)), ((), ())),
                           preferred_element_type=F32)


def _fdot(a, b):
    return jnp.dot(a, b, precision=lax.Precision.HIGHEST, preferred_element_type=F32)


def _silu(x):
    return x * (1.0 / (1.0 + jnp.exp(-x)))


def _softplus(x):
    return jnp.maximum(x, 0.0) + jnp.log1p(jnp.exp(-jnp.abs(x)))


def _const_spec(shape):
    nd = len(shape)
    return pl.BlockSpec(shape, lambda *_: (0,) * nd)


def _layer_spec(stacked, l):
    rest = stacked.shape[1:]
    return pl.BlockSpec((None,) + rest, lambda *_: (l,) + (0,) * len(rest))


def _ln_kernel(x_ref, g_ref, b_ref, o_ref):
    x = x_ref[...]
    mu = jnp.mean(x, -1, keepdims=True)
    xc = x - mu
    var = jnp.mean(xc * xc, -1, keepdims=True)
    o_ref[...] = xc * lax.rsqrt(var + LN_EPS) * g_ref[...] + b_ref[...]


def _layer_norm(x, g, b, tm):
    m, d = x.shape
    return pl.pallas_call(
        _ln_kernel, grid=(m // tm,),
        in_specs=[pl.BlockSpec((tm, d), lambda i: (i, 0)), _const_spec((1, d)), _const_spec((1, d))],
        out_specs=pl.BlockSpec((tm, d), lambda i: (i, 0)),
        out_shape=jax.ShapeDtypeStruct((m, d), F32),
        compiler_params=_cparams(("parallel",)), name="emb_ln")(x, g.reshape(1, d), b.reshape(1, d))


def _proj_kernel(x_ref, w_ref, o_ref):
    o_ref[...] = _bdot(x_ref[...], w_ref[...])


def _project(x, w, l, tm, name):
    m, k = x.shape
    n = w.shape[2]
    return pl.pallas_call(
        _proj_kernel, grid=(m // tm,),
        in_specs=[pl.BlockSpec((tm, k), lambda i: (i, 0)), _layer_spec(w, l)],
        out_specs=pl.BlockSpec((tm, n), lambda i: (i, 0)),
        out_shape=jax.ShapeDtypeStruct((m, n), F32),
        compiler_params=_cparams(("parallel",)), name=name)(x, w)


def _conv_chunk(xbuf, x_new, cw_ref, L):
    xbuf[SUBLANES:SUBLANES + L, :] = x_new
    y = cw_ref[CONV_WIDTH - 1:CONV_WIDTH, :] * x_new
    for k in range(CONV_WIDTH - 1):
        off = SUBLANES - (CONV_WIDTH - 1) + k
        y = y + cw_ref[k:k + 1, :] * xbuf[off:off + L, :]
    return y


def _ssd_kernel(p_ref, cprev_ref, h0_ref, cw_ref, cb_ref, dtb_ref, alog_ref, dexp_ref, nw_ref, e_ref,
                y_ref, cnew_ref, hout_ref, xbuf, ht, *, L, tv):
    c = pl.program_id(1)
    nc = pl.num_programs(1)

    @pl.when(c == 0)
    def _():
        xbuf[0:SUBLANES, :] = cprev_ref[0]
        for g in range(SSD_GROUPS):
            ht[g] = jnp.transpose(h0_ref[0, g])

    z = p_ref[0, :, 0:SSD_WIDTH]
    xbc_raw = p_ref[0, :, SSD_WIDTH:SSD_WIDTH + SSD_CONV_DIM]
    dt_raw = p_ref[0, :, SSD_WIDTH + SSD_CONV_DIM:P_SSD]

    xbc = _silu(_conv_chunk(xbuf, xbc_raw, cw_ref, L) + cb_ref[...])
    cnew_ref[0] = xbuf[tv:tv + SUBLANES, :]
    xbuf[0:SUBLANES, :] = xbuf[L:L + SUBLANES, :]

    xs = xbc[:, 0:SSD_WIDTH]
    gs = SSD_GROUPS * SSD_STATE
    bm = xbc[:, SSD_WIDTH:SSD_WIDTH + gs]
    cm = xbc[:, SSD_WIDTH + gs:SSD_WIDTH + 2 * gs]

    dt = _softplus(dt_raw + dtb_ref[...])
    if tv < L:
        rows = lax.broadcasted_iota(jnp.int32, (L, LANES), 0)
        dt = jnp.where(rows < tv, dt, 0.0)
    a = -jnp.exp(alog_ref[...])
    da = dt * a
    r_i = lax.broadcasted_iota(jnp.int32, (L, L), 0)
    c_i = lax.broadcasted_iota(jnp.int32, (L, L), 1)
    causal = c_i <= r_i
    tri = jnp.where(causal, 1.0, 0.0).astype(F32)
    acum = _fdot(tri, da)
    last = acum[L - 1:L, :]
    ea = jnp.exp(acum)
    wdec = jnp.exp(last - acum)

    def pad_rows(v):
        if L == LANES:
            return v
        return jnp.concatenate([v, jnp.zeros((LANES - L, v.shape[1]), v.dtype)], axis=0)

    acum_t = jnp.transpose(pad_rows(acum))

    def hilo(v):
        hi = v.astype(BF16)
        lo = (v - hi.astype(F32)).astype(BF16)
        return hi, lo

    parts = []
    for v in (dt, ea, wdec):
        parts.extend(hilo(v))
    stacked = jnp.concatenate(parts, axis=0)
    expanded = jnp.dot(stacked, e_ref[...], preferred_element_type=F32)
    dt_e = expanded[0:L] + expanded[L:2 * L]
    ea_e = expanded[2 * L:3 * L] + expanded[3 * L:4 * L]
    wd_e = expanded[4 * L:5 * L] + expanded[5 * L:6 * L]

    xdt = xs * dt_e
    xdt_b = xdt.astype(BF16)
    lane = lax.broadcasted_iota(jnp.int32, (L, LANES), 1)
    hpg = SSD_HEADS // SSD_GROUPS

    y_parts = []
    for g in range(SSD_GROUPS):
        bg = bm[:, g * SSD_STATE:(g + 1) * SSD_STATE]
        cg = cm[:, g * SSD_STATE:(g + 1) * SSD_STATE]
        cb = _bdot_nt(cg, bg)
        for j in range(hpg // 2):
            pair = g * (hpg // 2) + j
            xp = xdt_b[:, pair * LANES:(pair + 1) * LANES]
            ys = []
            for h in (2 * pair, 2 * pair + 1):
                seg = acum[:, h:h + 1] - acum_t[h:h + 1, 0:L]
                dec = jnp.exp(jnp.where(causal, seg, -jnp.inf))
                ys.append(_bdot(cb * dec, xp))
            y_parts.append(jnp.where(lane < SSD_HEAD_DIM, ys[0], ys[1]))
    y_in = jnp.concatenate(y_parts, axis=1)

    gw = hpg * SSD_HEAD_DIM
    y_st_parts = []
    for g in range(SSD_GROUPS):
        cg = cm[:, g * SSD_STATE:(g + 1) * SSD_STATE]
        y_st_parts.append(_bdot(cg, ht[g]))
    y_st = jnp.concatenate(y_st_parts, axis=1) * ea_e

    xw = (xdt * wd_e)
    for g in range(SSD_GROUPS):
        bg_t = jnp.transpose(pad_rows(bm[:, g * SSD_STATE:(g + 1) * SSD_STATE]))
        xw_g = pad_rows(xw[:, g * gw:(g + 1) * gw])
        ht[g] = ht[g] * ea_e[L - 1:L, g * gw:(g + 1) * gw] + _bdot(bg_t, xw_g)

    y = (y_in + y_st + dexp_ref[...] * xs) * _silu(z)
    outs = []
    for g in range(SSD_GROUPS):
        yg = y[:, g * gw:(g + 1) * gw]
        ms = jnp.mean(yg * yg, -1, keepdims=True)
        outs.append(yg * lax.rsqrt(ms + RMS_EPS) * nw_ref[:, g * gw:(g + 1) * gw])
    y_ref[0] = jnp.concatenate(outs, axis=1).astype(y_ref.dtype)

    @pl.when(c == nc - 1)
    def _():
        for g in range(SSD_GROUPS):
            hout_ref[0, g] = jnp.transpose(ht[g])


def _ssd_scan(p_ssd, conv_prev8, h0, l, cw, cb, dtb, alog, dexp, nw, emat, *, L, tv):
    b, t, _ = p_ssd.shape
    nc = t // L
    gw = SSD_WIDTH // SSD_GROUPS
    kern = functools.partial(_ssd_kernel, L=L, tv=tv)
    return pl.pallas_call(
        kern, grid=(b, nc),
        in_specs=[pl.BlockSpec((1, L, P_SSD), lambda i, c: (i, c, 0)),
                  pl.BlockSpec((None, 1, SUBLANES, SSD_CONV_DIM), lambda i, c: (l, i, 0, 0)),
                  pl.BlockSpec((None, 1, SSD_GROUPS, gw, SSD_STATE), lambda i, c: (l, i, 0, 0, 0)),
                  _layer_spec(cw, l), _layer_spec(cb, l), _layer_spec(dtb, l), _layer_spec(alog, l),
                  _layer_spec(dexp, l), _layer_spec(nw, l), _const_spec((LANES, SSD_WIDTH))],
        out_specs=[pl.BlockSpec((1, L, SSD_WIDTH), lambda i, c: (i, c, 0)),
                   pl.BlockSpec((1, SUBLANES, SSD_CONV_DIM), lambda i, c: (i, 0, 0)),
                   pl.BlockSpec((1, SSD_GROUPS, gw, SSD_STATE), lambda i, c: (i, 0, 0, 0))],
        out_shape=[jax.ShapeDtypeStruct((b, t, SSD_WIDTH), BF16),
                   jax.ShapeDtypeStruct((b, SUBLANES, SSD_CONV_DIM), F32),
                   jax.ShapeDtypeStruct((b, SSD_GROUPS, gw, SSD_STATE), F32)],
        scratch_shapes=[pltpu.VMEM((L + SUBLANES, SSD_CONV_DIM), F32),
                        pltpu.VMEM((SSD_GROUPS, SSD_STATE, gw), F32)],
        compiler_params=_cparams(("parallel", "arbitrary")), name="ssd_scan",
    )(p_ssd, conv_prev8, h0, cw, cb, dtb, alog, dexp, nw, emat)


def _gdn_chunk(p_ref, cw_ref, bias_ref, alog_ref, nw_ref, y_ref, cnew_ref, xbuf, st, *, L, tv):
    H = GDN_HEADS
    D = GDN_HEAD_DIM
    R = H * L
    S = max(R, LANES)

    qkv_raw = p_ref[:, 0:GDN_CONV_DIM]
    z = p_ref[:, GDN_CONV_DIM:GDN_CONV_DIM + GDN_WIDTH]
    ba = p_ref[:, GDN_CONV_DIM + GDN_WIDTH:P_GDN]

    qkv = _silu(_conv_chunk(xbuf, qkv_raw, cw_ref, L))
    cnew_ref[...] = xbuf[tv:tv + SUBLANES, :]
    xbuf[0:SUBLANES, :] = xbuf[L:L + SUBLANES, :]
    yield

    beta_f = 1.0 / (1.0 + jnp.exp(-ba))
    g_f = -jnp.exp(alog_ref[...]) * _softplus(ba + bias_ref[...])
    if tv < L:
        rows = lax.broadcasted_iota(jnp.int32, (L, LANES), 0)
        beta_f = jnp.where(rows < tv, beta_f, 0.0)
        g_f = jnp.where(rows < tv, g_f, 0.0)
    r_i = lax.broadcasted_iota(jnp.int32, (L, L), 0)
    c_i = lax.broadcasted_iota(jnp.int32, (L, L), 1)
    tri = jnp.where(c_i <= r_i, 1.0, 0.0).astype(F32)
    gcum_f = _fdot(tri, g_f)
    glast_f = jnp.broadcast_to(gcum_f[L - 1:L, :], (L, LANES))

    def pad_s(v):
        if R == S:
            return v
        return jnp.concatenate([v, jnp.zeros((S - R, v.shape[1]), v.dtype)], axis=0)

    def stack(v):
        return pad_s(jnp.concatenate([v[:, h * D:(h + 1) * D] for h in range(H)], axis=0))

    def col(v, off):
        return pad_s(jnp.concatenate([v[:, off + h:off + h + 1] for h in range(H)], axis=0))

    q_s = stack(qkv[:, 0:GDN_WIDTH])
    k_s = stack(qkv[:, GDN_WIDTH:2 * GDN_WIDTH])
    v_s = stack(qkv[:, 2 * GDN_WIDTH:3 * GDN_WIDTH])
    z_s = stack(z)
    q_s = q_s * lax.rsqrt(jnp.sum(q_s * q_s, -1, keepdims=True) + L2_EPS) * (D ** -0.5)
    k_s = k_s * lax.rsqrt(jnp.sum(k_s * k_s, -1, keepdims=True) + L2_EPS)
    beta = col(beta_f, 0)
    gcum = col(gcum_f, H)
    glast = col(glast_f, H)
    yield

    cmat = jnp.broadcast_to(gcum, (S, S))
    diff = cmat - jnp.transpose(cmat)
    rs = lax.broadcasted_iota(jnp.int32, (S, S), 0)
    cs = lax.broadcasted_iota(jnp.int32, (S, S), 1)
    if L & (L - 1) == 0:
        sh = L.bit_length() - 1
        same = (rs >> sh) == (cs >> sh)
    else:
        same = (rs // L) == (cs // L)
    incl = same & (cs <= rs)
    strict = same & (cs < rs)
    dec = jnp.exp(jnp.where(incl, diff, -jnp.inf))

    kb = k_s * beta
    n_mat = -jnp.where(strict, _bdot_nt(kb, k_s) * dec, 0.0)
    eye = jnp.where(rs == cs, 1.0, 0.0).astype(F32)
    t_mat = eye + n_mat
    npow = n_mat
    yield
    span = 2
    while span < L:
        npow = _bdot(npow, npow)
        yield
        t_mat = t_mat + _bdot(t_mat, npow)
        yield
        span *= 2

    eg = jnp.exp(gcum)
    t_b = t_mat.astype(BF16)
    u = _bdot(t_b, v_s * beta)
    w = _bdot(t_b, kb * eg)
    qg = q_s * eg
    attn = jnp.where(incl, _bdot_nt(q_s, k_s) * dec, 0.0)
    yield
    ws = []
    qs_ = []
    for h in range(H):
        sh_b = st[h].astype(BF16)
        ws.append(_bdot(w[h * L:(h + 1) * L], sh_b))
        qs_.append(_bdot(qg[h * L:(h + 1) * L], sh_b))
    v_new = u - pad_s(jnp.concatenate(ws, axis=0))
    yield
    o = pad_s(jnp.concatenate(qs_, axis=0)) + _bdot(attn, v_new)

    kd = k_s * jnp.exp(glast - gcum)
    kd_t = jnp.transpose(kd).astype(BF16)
    row_head = lax.broadcasted_iota(jnp.int32, (S, D), 0)
    eg_last = jnp.exp(glast)
    for h in range(H):
        vm = jnp.where((row_head >= h * L) & (row_head < (h + 1) * L), v_new, 0.0)
        st[h] = st[h] * eg_last[h * L:h * L + 1, :] + _bdot(kd_t, vm)
    yield

    ms = jnp.mean(o * o, -1, keepdims=True)
    o = o * lax.rsqrt(ms + RMS_EPS) * nw_ref[...] * _silu(z_s)
    y_ref[...] = jnp.concatenate([o[h * L:(h + 1) * L] for h in range(H)], axis=1).astype(y_ref.dtype)


def _gdn_kernel(p_ref, cprev_ref, s0_ref, cw_ref, bias_ref, alog_ref, nw_ref,
                y_ref, cnew_ref, sout_ref, xbuf, st, *, L, tv, nbat):
    c = pl.program_id(1)
    nc = pl.num_programs(1)

    @pl.when(c == 0)
    def _():
        xbuf[:, 0:SUBLANES, :] = cprev_ref[...]
        st[...] = s0_ref[...]

    live = [_gdn_chunk(p_ref.at[i], cw_ref, bias_ref, alog_ref, nw_ref, y_ref.at[i], cnew_ref.at[i],
                       xbuf.at[i], st.at[i], L=L, tv=tv) for i in range(nbat)]
    while live:
        live = [g for g in live if next(g, StopIteration) is not StopIteration]

    @pl.when(c == nc - 1)
    def _():
        sout_ref[...] = st[...]


def _gdn_scan(p_gdn, conv_prev8, s0, l, cw, bias, alog, nw, *, L, tv):
    b, t, _ = p_gdn.shape
    nc = t // L
    nbat = math.gcd(b, GDN_BATCH_PER_STEP)
    kern = functools.partial(_gdn_kernel, L=L, tv=tv, nbat=nbat)
    sshape = (nbat, GDN_HEADS, GDN_HEAD_DIM, GDN_HEAD_DIM)
    return pl.pallas_call(
        kern, grid=(b // nbat, nc),
        in_specs=[pl.BlockSpec((nbat, L, P_GDN), lambda i, c: (i, c, 0)),
                  pl.BlockSpec((None, nbat, SUBLANES, GDN_CONV_DIM), lambda i, c: (l, i, 0, 0)),
                  pl.BlockSpec((None,) + sshape, lambda i, c: (l, i, 0, 0, 0)),
                  _layer_spec(cw, l), _layer_spec(bias, l), _layer_spec(alog, l), _layer_spec(nw, l)],
        out_specs=[pl.BlockSpec((nbat, L, GDN_WIDTH), lambda i, c: (i, c, 0)),
                   pl.BlockSpec((nbat, SUBLANES, GDN_CONV_DIM), lambda i, c: (i, 0, 0)),
                   pl.BlockSpec(sshape, lambda i, c: (i, 0, 0, 0))],
        out_shape=[jax.ShapeDtypeStruct((b, t, GDN_WIDTH), BF16),
                   jax.ShapeDtypeStruct((b, SUBLANES, GDN_CONV_DIM), F32),
                   jax.ShapeDtypeStruct((b,) + sshape[1:], F32)],
        scratch_shapes=[pltpu.VMEM((nbat, L + SUBLANES, GDN_CONV_DIM), F32),
                        pltpu.VMEM(sshape, F32)],
        compiler_params=_cparams(("parallel", "arbitrary")), name="gdn_scan",
    )(p_gdn, conv_prev8, s0, cw, bias, alog, nw)


def _mla_prep_kernel(p_ref, cos_ref, sin_ref, qnw_ref, kvnw_ref, wuq_ref, wuk_ref,
                     q_ref, kcat_ref, kt_ref, ckv_ref, kr_ref, gate_ref, *, nb, tt, tq_t):
    cq = p_ref[:, 0:MLA_Q_RANK]
    ckv = p_ref[:, MLA_Q_RANK:MLA_Q_RANK + MLA_KV_RANK]
    kr = p_ref[:, MLA_Q_RANK + MLA_KV_RANK:MLA_Q_RANK + MLA_KV_RANK + 2 * LANES]
    gate = p_ref[:, MLA_Q_RANK + MLA_KV_RANK + 2 * LANES:P_MLA]
    cos = cos_ref[...]
    sin = sin_ref[...]

    cqn = cq * lax.rsqrt(jnp.mean(cq * cq, -1, keepdims=True) + RMS_EPS) * qnw_ref[...]
    q = _bdot(cqn, wuq_ref[...])
    nope_w = MLA_HEADS * MLA_NOPE
    x1 = q[:, nope_w:nope_w + LANES]
    x2 = q[:, nope_w + LANES:nope_w + 2 * LANES]
    r_all = jnp.concatenate([x1 * cos - x2 * sin, x2 * cos + x1 * sin], axis=1) * Q_SCALE

    ckvn = ckv * lax.rsqrt(jnp.mean(ckv * ckv, -1, keepdims=True) + RMS_EPS) * kvnw_ref[...]
    k1 = kr[:, 0:LANES]
    k2 = kr[:, LANES:2 * LANES]
    kr_rot = jnp.concatenate([k1 * cos - k2 * sin, k2 * cos + k1 * sin], axis=1)
    ckv_ref[...] = ckvn
    kr_ref[...] = kr_rot
    kcat_ref[...] = jnp.concatenate([ckvn, kr_rot], axis=1).astype(BF16)
    kt_ref[...] = jnp.transpose(ckvn).astype(BF16)
    gate_ref[...] = _silu(gate)

    tm = q.shape[0]
    if tq_t is None:
        lane = lax.broadcasted_iota(jnp.int32, (1, 2 * LANES), 1)
        head_of_lane = (lane & (LANES - 1)) >> 4
    else:
        r_all_t = jnp.transpose(r_all)
        row_i = lax.broadcasted_iota(jnp.int32, (2 * LANES, 1), 0)
        head_of_row = (row_i & (LANES - 1)) >> 4
    for pair in range(MLA_HEADS // 2):
        qn = q[:, pair * LANES:(pair + 1) * LANES] * Q_SCALE
        qlat = _bdot(qn, wuk_ref[pair])
        for i in range(2):
            h = 2 * pair + i
            ql = qlat[:, i * MLA_KV_RANK:(i + 1) * MLA_KV_RANK]
            if tq_t is None:
                qr = jnp.where(head_of_lane == h, r_all, 0.0).astype(BF16)
                q_ref[:, h, :, 0:MLA_KV_RANK] = ql.astype(BF16).reshape(nb, tt, MLA_KV_RANK)
                q_ref[:, h, :, MLA_KV_RANK:QK_WIDTH] = qr.reshape(nb, tt, 2 * LANES)
            else:
                ql_t = jnp.transpose(ql).astype(BF16)
                qr_t = jnp.where(head_of_row == h, r_all_t, 0.0).astype(BF16)
                for qb in range(tm // tq_t):
                    cols = slice(h * tq_t, (h + 1) * tq_t)
                    toks = slice(qb * tq_t, (qb + 1) * tq_t)
                    q_ref[0, qb, 0:MLA_KV_RANK, cols] = ql_t[:, toks]
                    q_ref[0, qb, MLA_KV_RANK:QK_WIDTH, cols] = qr_t[:, toks]


def _head_major_block(t, tm):
    if tm <= t:
        npb = t // tm
        return 1, tm, (lambda i: (i // npb, 0, i % npb, 0))
    return tm // t, t, (lambda i: (i, 0, 0, 0))


def _mla_prep(p_mla, cos_t, sin_t, l, qnw, kvnw, wuq, wuk_pairs, b, t, tm, tq_t=None):
    m = p_mla.shape[0]
    nt = cos_t.shape[0] // tm
    row = lambda i: (i, 0)
    tab = lambda i: (i % nt, 0)
    nb, tt, hm_map = _head_major_block(t, tm)
    kern = functools.partial(_mla_prep_kernel, nb=nb, tt=tt, tq_t=tq_t)
    if tq_t is None:
        q_spec = pl.BlockSpec((nb, MLA_HEADS, tt, QK_WIDTH), hm_map)
        q_shape = jax.ShapeDtypeStruct((b, MLA_HEADS, t, QK_WIDTH), BF16)
    else:
        npb = t // tm
        q_spec = pl.BlockSpec((1, tm // tq_t, QK_WIDTH, MLA_HEADS * tq_t), lambda i: (i // npb, i % npb, 0, 0))
        q_shape = jax.ShapeDtypeStruct((b, t // tq_t, QK_WIDTH, MLA_HEADS * tq_t), BF16)
    return pl.pallas_call(
        kern, grid=(m // tm,),
        in_specs=[pl.BlockSpec((tm, P_MLA), row), pl.BlockSpec((tm, LANES), tab), pl.BlockSpec((tm, LANES), tab),
                  _layer_spec(qnw, l), _layer_spec(kvnw, l), _layer_spec(wuq, l), _layer_spec(wuk_pairs, l)],
        out_specs=[q_spec, pl.BlockSpec((tm, QK_WIDTH), row),
                   pl.BlockSpec((MLA_KV_RANK, tm), lambda i: (0, i)),
                   pl.BlockSpec((tm, MLA_KV_RANK), row), pl.BlockSpec((tm, 2 * LANES), row),
                   pl.BlockSpec((tm, MLA_WIDTH), row)],
        out_shape=[q_shape,
                   jax.ShapeDtypeStruct((m, QK_WIDTH), BF16), jax.ShapeDtypeStruct((MLA_KV_RANK, m), BF16),
                   jax.ShapeDtypeStruct((m, MLA_KV_RANK), F32), jax.ShapeDtypeStruct((m, 2 * LANES), F32),
                   jax.ShapeDtypeStruct((m, MLA_WIDTH), F32)],
        compiler_params=_cparams(("parallel",)), name="mla_prep",
    )(p_mla, cos_t, sin_t, qnw, kvnw, wuq, wuk_pairs)


def _softmax_rows(s, m_scr, l_scr):
    n = s.shape[1]
    m_prev = m_scr[...]
    m_new = jnp.maximum(m_prev, jnp.max(s, -1, keepdims=True))
    alpha = jnp.exp2(m_prev - m_new)
    m_wide = jnp.concatenate([m_new] * (n // LANES), axis=1) if n >= LANES else m_new[:, 0:n]
    p = jnp.exp2(s - m_wide)
    l_scr[...] = alpha * l_scr[...] + jnp.sum(p, -1, keepdims=True)
    m_scr[...] = m_new
    return p, alpha


def _softmax_init(m_scr, l_scr, acc_scr):
    m_scr[...] = jnp.full(m_scr.shape, -jnp.inf, F32)
    l_scr[...] = jnp.zeros(l_scr.shape, F32)
    acc_scr[...] = jnp.zeros(acc_scr.shape, F32)


def _flash_kernel(qi_ref, ki_ref, q_ref, k_ref, kt_ref, o_ref, m_scr, l_scr, acc_scr, *, tq, tk):
    step_i = pl.program_id(1)
    qi = qi_ref[step_i]
    ki = ki_ref[step_i]
    rows = tq * MLA_HEADS

    @pl.when(ki == 0)
    def _():
        _softmax_init(m_scr, l_scr, acc_scr)

    def step(masked, nkeys):
        s_t = jnp.dot(k_ref[0, 0:nkeys, :], q_ref[0, 0], preferred_element_type=F32)
        if masked:
            k_pos = ki * tk + lax.broadcasted_iota(jnp.int32, (nkeys, rows), 0)
            q_pos = qi * tq + (lax.broadcasted_iota(jnp.int32, (nkeys, rows), 1) & (tq - 1))
            s_t = jnp.where(k_pos <= q_pos, s_t, -jnp.inf)
        m_prev = m_scr[...]
        m_new = jnp.maximum(m_prev, jnp.max(s_t, axis=0, keepdims=True))
        alpha = jnp.exp2(m_prev - m_new)
        p_t = jnp.exp2(s_t - m_new)
        l_scr[...] = alpha * l_scr[...] + jnp.sum(p_t, axis=0, keepdims=True)
        m_scr[...] = m_new
        acc_scr[...] = acc_scr[...] * alpha + jnp.dot(kt_ref[:, 0:nkeys], p_t.astype(BF16),
                                                      preferred_element_type=F32)

    crosses = (ki + 1) * tk > qi * tq + 1
    short = (qi + 1) * tq - ki * tk
    if tq < tk and tk % tq == 0:
        use_short = crosses & (short == tq)

        @pl.when(use_short)
        def _():
            step(True, tq)
    else:
        use_short = False

    @pl.when(crosses & jnp.logical_not(use_short))
    def _():
        step(True, tk)

    @pl.when(jnp.logical_not(crosses))
    def _():
        step(False, tk)

    @pl.when(ki == ((qi + 1) * tq - 1) // tk)
    def _():
        o_t = acc_scr[...] * (1.0 / l_scr[...])
        o_ref[0] = jnp.transpose(o_t).astype(o_ref.dtype).reshape(MLA_HEADS, tq, MLA_KV_RANK)


def _flash_attention(q_t, kcat, kt, *, tq, tk):
    b, nq, _, rows = q_t.shape
    t = nq * tq
    nkb = t // tk
    kern = functools.partial(_flash_kernel, tq=tq, tk=tk)
    pairs = [(qi, ki) for qi in range(nq) for ki in range(((qi + 1) * tq - 1) // tk + 1)]
    qi_arr = jnp.asarray([p[0] for p in pairs], jnp.int32)
    ki_arr = jnp.asarray([p[1] for p in pairs], jnp.int32)
    grid_spec = pltpu.PrefetchScalarGridSpec(
        num_scalar_prefetch=2, grid=(b, len(pairs)),
        in_specs=[pl.BlockSpec((1, 1, QK_WIDTH, rows), lambda i, s, qa, ka: (i, qa[s], 0, 0)),
                  pl.BlockSpec((1, tk, QK_WIDTH), lambda i, s, qa, ka: (i, ka[s], 0)),
                  pl.BlockSpec((MLA_KV_RANK, tk), lambda i, s, qa, ka: (0, i * nkb + ka[s]))],
        out_specs=pl.BlockSpec((1, MLA_HEADS, tq, MLA_KV_RANK), lambda i, s, qa, ka: (i, 0, qa[s], 0)),
        scratch_shapes=[pltpu.VMEM((1, rows), F32), pltpu.VMEM((1, rows), F32),
                        pltpu.VMEM((MLA_KV_RANK, rows), F32)])
    return pl.pallas_call(
        kern, grid_spec=grid_spec,
        out_shape=jax.ShapeDtypeStruct((b, MLA_HEADS, t, MLA_KV_RANK), BF16),
        compiler_params=_cparams(("parallel", "arbitrary")), name="mla_flash",
    )(qi_arr, ki_arr, q_t, kcat, kt)


def _paged_kernel(pt_ref, q_ref, knew_ref, sel_ref, rowsel_ref, rowselt_ref, lat_hbm, rope_hbm, o_ref,
                  latbuf, ropebuf, sems, m_scr, l_scr, acc_scr, *, tpad, tv, layer, ngroups):
    bi = pl.program_id(0)
    nb = pl.num_programs(0)
    npg = PAGES_PER_GROUP
    rows = tpad * MLA_HEADS
    rows_v = tv * MLA_HEADS
    reps = MLA_KV_RANK // LANES

    def copies(seq, grp, slot):
        out = []
        for i in range(npg):
            page = pt_ref[seq, grp * npg + i]
            dst = pl.ds(i * PAGE_SIZE, PAGE_SIZE)
            out.append(pltpu.make_async_copy(lat_hbm.at[layer, page], latbuf.at[slot, dst], sems.at[0, slot, i]))
            out.append(pltpu.make_async_copy(rope_hbm.at[layer, page], ropebuf.at[slot, :, dst], sems.at[1, slot, i]))
        return out

    def start(seq, grp, slot):
        for cp in copies(seq, grp, slot):
            cp.start()

    def wait(seq, grp, slot):
        for cp in copies(seq, grp, slot):
            cp.wait()

    @pl.when(bi == 0)
    def _():
        for g in range(ngroups):
            start(0, g, g)

    for a in range(2):
        _softmax_init(m_scr.at[a], l_scr.at[a], acc_scr.at[a])
    q_v = jnp.dot(rowsel_ref[...], q_ref[0].reshape(rows, QK_WIDTH), preferred_element_type=F32).astype(BF16)
    q_lat = q_v[:, 0:MLA_KV_RANK]
    q_rd = jnp.dot(q_v[:, MLA_KV_RANK:QK_WIDTH], sel_ref[...], preferred_element_type=F32).astype(BF16)

    nt = (((1,), (1,)), ((), ()))

    def accumulate(a, s, v_b):
        p, alpha = _softmax_rows(s, m_scr.at[a], l_scr.at[a])
        yield
        acc_scr[a] = (acc_scr[a] * jnp.concatenate([alpha] * reps, axis=1)
                      + jnp.dot(p.astype(BF16), v_b, preferred_element_type=F32))

    def consume(a, slot):
        lat_b = latbuf[slot].astype(BF16)
        rope_b = ropebuf[slot].astype(BF16)
        yield
        s = (lax.dot_general(q_lat, lat_b, nt, preferred_element_type=F32)
             + jnp.dot(q_rd, rope_b, preferred_element_type=F32))
        yield
        yield from accumulate(a, s, lat_b)

    def lockstep(gens):
        while gens:
            gens = [g for g in gens if next(g, StopIteration) is not StopIteration]

    k_b = knew_ref[0]
    s2 = lax.dot_general(q_v, k_b, nt, preferred_element_type=F32)
    q_pos = lax.rem(lax.broadcasted_iota(jnp.int32, (rows_v, tpad), 0), tv)
    k_pos = lax.broadcasted_iota(jnp.int32, (rows_v, tpad), 1)
    lockstep([accumulate(0, jnp.where(k_pos <= q_pos, s2, -jnp.inf), k_b[:, 0:MLA_KV_RANK])])

    for g0 in range(0, ngroups, 2):
        wait(bi, g0, g0)
        wait(bi, g0 + 1, g0 + 1)
        lockstep([consume(0, g0), consume(1, g0 + 1)])

        @pl.when(bi + 1 < nb)
        def _():
            start(bi + 1, g0, g0)
            start(bi + 1, g0 + 1, g0 + 1)

    m0, m1 = m_scr[0], m_scr[1]
    m = jnp.maximum(m0, m1)
    a0, a1 = jnp.exp2(m0 - m), jnp.exp2(m1 - m)
    inv_l = 1.0 / (a0 * l_scr[0] + a1 * l_scr[1])
    wide = lambda v: jnp.concatenate([v] * reps, axis=1)
    o_v = ((acc_scr[0] * wide(a0) + acc_scr[1] * wide(a1)) * wide(inv_l)).astype(BF16)
    o = jnp.dot(rowselt_ref[...], o_v, preferred_element_type=F32)
    o_ref[0] = o.astype(o_ref.dtype).reshape(MLA_HEADS, tpad, MLA_KV_RANK)


def _row_select_matrix(tpad, tv):
    s = np.zeros((MLA_HEADS * tv, MLA_HEADS * tpad), np.float32)
    for h in range(MLA_HEADS):
        for t in range(tv):
            s[h * tv + t, h * tpad + t] = 1.0
    return s


def _paged_attention(q, knew, cache_lat, cache_rope_t, page_table, sel, *, tpad, tv, layer):
    b = q.shape[0]
    n_pages = page_table.shape[1]
    npg = PAGES_PER_GROUP
    ngroups = n_pages // npg
    assert n_pages % (2 * npg) == 0, "page groups are consumed in slot pairs"
    rows = tpad * MLA_HEADS
    rows_v = tv * MLA_HEADS
    gk = npg * PAGE_SIZE
    rowsel = _row_select_matrix(tpad, tv)
    kern = functools.partial(_paged_kernel, tpad=tpad, tv=tv, layer=layer, ngroups=ngroups)
    stat = pltpu.VMEM((2, rows_v, LANES), F32)
    grid_spec = pltpu.PrefetchScalarGridSpec(
        num_scalar_prefetch=1, grid=(b,),
        in_specs=[pl.BlockSpec((1, MLA_HEADS, tpad, QK_WIDTH), lambda bi, pt: (bi, 0, 0, 0)),
                  pl.BlockSpec((1, tpad, QK_WIDTH), lambda bi, pt: (bi, 0, 0)),
                  pl.BlockSpec(sel.shape, lambda bi, pt: (0, 0)),
                  pl.BlockSpec((rows_v, rows), lambda bi, pt: (0, 0)),
                  pl.BlockSpec((rows, rows_v), lambda bi, pt: (0, 0)),
                  pl.BlockSpec(memory_space=pl.ANY), pl.BlockSpec(memory_space=pl.ANY)],
        out_specs=pl.BlockSpec((1, MLA_HEADS, tpad, MLA_KV_RANK), lambda bi, pt: (bi, 0, 0, 0)),
        scratch_shapes=[pltpu.VMEM((ngroups, gk, MLA_KV_RANK), F32), pltpu.VMEM((ngroups, MLA_ROPE, gk), F32),
                        pltpu.SemaphoreType.DMA((2, ngroups, npg)),
                        stat, stat, pltpu.VMEM((2, rows_v, MLA_KV_RANK), F32)])
    return pl.pallas_call(
        kern, grid_spec=grid_spec,
        out_shape=jax.ShapeDtypeStruct((b, MLA_HEADS, tpad, MLA_KV_RANK), BF16),
        compiler_params=_cparams(("arbitrary",)), name="mla_paged",
    )(page_table, q, knew, sel, jnp.asarray(rowsel, BF16), jnp.asarray(rowsel.T, BF16), cache_lat, cache_rope_t)


def _out_kernel(x_ref, yssd_ref, olat_ref, gate_ref, ygdn_ref, wuv_ref, wout_ref, g_ref, b_ref, o_ref, *, tm):
    y_mla = []
    for pair in range(MLA_HEADS // 2):
        o_pair = None
        for i in range(2):
            o_h = olat_ref[:, 2 * pair + i].reshape(tm, MLA_KV_RANK)
            part = jnp.dot(o_h, wuv_ref[pair, i * MLA_KV_RANK:(i + 1) * MLA_KV_RANK, :], preferred_element_type=F32)
            o_pair = part if o_pair is None else o_pair + part
        y_mla.append(o_pair * gate_ref[:, pair * LANES:(pair + 1) * LANES])
    y_mla = jnp.concatenate(y_mla, axis=1)
    mm = (_bdot(yssd_ref[...], wout_ref[0:SSD_WIDTH, :])
          + _bdot(y_mla, wout_ref[SSD_WIDTH:SSD_WIDTH + MLA_WIDTH, :])
          + _bdot(ygdn_ref[...], wout_ref[SSD_WIDTH + MLA_WIDTH:MIX_WIDTH, :]))
    v = DEEPNORM_ALPHA * x_ref[...] + mm
    mu = jnp.mean(v, -1, keepdims=True)
    vc = v - mu
    var = jnp.mean(vc * vc, -1, keepdims=True)
    o_ref[...] = vc * lax.rsqrt(var + LN_EPS) * g_ref[...] + b_ref[...]


def _out_proj(x, y_ssd, o_lat, gate, y_gdn, l, wuv_pairs, wout, g, b, tm):
    m = x.shape[0]
    t = o_lat.shape[2]
    row = lambda i: (i, 0)
    nb, tt, hm_map = _head_major_block(t, tm)
    return pl.pallas_call(
        functools.partial(_out_kernel, tm=tm), grid=(m // tm,),
        in_specs=[pl.BlockSpec((tm, D_MODEL), row), pl.BlockSpec((tm, SSD_WIDTH), row),
                  pl.BlockSpec((nb, MLA_HEADS, tt, MLA_KV_RANK), hm_map), pl.BlockSpec((tm, MLA_WIDTH), row),
                  pl.BlockSpec((tm, GDN_WIDTH), row),
                  _layer_spec(wuv_pairs, l), _layer_spec(wout, l), _layer_spec(g, l), _layer_spec(b, l)],
        out_specs=pl.BlockSpec((tm, D_MODEL), row),
        out_shape=jax.ShapeDtypeStruct((m, D_MODEL), F32),
        compiler_params=_cparams(("parallel",)), name="out_proj",
    )(x, y_ssd, o_lat, gate, y_gdn, wuv_pairs, wout, g, b)


def _pad_last(v, n=LANES, before=0):
    return jnp.pad(v, ((0, 0),) * (v.ndim - 1) + ((before, n - before - v.shape[-1]),))


def _prep_weights(w_in, ssd_conv_w, ssd_conv_b, ssd_dt_bias, ssd_a_log, ssd_d, ssd_norm_w,
                  mla_q_norm_w, mla_w_uq, mla_kv_norm_w, mla_w_uk, mla_w_uv,
                  gdn_conv_w, gdn_dt_bias, gdn_a_log, gdn_norm_w, w_out, ln_g, ln_b):
    depth = w_in.shape[0]
    (w_z, w_xbc, w_dt, w_cq, w_ckv, w_kr, w_gate, w_qkv, w_gz, w_gb, w_ga) = jnp.split(w_in, IN_OFFSETS, axis=2)
    w_ssd = jnp.concatenate([w_z, w_xbc, _pad_last(w_dt)], axis=2).astype(BF16)
    kr_tiled = jnp.concatenate([jnp.tile(w_kr[..., :ROPE_HALF], (1, 1, MLA_HEADS)),
                                jnp.tile(w_kr[..., ROPE_HALF:], (1, 1, MLA_HEADS))], axis=2)
    w_mla = jnp.concatenate([w_cq, w_ckv, kr_tiled, w_gate], axis=2).astype(BF16)
    w_gdn = jnp.concatenate([w_qkv, w_gz, _pad_last(jnp.concatenate([w_gb, w_ga], axis=2))], axis=2).astype(BF16)

    uq = mla_w_uq.reshape(depth, MLA_Q_RANK, MLA_HEADS, MLA_NOPE + MLA_ROPE)
    wuq = jnp.concatenate([uq[..., :MLA_NOPE].reshape(depth, MLA_Q_RANK, -1),
                           uq[..., MLA_NOPE:MLA_NOPE + ROPE_HALF].reshape(depth, MLA_Q_RANK, -1),
                           uq[..., MLA_NOPE + ROPE_HALF:].reshape(depth, MLA_Q_RANK, -1)], axis=2).astype(BF16)
    hp = MLA_HEADS // 2
    uk = jnp.transpose(mla_w_uk, (0, 2, 3, 1)).reshape(depth, hp, 2, MLA_NOPE, MLA_KV_RANK)
    uv = jnp.transpose(mla_w_uv, (0, 2, 1, 3)).reshape(depth, hp, 2, MLA_KV_RANK, MLA_V)
    zk = jnp.zeros_like(uk[:, :, 0])
    zv = jnp.zeros_like(uv[:, :, 0])
    wuk_pairs = jnp.concatenate([jnp.concatenate([uk[:, :, 0], zk], axis=-1),
                                 jnp.concatenate([zk, uk[:, :, 1]], axis=-1)], axis=-2).astype(BF16)
    wuv_pairs = jnp.concatenate([jnp.concatenate([uv[:, :, 0], zv], axis=-1),
                                 jnp.concatenate([zv, uv[:, :, 1]], axis=-1)], axis=-2).astype(BF16)

    row = lambda v: v[:, None, :]
    return dict(
        w_ssd=w_ssd, w_mla=w_mla, w_gdn=w_gdn,
        ssd_cw=ssd_conv_w, ssd_cb=row(ssd_conv_b),
        ssd_dtb=_pad_last(row(ssd_dt_bias)), ssd_alog=_pad_last(row(ssd_a_log)),
        ssd_dexp=row(jnp.repeat(ssd_d, SSD_HEAD_DIM, axis=1)), ssd_nw=row(ssd_norm_w),
        qnw=row(mla_q_norm_w), kvnw=row(mla_kv_norm_w), wuq=wuq,
        wuk_pairs=wuk_pairs, wuv_pairs=wuv_pairs,
        gdn_cw=gdn_conv_w, gdn_bias=_pad_last(row(gdn_dt_bias), before=GDN_HEADS),
        gdn_alog=_pad_last(row(gdn_a_log), before=GDN_HEADS), gdn_nw=row(gdn_norm_w),
        w_out=w_out.astype(BF16), ln_g=row(ln_g), ln_b=row(ln_b))


def _rope_tables(pos):
    inv = ROPE_THETA ** (-jnp.arange(ROPE_HALF, dtype=F32) / ROPE_HALF)
    ang = pos.astype(F32)[:, None] * inv[None, :]
    return jnp.tile(jnp.cos(ang), (1, MLA_HEADS)), jnp.tile(jnp.sin(ang), (1, MLA_HEADS))


def _head_expand_matrix():
    e = np.zeros((LANES, SSD_WIDTH), np.float32)
    for h in range(SSD_HEADS):
        e[h, h * SSD_HEAD_DIM:(h + 1) * SSD_HEAD_DIM] = 1.0
    return jnp.asarray(e, BF16)


def _rope_select_matrix():
    s = np.zeros((2 * LANES, MLA_ROPE), np.float32)
    for j in range(2 * LANES):
        s[j, (j // LANES) * ROPE_HALF + (j % ROPE_HALF)] = 1.0
    return jnp.asarray(s, BF16)


def _tail8(conv_state):
    return jnp.pad(conv_state, ((0, 0), (0, 0), (SUBLANES - (CONV_WIDTH - 1), 0), (0, 0)))


def _trunk(x, pos, tv, ssd_conv, ssd_state, gdn_conv, gdn_state, emb_g, emb_b, w, cfg, paged=None):
    b, t, _ = x.shape
    m = b * t
    tm = cfg["tm"]
    cos_t, sin_t = _rope_tables(pos)
    if cos_t.shape[0] < tm:
        reps = tm // cos_t.shape[0]
        cos_t, sin_t = jnp.tile(cos_t, (reps, 1)), jnp.tile(sin_t, (reps, 1))
    emat = _head_expand_matrix()
    sel = _rope_select_matrix()
    ssd_c8_in, gdn_c8_in = _tail8(ssd_conv), _tail8(gdn_conv)
    gw = SSD_WIDTH // SSD_GROUPS
    ssd_h_in = ssd_state.reshape(DEPTH, b, SSD_GROUPS, gw, SSD_STATE)
    h = _layer_norm(x.reshape(m, D_MODEL), emb_g, emb_b, tm)
    new_states = []
    for l in range(DEPTH):
        tmp = cfg["tm_proj"]
        p_ssd = _project(h, w["w_ssd"], l, tmp, "proj_ssd").reshape(b, t, P_SSD)
        p_mla = _project(h, w["w_mla"], l, tmp, "proj_mla")
        p_gdn = _project(h, w["w_gdn"], l, tmp, "proj_gdn").reshape(b, t, P_GDN)

        y_ssd, ssd_c8, ssd_h = _ssd_scan(
            p_ssd, ssd_c8_in, ssd_h_in, l,
            w["ssd_cw"], w["ssd_cb"], w["ssd_dtb"], w["ssd_alog"], w["ssd_dexp"], w["ssd_nw"], emat,
            L=cfg["ssd_chunk"], tv=min(tv, cfg["ssd_chunk"]))
        y_gdn, gdn_c8, gdn_s = _gdn_scan(
            p_gdn, gdn_c8_in, gdn_state, l, w["gdn_cw"], w["gdn_bias"], w["gdn_alog"], w["gdn_nw"],
            L=cfg["gdn_chunk"], tv=min(tv, cfg["gdn_chunk"]))

        q, kcat, kt, ckv, kr_t, gate = _mla_prep(p_mla, cos_t, sin_t, l, w["qnw"], w["kvnw"], w["wuq"],
                                                 w["wuk_pairs"], b, t, tm, tq_t=cfg["tq"] if paged is None else None)
        kcat = kcat.reshape(b, t, QK_WIDTH)
        if paged is None:
            o_lat = _flash_attention(q, kcat, kt, tq=cfg["tq"], tk=cfg["tk"])
        else:
            cache_lat, cache_rope_t, page_table = paged
            o_lat = _paged_attention(q, kcat, cache_lat, cache_rope_t, page_table, sel, tpad=t, tv=tv, layer=l)

        h = _out_proj(h, y_ssd.reshape(m, SSD_WIDTH), o_lat, gate, y_gdn.reshape(m, GDN_WIDTH), l,
                      w["wuv_pairs"], w["w_out"], w["ln_g"], w["ln_b"], tm)

        kr = jnp.concatenate([kr_t[:, 0:ROPE_HALF], kr_t[:, LANES:LANES + ROPE_HALF]], axis=1)
        sl = slice(SUBLANES - (CONV_WIDTH - 1), SUBLANES)
        new_states.append((ckv.reshape(b, t, MLA_KV_RANK), kr.reshape(b, t, MLA_ROPE), ssd_c8[:, sl],
                           ssd_h.reshape(b, SSD_HEADS, SSD_HEAD_DIM, SSD_STATE), gdn_c8[:, sl], gdn_s))
    return h.reshape(b, t, D_MODEL), tuple(jnp.stack(s) for s in zip(*new_states))


def kernel(x_prompt, x_sample, cache_kv_latent, cache_k_rope, state_ssd_conv, state_ssd, state_gdn_conv, state_gdn, page_table, emb_ln_g, emb_ln_b, w_in, ssd_conv_w, ssd_conv_b, ssd_dt_bias, ssd_a_log, ssd_d, ssd_norm_w, mla_q_norm_w, mla_w_uq, mla_kv_norm_w, mla_w_uk, mla_w_uv, gdn_conv_w, gdn_dt_bias, gdn_a_log, gdn_norm_w, w_out, ln_g, ln_b):
    weights = _prep_weights(w_in, ssd_conv_w, ssd_conv_b, ssd_dt_bias, ssd_a_log, ssd_d, ssd_norm_w,
                            mla_q_norm_w, mla_w_uq, mla_kv_norm_w, mla_w_uk, mla_w_uv,
                            gdn_conv_w, gdn_dt_bias, gdn_a_log, gdn_norm_w, w_out, ln_g, ln_b)

    bp, tp, _ = x_prompt.shape
    zeros = lambda *s: jnp.zeros(s, F32)
    ssd_chunk = SSD_CHUNK if tp % SSD_CHUNK == 0 else tp
    gdn_chunk = GDN_CHUNK if tp % GDN_CHUNK == 0 else tp
    cfg_p = dict(tm=min(512, bp * tp), tm_proj=min(1024, bp * tp), ssd_chunk=ssd_chunk, gdn_chunk=gdn_chunk,
                 tq=min(256, tp), tk=min(512, tp))
    y_prompt, st_p = _trunk(
        x_prompt, jnp.arange(tp), tp,
        zeros(DEPTH, bp, CONV_WIDTH - 1, SSD_CONV_DIM), zeros(DEPTH, bp, SSD_HEADS, SSD_HEAD_DIM, SSD_STATE),
        zeros(DEPTH, bp, CONV_WIDTH - 1, GDN_CONV_DIM), zeros(DEPTH, bp, GDN_HEADS, GDN_HEAD_DIM, GDN_HEAD_DIM),
        emb_ln_g, emb_ln_b, weights, cfg_p)

    bs, ts, _ = x_sample.shape
    tpad = -(-ts // BF16_ROWS) * BF16_ROWS
    past_len = page_table.shape[1] * PAGE_SIZE
    xs = jnp.pad(x_sample, ((0, 0), (0, tpad - ts), (0, 0)))
    cfg_s = dict(tm=min(512, bs * tpad), tm_proj=min(512, bs * tpad), ssd_chunk=tpad, gdn_chunk=tpad,
                 tq=tpad, tk=tpad)
    y_s, st_s = _trunk(
        xs, past_len + jnp.arange(tpad), ts, state_ssd_conv, state_ssd, state_gdn_conv, state_gdn,
        emb_ln_g, emb_ln_b, weights, cfg_s,
        paged=(cache_kv_latent, jnp.swapaxes(cache_k_rope, 2, 3), page_table))
    y_sample = y_s[:, :ts]
    s_lat, s_rope, s_ssd_conv, s_ssd, s_gdn_conv, s_gdn = st_s
    return (y_prompt, y_sample) + tuple(st_p) + (s_lat[:, :, :ts], s_rope[:, :, :ts], s_ssd_conv, s_ssd, s_gdn_conv, s_gdn)
```

```python
import functools
import math

import jax
import jax.numpy as jnp
import numpy as np
from jax import lax
from jax.experimental import pallas as pl
from jax.experimental.pallas import tpu as pltpu

F32 = jnp.float32
BF16 = jnp.bfloat16

D_MODEL = 1024
CONV_WIDTH = 4
SSD_HEADS = 16
SSD_HEAD_DIM = 64
SSD_WIDTH = SSD_HEADS * SSD_HEAD_DIM
SSD_GROUPS = 2
SSD_STATE = 128
SSD_CONV_DIM = SSD_WIDTH + 2 * SSD_GROUPS * SSD_STATE
SSD_CHUNK = 128
MLA_HEADS = 8
MLA_NOPE = 64
MLA_ROPE = 32
MLA_V = 64
MLA_WIDTH = MLA_HEADS * MLA_V
MLA_Q_RANK = 384
MLA_KV_RANK = 256
MLA_SCALE = (MLA_NOPE + MLA_ROPE) ** -0.5
ROPE_THETA = 10000.0
GDN_HEADS = 4
GDN_HEAD_DIM = 128
GDN_WIDTH = GDN_HEADS * GDN_HEAD_DIM
GDN_CONV_DIM = 3 * GDN_WIDTH
GDN_CHUNK = 64
MIX_WIDTH = SSD_WIDTH + MLA_WIDTH + GDN_WIDTH
IN_SIZES = (SSD_WIDTH, SSD_CONV_DIM, SSD_HEADS, MLA_Q_RANK, MLA_KV_RANK, MLA_ROPE, MLA_WIDTH,
            GDN_CONV_DIM, GDN_WIDTH, GDN_HEADS, GDN_HEADS)
IN_OFFSETS = tuple(int(o) for o in np.cumsum(IN_SIZES)[:-1])
DEPTH = 2
DEEPNORM_ALPHA = (2 * DEPTH) ** 0.25
LN_EPS = 1e-5
RMS_EPS = 1e-6
L2_EPS = 1e-6
PAGE_SIZE = 128

LANES = 128
SUBLANES = 8
BF16_ROWS = 16
ROPE_HALF = MLA_ROPE // 2
P_SSD = SSD_WIDTH + SSD_CONV_DIM + LANES
P_MLA = MLA_Q_RANK + MLA_KV_RANK + 2 * LANES + MLA_WIDTH
P_GDN = GDN_CONV_DIM + GDN_WIDTH + LANES
QK_WIDTH = MLA_KV_RANK + 2 * LANES
Q_SCALE = MLA_SCALE * math.log2(math.e)
VMEM_LIMIT = 56 * 1024 * 1024
PAGES_PER_GROUP = 32
GDN_BATCH_PER_STEP = 4

def _cparams(sem):
    return pltpu.CompilerParams(dimension_semantics=sem, vmem_limit_bytes=VMEM_LIMIT)


def _bdot(a, b):
    return jnp.dot(a.astype(BF16), b.astype(BF16), preferred_element_type=F32)


def _bdot_nt(a, b):
    return lax.dot_general(a.astype(BF16), b.astype(BF16), (((1,), (1,)), ((), ())),
                           preferred_element_type=F32)


def _fdot(a, b):
    return jnp.dot(a, b, precision=lax.Precision.HIGHEST, preferred_element_type=F32)


def _silu(x):
    return x * (1.0 / (1.0 + jnp.exp(-x)))


def _softplus(x):
    return jnp.maximum(x, 0.0) + jnp.log1p(jnp.exp(-jnp.abs(x)))


def _const_spec(shape):
    nd = len(shape)
    return pl.BlockSpec(shape, lambda *_: (0,) * nd)


def _layer_spec(stacked, l):
    rest = stacked.shape[1:]
    return pl.BlockSpec((None,) + rest, lambda *_: (l,) + (0,) * len(rest))


def _ln_kernel(x_ref, g_ref, b_ref, o_ref):
    x = x_ref[...]
    mu = jnp.mean(x, -1, keepdims=True)
    xc = x - mu
    var = jnp.mean(xc * xc, -1, keepdims=True)
    o_ref[...] = xc * lax.rsqrt(var + LN_EPS) * g_ref[...] + b_ref[...]


def _layer_norm(x, g, b, tm):
    m, d = x.shape
    return pl.pallas_call(
        _ln_kernel, grid=(m // tm,),
        in_specs=[pl.BlockSpec((tm, d), lambda i: (i, 0)), _const_spec((1, d)), _const_spec((1, d))],
        out_specs=pl.BlockSpec((tm, d), lambda i: (i, 0)),
        out_shape=jax.ShapeDtypeStruct((m, d), F32),
        compiler_params=_cparams(("parallel",)), name="emb_ln")(x, g.reshape(1, d), b.reshape(1, d))


def _proj_kernel(x_ref, w_ref, ssd_ref, dt_ref, mla_ref, gdn_ref, ba_ref):
    x = x_ref[...].astype(BF16)
    p = jnp.dot(x, w_ref[:, 0:P_SSD], preferred_element_type=F32)
    ssd_ref[...] = p[:, 0:P_SSD - LANES].astype(BF16)
    dt_ref[...] = p[:, P_SSD - LANES:P_SSD]
    mla_ref[...] = jnp.dot(x, w_ref[:, P_SSD:P_SSD + P_MLA], preferred_element_type=F32).astype(BF16)
    p = jnp.dot(x, w_ref[:, P_SSD + P_MLA:P_SSD + P_MLA + P_GDN], preferred_element_type=F32)
    gdn_ref[...] = p[:, 0:P_GDN - LANES].astype(BF16)
    ba_ref[...] = p[:, P_GDN - LANES:P_GDN]


def _project(x, w, l, tm):
    m, k = x.shape
    row = lambda i: (i, 0)
    widths = (P_SSD - LANES, LANES, P_MLA, P_GDN - LANES, LANES)
    dtypes = (BF16, F32, BF16, BF16, F32)
    return pl.pallas_call(
        _proj_kernel, grid=(m // tm,),
        in_specs=[pl.BlockSpec((tm, k), row), _layer_spec(w, l)],
        out_specs=[pl.BlockSpec((tm, n), row) for n in widths],
        out_shape=[jax.ShapeDtypeStruct((m, n), d) for n, d in zip(widths, dtypes)],
        compiler_params=_cparams(("parallel",)), name="in_proj")(x, w)


def _conv_chunk(xbuf, x_new, cw_ref, L):
    xbuf[SUBLANES:SUBLANES + L, :] = x_new
    y = cw_ref[CONV_WIDTH - 1:CONV_WIDTH, :] * x_new
    for k in range(CONV_WIDTH - 1):
        off = SUBLANES - (CONV_WIDTH - 1) + k
        y = y + cw_ref[k:k + 1, :] * xbuf[off:off + L, :]
    return y


def _ssd_kernel(p_ref, dt_ref, cprev_ref, h0_ref, cw_ref, cb_ref, dtb_ref, alog_ref, dexp_ref, nw_ref, e_ref,
                y_ref, cnew_ref, hout_ref, xbuf, ht, *, L, tv):
    c = pl.program_id(1)
    nc = pl.num_programs(1)

    @pl.when(c == 0)
    def _():
        xbuf[0:SUBLANES, :] = cprev_ref[0]
        for g in range(SSD_GROUPS):
            ht[g] = jnp.transpose(h0_ref[0, g])

    z = p_ref[0, :, 0:SSD_WIDTH].astype(F32)
    xbc_raw = p_ref[0, :, SSD_WIDTH:SSD_WIDTH + SSD_CONV_DIM].astype(F32)
    dt_raw = dt_ref[0]

    xbc = _silu(_conv_chunk(xbuf, xbc_raw, cw_ref, L) + cb_ref[...])
    cnew_ref[0] = xbuf[tv:tv + SUBLANES, :]
    xbuf[0:SUBLANES, :] = xbuf[L:L + SUBLANES, :]

    xs = xbc[:, 0:SSD_WIDTH]
    gs = SSD_GROUPS * SSD_STATE
    bm = xbc[:, SSD_WIDTH:SSD_WIDTH + gs]
    cm = xbc[:, SSD_WIDTH + gs:SSD_WIDTH + 2 * gs]

    dt = _softplus(dt_raw + dtb_ref[...])
    if tv < L:
        rows = lax.broadcasted_iota(jnp.int32, (L, LANES), 0)
        dt = jnp.where(rows < tv, dt, 0.0)
    a = -jnp.exp(alog_ref[...])
    da = dt * a
    r_i = lax.broadcasted_iota(jnp.int32, (L, L), 0)
    c_i = lax.broadcasted_iota(jnp.int32, (L, L), 1)
    causal = c_i <= r_i
    tri = jnp.where(causal, 1.0, 0.0).astype(F32)
    acum = _fdot(tri, da)
    last = acum[L - 1:L, :]
    ea = jnp.exp(acum)
    wdec = jnp.exp(last - acum)

    def pad_rows(v):
        if L == LANES:
            return v
        return jnp.concatenate([v, jnp.zeros((LANES - L, v.shape[1]), v.dtype)], axis=0)

    acum_t = jnp.transpose(pad_rows(acum))

    def hilo(v):
        hi = v.astype(BF16)
        lo = (v - hi.astype(F32)).astype(BF16)
        return hi, lo

    parts = []
    for v in (dt, ea, wdec):
        parts.extend(hilo(v))
    stacked = jnp.concatenate(parts, axis=0)
    expanded = jnp.dot(stacked, e_ref[...], preferred_element_type=F32)
    dt_e = expanded[0:L] + expanded[L:2 * L]
    ea_e = expanded[2 * L:3 * L] + expanded[3 * L:4 * L]
    wd_e = expanded[4 * L:5 * L] + expanded[5 * L:6 * L]

    xdt = xs * dt_e
    xdt_b = xdt.astype(BF16)
    lane = lax.broadcasted_iota(jnp.int32, (L, LANES), 1)
    hpg = SSD_HEADS // SSD_GROUPS

    y_parts = []
    for g in range(SSD_GROUPS):
        bg = bm[:, g * SSD_STATE:(g + 1) * SSD_STATE]
        cg = cm[:, g * SSD_STATE:(g + 1) * SSD_STATE]
        cb = _bdot_nt(cg, bg)
        for j in range(hpg // 2):
            pair = g * (hpg // 2) + j
            xp = xdt_b[:, pair * LANES:(pair + 1) * LANES]
            ys = []
            for h in (2 * pair, 2 * pair + 1):
                seg = acum[:, h:h + 1] - acum_t[h:h + 1, 0:L]
                dec = jnp.exp(jnp.where(causal, seg, -jnp.inf))
                ys.append(_bdot(cb * dec, xp))
            y_parts.append(jnp.where(lane < SSD_HEAD_DIM, ys[0], ys[1]))
    y_in = jnp.concatenate(y_parts, axis=1)

    gw = hpg * SSD_HEAD_DIM
    y_st_parts = []
    for g in range(SSD_GROUPS):
        cg = cm[:, g * SSD_STATE:(g + 1) * SSD_STATE]
        y_st_parts.append(_bdot(cg, ht[g]))
    y_st = jnp.concatenate(y_st_parts, axis=1) * ea_e

    xw = (xdt * wd_e)
    for g in range(SSD_GROUPS):
        bg_t = jnp.transpose(pad_rows(bm[:, g * SSD_STATE:(g + 1) * SSD_STATE]))
        xw_g = pad_rows(xw[:, g * gw:(g + 1) * gw])
        ht[g] = ht[g] * ea_e[L - 1:L, g * gw:(g + 1) * gw] + _bdot(bg_t, xw_g)

    y = (y_in + y_st + dexp_ref[...] * xs) * _silu(z)
    outs = []
    for g in range(SSD_GROUPS):
        yg = y[:, g * gw:(g + 1) * gw]
        ms = jnp.mean(yg * yg, -1, keepdims=True)
        outs.append(yg * lax.rsqrt(ms + RMS_EPS) * nw_ref[:, g * gw:(g + 1) * gw])
    y_ref[0] = jnp.concatenate(outs, axis=1).astype(y_ref.dtype)

    @pl.when(c == nc - 1)
    def _():
        for g in range(SSD_GROUPS):
            hout_ref[0, g] = jnp.transpose(ht[g])


def _ssd_scan(p_ssd, p_dt, conv_prev8, h0, l, cw, cb, dtb, alog, dexp, nw, emat, *, L, tv):
    b, t, _ = p_ssd.shape
    nc = t // L
    gw = SSD_WIDTH // SSD_GROUPS
    kern = functools.partial(_ssd_kernel, L=L, tv=tv)
    return pl.pallas_call(
        kern, grid=(b, nc),
        in_specs=[pl.BlockSpec((1, L, P_SSD - LANES), lambda i, c: (i, c, 0)),
                  pl.BlockSpec((1, L, LANES), lambda i, c: (i, c, 0)),
                  pl.BlockSpec((None, 1, SUBLANES, SSD_CONV_DIM), lambda i, c: (l, i, 0, 0)),
                  pl.BlockSpec((None, 1, SSD_GROUPS, gw, SSD_STATE), lambda i, c: (l, i, 0, 0, 0)),
                  _layer_spec(cw, l), _layer_spec(cb, l), _layer_spec(dtb, l), _layer_spec(alog, l),
                  _layer_spec(dexp, l), _layer_spec(nw, l), _const_spec((LANES, SSD_WIDTH))],
        out_specs=[pl.BlockSpec((1, L, SSD_WIDTH), lambda i, c: (i, c, 0)),
                   pl.BlockSpec((1, SUBLANES, SSD_CONV_DIM), lambda i, c: (i, 0, 0)),
                   pl.BlockSpec((1, SSD_GROUPS, gw, SSD_STATE), lambda i, c: (i, 0, 0, 0))],
        out_shape=[jax.ShapeDtypeStruct((b, t, SSD_WIDTH), BF16),
                   jax.ShapeDtypeStruct((b, SUBLANES, SSD_CONV_DIM), F32),
                   jax.ShapeDtypeStruct((b, SSD_GROUPS, gw, SSD_STATE), F32)],
        scratch_shapes=[pltpu.VMEM((L + SUBLANES, SSD_CONV_DIM), F32),
                        pltpu.VMEM((SSD_GROUPS, SSD_STATE, gw), F32)],
        compiler_params=_cparams(("parallel", "arbitrary")), name="ssd_scan",
    )(p_ssd, p_dt, conv_prev8, h0, cw, cb, dtb, alog, dexp, nw, emat)


def _gdn_chunk(p_ref, ba_ref, cw_ref, bias_ref, alog_ref, nw_ref, y_ref, cnew_ref, xbuf, st, *, L, tv):
    H = GDN_HEADS
    D = GDN_HEAD_DIM
    R = H * L
    S = max(R, LANES)

    qkv_raw = p_ref[:, 0:GDN_CONV_DIM].astype(F32)
    z = p_ref[:, GDN_CONV_DIM:GDN_CONV_DIM + GDN_WIDTH].astype(F32)
    ba = ba_ref[...]

    qkv = _silu(_conv_chunk(xbuf, qkv_raw, cw_ref, L))
    cnew_ref[...] = xbuf[tv:tv + SUBLANES, :]
    xbuf[0:SUBLANES, :] = xbuf[L:L + SUBLANES, :]
    yield

    beta_f = 1.0 / (1.0 + jnp.exp(-ba))
    g_f = -jnp.exp(alog_ref[...]) * _softplus(ba + bias_ref[...])
    if tv < L:
        rows = lax.broadcasted_iota(jnp.int32, (L, LANES), 0)
        beta_f = jnp.where(rows < tv, beta_f, 0.0)
        g_f = jnp.where(rows < tv, g_f, 0.0)
    r_i = lax.broadcasted_iota(jnp.int32, (L, L), 0)
    c_i = lax.broadcasted_iota(jnp.int32, (L, L), 1)
    tri = jnp.where(c_i <= r_i, 1.0, 0.0).astype(F32)
    gcum_f = _fdot(tri, g_f)
    glast_f = jnp.broadcast_to(gcum_f[L - 1:L, :], (L, LANES))

    def pad_s(v):
        if R == S:
            return v
        return jnp.concatenate([v, jnp.zeros((S - R, v.shape[1]), v.dtype)], axis=0)

    def stack(v):
        return pad_s(jnp.concatenate([v[:, h * D:(h + 1) * D] for h in range(H)], axis=0))

    def col(v, off):
        return pad_s(jnp.concatenate([v[:, off + h:off + h + 1] for h in range(H)], axis=0))

    q_s = stack(qkv[:, 0:GDN_WIDTH])
    k_s = stack(qkv[:, GDN_WIDTH:2 * GDN_WIDTH])
    v_s = stack(qkv[:, 2 * GDN_WIDTH:3 * GDN_WIDTH])
    z_s = stack(z)
    q_s = q_s * lax.rsqrt(jnp.sum(q_s * q_s, -1, keepdims=True) + L2_EPS) * (D ** -0.5)
    k_s = k_s * lax.rsqrt(jnp.sum(k_s * k_s, -1, keepdims=True) + L2_EPS)
    beta = col(beta_f, 0)
    gcum = col(gcum_f, H)
    glast = col(glast_f, H)
    yield

    cmat = jnp.broadcast_to(gcum, (S, S))
    diff = cmat - jnp.transpose(cmat)
    rs = lax.broadcasted_iota(jnp.int32, (S, S), 0)
    cs = lax.broadcasted_iota(jnp.int32, (S, S), 1)
    if L & (L - 1) == 0:
        sh = L.bit_length() - 1
        same = (rs >> sh) == (cs >> sh)
    else:
        same = (rs // L) == (cs // L)
    incl = same & (cs <= rs)
    strict = same & (cs < rs)
    dec = jnp.exp(jnp.where(incl, diff, -jnp.inf))

    kb = k_s * beta
    n_mat = -jnp.where(strict, _bdot_nt(kb, k_s) * dec, 0.0)
    eye = jnp.where(rs == cs, 1.0, 0.0).astype(F32)
    t_mat = eye + n_mat
    npow = n_mat
    yield
    span = 2
    while span < L:
        npow = _bdot(npow, npow)
        yield
        t_mat = t_mat + _bdot(t_mat, npow)
        yield
        span *= 2

    eg = jnp.exp(gcum)
    t_b = t_mat.astype(BF16)
    u = _bdot(t_b, v_s * beta)
    w = _bdot(t_b, kb * eg)
    qg = q_s * eg
    attn = jnp.where(incl, _bdot_nt(q_s, k_s) * dec, 0.0)
    yield
    ws = []
    qs_ = []
    for h in range(H):
        sh_b = st[h].astype(BF16)
        ws.append(_bdot(w[h * L:(h + 1) * L], sh_b))
        qs_.append(_bdot(qg[h * L:(h + 1) * L], sh_b))
    v_new = u - pad_s(jnp.concatenate(ws, axis=0))
    yield
    o = pad_s(jnp.concatenate(qs_, axis=0)) + _bdot(attn, v_new)

    kd = k_s * jnp.exp(glast - gcum)
    kd_t = jnp.transpose(kd).astype(BF16)
    row_head = lax.broadcasted_iota(jnp.int32, (S, D), 0)
    eg_last = jnp.exp(glast)
    for h in range(H):
        vm = jnp.where((row_head >= h * L) & (row_head < (h + 1) * L), v_new, 0.0)
        st[h] = st[h] * eg_last[h * L:h * L + 1, :] + _bdot(kd_t, vm)
    yield

    ms = jnp.mean(o * o, -1, keepdims=True)
    o = o * lax.rsqrt(ms + RMS_EPS) * nw_ref[...] * _silu(z_s)
    y_ref[...] = jnp.concatenate([o[h * L:(h + 1) * L] for h in range(H)], axis=1).astype(y_ref.dtype)


def _gdn_kernel(p_ref, ba_ref, cprev_ref, s0_ref, cw_ref, bias_ref, alog_ref, nw_ref,
                y_ref, cnew_ref, sout_ref, xbuf, st, *, L, tv, nbat):
    c = pl.program_id(1)
    nc = pl.num_programs(1)

    @pl.when(c == 0)
    def _():
        xbuf[:, 0:SUBLANES, :] = cprev_ref[...]
        st[...] = s0_ref[...]

    live = [_gdn_chunk(p_ref.at[i], ba_ref.at[i], cw_ref, bias_ref, alog_ref, nw_ref, y_ref.at[i],
                       cnew_ref.at[i], xbuf.at[i], st.at[i], L=L, tv=tv) for i in range(nbat)]
    while live:
        live = [g for g in live if next(g, StopIteration) is not StopIteration]

    @pl.when(c == nc - 1)
    def _():
        sout_ref[...] = st[...]


def _gdn_scan(p_gdn, p_ba, conv_prev8, s0, l, cw, bias, alog, nw, *, L, tv):
    b, t, _ = p_gdn.shape
    nc = t // L
    nbat = math.gcd(b, GDN_BATCH_PER_STEP)
    kern = functools.partial(_gdn_kernel, L=L, tv=tv, nbat=nbat)
    sshape = (nbat, GDN_HEADS, GDN_HEAD_DIM, GDN_HEAD_DIM)
    return pl.pallas_call(
        kern, grid=(b // nbat, nc),
        in_specs=[pl.BlockSpec((nbat, L, P_GDN - LANES), lambda i, c: (i, c, 0)),
                  pl.BlockSpec((nbat, L, LANES), lambda i, c: (i, c, 0)),
                  pl.BlockSpec((None, nbat, SUBLANES, GDN_CONV_DIM), lambda i, c: (l, i, 0, 0)),
                  pl.BlockSpec((None,) + sshape, lambda i, c: (l, i, 0, 0, 0)),
                  _layer_spec(cw, l), _layer_spec(bias, l), _layer_spec(alog, l), _layer_spec(nw, l)],
        out_specs=[pl.BlockSpec((nbat, L, GDN_WIDTH), lambda i, c: (i, c, 0)),
                   pl.BlockSpec((nbat, SUBLANES, GDN_CONV_DIM), lambda i, c: (i, 0, 0)),
                   pl.BlockSpec(sshape, lambda i, c: (i, 0, 0, 0))],
        out_shape=[jax.ShapeDtypeStruct((b, t, GDN_WIDTH), BF16),
                   jax.ShapeDtypeStruct((b, SUBLANES, GDN_CONV_DIM), F32),
                   jax.ShapeDtypeStruct((b,) + sshape[1:], F32)],
        scratch_shapes=[pltpu.VMEM((nbat, L + SUBLANES, GDN_CONV_DIM), F32),
                        pltpu.VMEM(sshape, F32)],
        compiler_params=_cparams(("parallel", "arbitrary")), name="gdn_scan",
    )(p_gdn, p_ba, conv_prev8, s0, cw, bias, alog, nw)


def _mla_prep_kernel(p_ref, cos_ref, sin_ref, qnw_ref, kvnw_ref, wuq_ref, wuk_ref,
                     q_ref, kcat_ref, kt_ref, ckv_ref, kr_ref, gate_ref, *, nb, tt, tq_t):
    cq = p_ref[:, 0:MLA_Q_RANK].astype(F32)
    ckv = p_ref[:, MLA_Q_RANK:MLA_Q_RANK + MLA_KV_RANK].astype(F32)
    kr = p_ref[:, MLA_Q_RANK + MLA_KV_RANK:MLA_Q_RANK + MLA_KV_RANK + 2 * LANES].astype(F32)
    gate = p_ref[:, MLA_Q_RANK + MLA_KV_RANK + 2 * LANES:P_MLA].astype(F32)
    cos = cos_ref[...]
    sin = sin_ref[...]

    cqn = cq * lax.rsqrt(jnp.mean(cq * cq, -1, keepdims=True) + RMS_EPS) * qnw_ref[...]
    q = _bdot(cqn, wuq_ref[...])
    nope_w = MLA_HEADS * MLA_NOPE
    x1 = q[:, nope_w:nope_w + LANES]
    x2 = q[:, nope_w + LANES:nope_w + 2 * LANES]
    r_all = jnp.concatenate([x1 * cos - x2 * sin, x2 * cos + x1 * sin], axis=1) * Q_SCALE

    ckvn = ckv * lax.rsqrt(jnp.mean(ckv * ckv, -1, keepdims=True) + RMS_EPS) * kvnw_ref[...]
    k1 = kr[:, 0:LANES]
    k2 = kr[:, LANES:2 * LANES]
    kr_rot = jnp.concatenate([k1 * cos - k2 * sin, k2 * cos + k1 * sin], axis=1)
    ckv_ref[...] = ckvn
    kr_ref[...] = kr_rot
    kcat_ref[...] = jnp.concatenate([ckvn, kr_rot], axis=1).astype(BF16)
    kt_ref[...] = jnp.transpose(ckvn).astype(BF16)
    gate_ref[...] = _silu(gate).astype(gate_ref.dtype)

    tm = q.shape[0]
    if tq_t is None:
        lane = lax.broadcasted_iota(jnp.int32, (1, 2 * LANES), 1)
        head_of_lane = (lane & (LANES - 1)) >> 4
    else:
        r_all_t = jnp.transpose(r_all)
        row_i = lax.broadcasted_iota(jnp.int32, (2 * LANES, 1), 0)
        head_of_row = (row_i & (LANES - 1)) >> 4
    for pair in range(MLA_HEADS // 2):
        qn = q[:, pair * LANES:(pair + 1) * LANES] * Q_SCALE
        qlat = _bdot(qn, wuk_ref[pair])
        for i in range(2):
            h = 2 * pair + i
            ql = qlat[:, i * MLA_KV_RANK:(i + 1) * MLA_KV_RANK]
            if tq_t is None:
                qr = jnp.where(head_of_lane == h, r_all, 0.0).astype(BF16)
                q_ref[:, h, :, 0:MLA_KV_RANK] = ql.astype(BF16).reshape(nb, tt, MLA_KV_RANK)
                q_ref[:, h, :, MLA_KV_RANK:QK_WIDTH] = qr.reshape(nb, tt, 2 * LANES)
            else:
                ql_t = jnp.transpose(ql).astype(BF16)
                qr_t = jnp.where(head_of_row == h, r_all_t, 0.0).astype(BF16)
                for qb in range(tm // tq_t):
                    cols = slice(h * tq_t, (h + 1) * tq_t)
                    toks = slice(qb * tq_t, (qb + 1) * tq_t)
                    q_ref[0, qb, 0:MLA_KV_RANK, cols] = ql_t[:, toks]
                    q_ref[0, qb, MLA_KV_RANK:QK_WIDTH, cols] = qr_t[:, toks]


def _head_major_block(t, tm):
    if tm <= t:
        npb = t // tm
        return 1, tm, (lambda i: (i // npb, 0, i % npb, 0))
    return tm // t, t, (lambda i: (i, 0, 0, 0))


def _mla_prep(p_mla, cos_t, sin_t, l, qnw, kvnw, wuq, wuk_pairs, b, t, tm, tq_t=None):
    m = p_mla.shape[0]
    nt = cos_t.shape[0] // tm
    row = lambda i: (i, 0)
    tab = lambda i: (i % nt, 0)
    nb, tt, hm_map = _head_major_block(t, tm)
    kern = functools.partial(_mla_prep_kernel, nb=nb, tt=tt, tq_t=tq_t)
    if tq_t is None:
        q_spec = pl.BlockSpec((nb, MLA_HEADS, tt, QK_WIDTH), hm_map)
        q_shape = jax.ShapeDtypeStruct((b, MLA_HEADS, t, QK_WIDTH), BF16)
    else:
        npb = t // tm
        q_spec = pl.BlockSpec((1, tm // tq_t, QK_WIDTH, MLA_HEADS * tq_t), lambda i: (i // npb, i % npb, 0, 0))
        q_shape = jax.ShapeDtypeStruct((b, t // tq_t, QK_WIDTH, MLA_HEADS * tq_t), BF16)
    return pl.pallas_call(
        kern, grid=(m // tm,),
        in_specs=[pl.BlockSpec((tm, P_MLA), row), pl.BlockSpec((tm, LANES), tab), pl.BlockSpec((tm, LANES), tab),
                  _layer_spec(qnw, l), _layer_spec(kvnw, l), _layer_spec(wuq, l), _layer_spec(wuk_pairs, l)],
        out_specs=[q_spec, pl.BlockSpec((tm, QK_WIDTH), row),
                   pl.BlockSpec((MLA_KV_RANK, tm), lambda i: (0, i)),
                   pl.BlockSpec((tm, MLA_KV_RANK), row), pl.BlockSpec((tm, 2 * LANES), row),
                   pl.BlockSpec((tm, MLA_WIDTH), row)],
        out_shape=[q_shape,
                   jax.ShapeDtypeStruct((m, QK_WIDTH), BF16), jax.ShapeDtypeStruct((MLA_KV_RANK, m), BF16),
                   jax.ShapeDtypeStruct((m, MLA_KV_RANK), F32), jax.ShapeDtypeStruct((m, 2 * LANES), F32),
                   jax.ShapeDtypeStruct((m, MLA_WIDTH), BF16)],
        compiler_params=_cparams(("parallel",)), name="mla_prep",
    )(p_mla, cos_t, sin_t, qnw, kvnw, wuq, wuk_pairs)


def _softmax_rows(s, m_scr, l_scr):
    n = s.shape[1]
    m_prev = m_scr[...]
    m_new = jnp.maximum(m_prev, jnp.max(s, -1, keepdims=True))
    alpha = jnp.exp2(m_prev - m_new)
    m_wide = jnp.concatenate([m_new] * (n // LANES), axis=1) if n >= LANES else m_new[:, 0:n]
    p = jnp.exp2(s - m_wide)
    l_scr[...] = alpha * l_scr[...] + jnp.sum(p, -1, keepdims=True)
    m_scr[...] = m_new
    return p, alpha


def _softmax_init(m_scr, l_scr, acc_scr):
    m_scr[...] = jnp.full(m_scr.shape, -jnp.inf, F32)
    l_scr[...] = jnp.zeros(l_scr.shape, F32)
    acc_scr[...] = jnp.zeros(acc_scr.shape, F32)


def _flash_kernel(qi_ref, ki_ref, q_ref, k_ref, kt_ref, o_ref, m_scr, l_scr, acc_scr, *, tq, tk):
    step_i = pl.program_id(1)
    qi = qi_ref[step_i]
    ki = ki_ref[step_i]
    rows = tq * MLA_HEADS

    @pl.when(ki == 0)
    def _():
        _softmax_init(m_scr, l_scr, acc_scr)

    def step(masked, nkeys):
        s_t = jnp.dot(k_ref[0, 0:nkeys, :], q_ref[0, 0], preferred_element_type=F32)
        if masked:
            k_pos = ki * tk + lax.broadcasted_iota(jnp.int32, (nkeys, rows), 0)
            q_pos = qi * tq + (lax.broadcasted_iota(jnp.int32, (nkeys, rows), 1) & (tq - 1))
            s_t = jnp.where(k_pos <= q_pos, s_t, -jnp.inf)
        m_prev = m_scr[...]
        m_new = jnp.maximum(m_prev, jnp.max(s_t, axis=0, keepdims=True))
        alpha = jnp.exp2(m_prev - m_new)
        p_t = jnp.exp2(s_t - m_new)
        l_scr[...] = alpha * l_scr[...] + jnp.sum(p_t, axis=0, keepdims=True)
        m_scr[...] = m_new
        acc_scr[...] = acc_scr[...] * alpha + jnp.dot(kt_ref[:, 0:nkeys], p_t.astype(BF16),
                                                      preferred_element_type=F32)

    crosses = (ki + 1) * tk > qi * tq + 1
    short = (qi + 1) * tq - ki * tk
    if tq < tk and tk % tq == 0:
        use_short = crosses & (short == tq)

        @pl.when(use_short)
        def _():
            step(True, tq)
    else:
        use_short = False

    @pl.when(crosses & jnp.logical_not(use_short))
    def _():
        step(True, tk)

    @pl.when(jnp.logical_not(crosses))
    def _():
        step(False, tk)

    @pl.when(ki == ((qi + 1) * tq - 1) // tk)
    def _():
        o_t = acc_scr[...] * (1.0 / l_scr[...])
        o_ref[0] = jnp.transpose(o_t).astype(o_ref.dtype).reshape(MLA_HEADS, tq, MLA_KV_RANK)


def _flash_attention(q_t, kcat, kt, *, tq, tk):
    b, nq, _, rows = q_t.shape
    t = nq * tq
    nkb = t // tk
    kern = functools.partial(_flash_kernel, tq=tq, tk=tk)
    pairs = [(qi, ki) for qi in range(nq) for ki in range(((qi + 1) * tq - 1) // tk + 1)]
    qi_arr = jnp.asarray([p[0] for p in pairs], jnp.int32)
    ki_arr = jnp.asarray([p[1] for p in pairs], jnp.int32)
    grid_spec = pltpu.PrefetchScalarGridSpec(
        num_scalar_prefetch=2, grid=(b, len(pairs)),
        in_specs=[pl.BlockSpec((1, 1, QK_WIDTH, rows), lambda i, s, qa, ka: (i, qa[s], 0, 0)),
                  pl.BlockSpec((1, tk, QK_WIDTH), lambda i, s, qa, ka: (i, ka[s], 0)),
                  pl.BlockSpec((MLA_KV_RANK, tk), lambda i, s, qa, ka: (0, i * nkb + ka[s]))],
        out_specs=pl.BlockSpec((1, MLA_HEADS, tq, MLA_KV_RANK), lambda i, s, qa, ka: (i, 0, qa[s], 0)),
        scratch_shapes=[pltpu.VMEM((1, rows), F32), pltpu.VMEM((1, rows), F32),
                        pltpu.VMEM((MLA_KV_RANK, rows), F32)])
    return pl.pallas_call(
        kern, grid_spec=grid_spec,
        out_shape=jax.ShapeDtypeStruct((b, MLA_HEADS, t, MLA_KV_RANK), BF16),
        compiler_params=_cparams(("parallel", "arbitrary")), name="mla_flash",
    )(qi_arr, ki_arr, q_t, kcat, kt)


def _paged_kernel(pt_ref, q_ref, knew_ref, sel_ref, rowsel_ref, rowselt_ref, lat_hbm, rope_hbm, o_ref,
                  latbuf, ropebuf, sems, m_scr, l_scr, acc_scr, *, tpad, tv, layer, ngroups):
    bi = pl.program_id(0)
    nb = pl.num_programs(0)
    npg = PAGES_PER_GROUP
    rows = tpad * MLA_HEADS
    rows_v = tv * MLA_HEADS
    reps = MLA_KV_RANK // LANES

    def copies(seq, grp, slot):
        out = []
        for i in range(npg):
            page = pt_ref[seq, grp * npg + i]
            dst = pl.ds(i * PAGE_SIZE, PAGE_SIZE)
            out.append(pltpu.make_async_copy(lat_hbm.at[layer, page], latbuf.at[slot, dst], sems.at[0, slot, i]))
            out.append(pltpu.make_async_copy(rope_hbm.at[layer, page], ropebuf.at[slot, :, dst], sems.at[1, slot, i]))
        return out

    def start(seq, grp, slot):
        for cp in copies(seq, grp, slot):
            cp.start()

    def wait(seq, grp, slot):
        for cp in copies(seq, grp, slot):
            cp.wait()

    @pl.when(bi == 0)
    def _():
        for g in range(ngroups):
            start(0, g, g)

    for a in range(2):
        _softmax_init(m_scr.at[a], l_scr.at[a], acc_scr.at[a])
    q_v = jnp.dot(rowsel_ref[...], q_ref[0].reshape(rows, QK_WIDTH), preferred_element_type=F32).astype(BF16)
    q_lat = q_v[:, 0:MLA_KV_RANK]
    q_rd = jnp.dot(q_v[:, MLA_KV_RANK:QK_WIDTH], sel_ref[...], preferred_element_type=F32).astype(BF16)

    nt = (((1,), (1,)), ((), ()))

    def accumulate(a, s, v_b):
        p, alpha = _softmax_rows(s, m_scr.at[a], l_scr.at[a])
        yield
        acc_scr[a] = (acc_scr[a] * jnp.concatenate([alpha] * reps, axis=1)
                      + jnp.dot(p.astype(BF16), v_b, preferred_element_type=F32))

    def consume(a, slot):
        lat_b = latbuf[slot].astype(BF16)
        rope_b = ropebuf[slot].astype(BF16)
        yield
        s = (lax.dot_general(q_lat, lat_b, nt, preferred_element_type=F32)
             + jnp.dot(q_rd, rope_b, preferred_element_type=F32))
        yield
        yield from accumulate(a, s, lat_b)

    def lockstep(gens):
        while gens:
            gens = [g for g in gens if next(g, StopIteration) is not StopIteration]

    k_b = knew_ref[0]
    s2 = lax.dot_general(q_v, k_b, nt, preferred_element_type=F32)
    q_pos = lax.rem(lax.broadcasted_iota(jnp.int32, (rows_v, tpad), 0), tv)
    k_pos = lax.broadcasted_iota(jnp.int32, (rows_v, tpad), 1)
    lockstep([accumulate(0, jnp.where(k_pos <= q_pos, s2, -jnp.inf), k_b[:, 0:MLA_KV_RANK])])

    for g0 in range(0, ngroups, 2):
        wait(bi, g0, g0)
        wait(bi, g0 + 1, g0 + 1)
        lockstep([consume(0, g0), consume(1, g0 + 1)])

        @pl.when(bi + 1 < nb)
        def _():
            start(bi + 1, g0, g0)
            start(bi + 1, g0 + 1, g0 + 1)

    m0, m1 = m_scr[0], m_scr[1]
    m = jnp.maximum(m0, m1)
    a0, a1 = jnp.exp2(m0 - m), jnp.exp2(m1 - m)
    inv_l = 1.0 / (a0 * l_scr[0] + a1 * l_scr[1])
    wide = lambda v: jnp.concatenate([v] * reps, axis=1)
    o_v = ((acc_scr[0] * wide(a0) + acc_scr[1] * wide(a1)) * wide(inv_l)).astype(BF16)
    o = jnp.dot(rowselt_ref[...], o_v, preferred_element_type=F32)
    o_ref[0] = o.astype(o_ref.dtype).reshape(MLA_HEADS, tpad, MLA_KV_RANK)


def _row_select_matrix(tpad, tv):
    s = np.zeros((MLA_HEADS * tv, MLA_HEADS * tpad), np.float32)
    for h in range(MLA_HEADS):
        for t in range(tv):
            s[h * tv + t, h * tpad + t] = 1.0
    return s


def _paged_attention(q, knew, cache_lat, cache_rope_t, page_table, sel, *, tpad, tv, layer):
    b = q.shape[0]
    n_pages = page_table.shape[1]
    npg = PAGES_PER_GROUP
    ngroups = n_pages // npg
    assert n_pages % (2 * npg) == 0, "page groups are consumed in slot pairs"
    rows = tpad * MLA_HEADS
    rows_v = tv * MLA_HEADS
    gk = npg * PAGE_SIZE
    rowsel = _row_select_matrix(tpad, tv)
    kern = functools.partial(_paged_kernel, tpad=tpad, tv=tv, layer=layer, ngroups=ngroups)
    stat = pltpu.VMEM((2, rows_v, LANES), F32)
    grid_spec = pltpu.PrefetchScalarGridSpec(
        num_scalar_prefetch=1, grid=(b,),
        in_specs=[pl.BlockSpec((1, MLA_HEADS, tpad, QK_WIDTH), lambda bi, pt: (bi, 0, 0, 0)),
                  pl.BlockSpec((1, tpad, QK_WIDTH), lambda bi, pt: (bi, 0, 0)),
                  pl.BlockSpec(sel.shape, lambda bi, pt: (0, 0)),
                  pl.BlockSpec((rows_v, rows), lambda bi, pt: (0, 0)),
                  pl.BlockSpec((rows, rows_v), lambda bi, pt: (0, 0)),
                  pl.BlockSpec(memory_space=pl.ANY), pl.BlockSpec(memory_space=pl.ANY)],
        out_specs=pl.BlockSpec((1, MLA_HEADS, tpad, MLA_KV_RANK), lambda bi, pt: (bi, 0, 0, 0)),
        scratch_shapes=[pltpu.VMEM((ngroups, gk, MLA_KV_RANK), F32), pltpu.VMEM((ngroups, MLA_ROPE, gk), F32),
                        pltpu.SemaphoreType.DMA((2, ngroups, npg)),
                        stat, stat, pltpu.VMEM((2, rows_v, MLA_KV_RANK), F32)])
    return pl.pallas_call(
        kern, grid_spec=grid_spec,
        out_shape=jax.ShapeDtypeStruct((b, MLA_HEADS, tpad, MLA_KV_RANK), BF16),
        compiler_params=_cparams(("arbitrary",)), name="mla_paged",
    )(page_table, q, knew, sel, jnp.asarray(rowsel, BF16), jnp.asarray(rowsel.T, BF16), cache_lat, cache_rope_t)


def _out_kernel(x_ref, yssd_ref, olat_ref, gate_ref, ygdn_ref, wuv_ref, wout_ref, g_ref, b_ref, o_ref, *, tm):
    y_mla = []
    for pair in range(MLA_HEADS // 2):
        o_pair = None
        for i in range(2):
            o_h = olat_ref[:, 2 * pair + i].reshape(tm, MLA_KV_RANK)
            part = jnp.dot(o_h, wuv_ref[pair, i * MLA_KV_RANK:(i + 1) * MLA_KV_RANK, :], preferred_element_type=F32)
            o_pair = part if o_pair is None else o_pair + part
        y_mla.append(o_pair * gate_ref[:, pair * LANES:(pair + 1) * LANES])
    y_mla = jnp.concatenate(y_mla, axis=1)
    mm = (_bdot(yssd_ref[...], wout_ref[0:SSD_WIDTH, :])
          + _bdot(y_mla, wout_ref[SSD_WIDTH:SSD_WIDTH + MLA_WIDTH, :])
          + _bdot(ygdn_ref[...], wout_ref[SSD_WIDTH + MLA_WIDTH:MIX_WIDTH, :]))
    v = DEEPNORM_ALPHA * x_ref[...] + mm
    mu = jnp.mean(v, -1, keepdims=True)
    vc = v - mu
    var = jnp.mean(vc * vc, -1, keepdims=True)
    o_ref[...] = vc * lax.rsqrt(var + LN_EPS) * g_ref[...] + b_ref[...]


def _out_proj(x, y_ssd, o_lat, gate, y_gdn, l, wuv_pairs, wout, g, b, tm):
    m = x.shape[0]
    t = o_lat.shape[2]
    row = lambda i: (i, 0)
    nb, tt, hm_map = _head_major_block(t, tm)
    return pl.pallas_call(
        functools.partial(_out_kernel, tm=tm), grid=(m // tm,),
        in_specs=[pl.BlockSpec((tm, D_MODEL), row), pl.BlockSpec((tm, SSD_WIDTH), row),
                  pl.BlockSpec((nb, MLA_HEADS, tt, MLA_KV_RANK), hm_map), pl.BlockSpec((tm, MLA_WIDTH), row),
                  pl.BlockSpec((tm, GDN_WIDTH), row),
                  _layer_spec(wuv_pairs, l), _layer_spec(wout, l), _layer_spec(g, l), _layer_spec(b, l)],
        out_specs=pl.BlockSpec((tm, D_MODEL), row),
        out_shape=jax.ShapeDtypeStruct((m, D_MODEL), F32),
        compiler_params=_cparams(("parallel",)), name="out_proj",
    )(x, y_ssd, o_lat, gate, y_gdn, wuv_pairs, wout, g, b)


def _pad_last(v, n=LANES, before=0):
    return jnp.pad(v, ((0, 0),) * (v.ndim - 1) + ((before, n - before - v.shape[-1]),))


def _prep_weights(w_in, ssd_conv_w, ssd_conv_b, ssd_dt_bias, ssd_a_log, ssd_d, ssd_norm_w,
                  mla_q_norm_w, mla_w_uq, mla_kv_norm_w, mla_w_uk, mla_w_uv,
                  gdn_conv_w, gdn_dt_bias, gdn_a_log, gdn_norm_w, w_out, ln_g, ln_b):
    depth = w_in.shape[0]
    (w_z, w_xbc, w_dt, w_cq, w_ckv, w_kr, w_gate, w_qkv, w_gz, w_gb, w_ga) = jnp.split(w_in, IN_OFFSETS, axis=2)
    w_ssd = [w_z, w_xbc, _pad_last(w_dt)]
    kr_tiled = jnp.concatenate([jnp.tile(w_kr[..., :ROPE_HALF], (1, 1, MLA_HEADS)),
                                jnp.tile(w_kr[..., ROPE_HALF:], (1, 1, MLA_HEADS))], axis=2)
    w_mla = [w_cq, w_ckv, kr_tiled, w_gate]
    w_gdn = [w_qkv, w_gz, _pad_last(jnp.concatenate([w_gb, w_ga], axis=2))]

    uq = mla_w_uq.reshape(depth, MLA_Q_RANK, MLA_HEADS, MLA_NOPE + MLA_ROPE)
    wuq = jnp.concatenate([uq[..., :MLA_NOPE].reshape(depth, MLA_Q_RANK, -1),
                           uq[..., MLA_NOPE:MLA_NOPE + ROPE_HALF].reshape(depth, MLA_Q_RANK, -1),
                           uq[..., MLA_NOPE + ROPE_HALF:].reshape(depth, MLA_Q_RANK, -1)], axis=2).astype(BF16)
    hp = MLA_HEADS // 2
    uk = jnp.transpose(mla_w_uk, (0, 2, 3, 1)).reshape(depth, hp, 2, MLA_NOPE, MLA_KV_RANK)
    uv = jnp.transpose(mla_w_uv, (0, 2, 1, 3)).reshape(depth, hp, 2, MLA_KV_RANK, MLA_V)
    zk = jnp.zeros_like(uk[:, :, 0])
    zv = jnp.zeros_like(uv[:, :, 0])
    wuk_pairs = jnp.concatenate([jnp.concatenate([uk[:, :, 0], zk], axis=-1),
                                 jnp.concatenate([zk, uk[:, :, 1]], axis=-1)], axis=-2).astype(BF16)
    wuv_pairs = jnp.concatenate([jnp.concatenate([uv[:, :, 0], zv], axis=-1),
                                 jnp.concatenate([zv, uv[:, :, 1]], axis=-1)], axis=-2).astype(BF16)

    row = lambda v: v[:, None, :]
    return dict(
        w_in=jnp.concatenate(w_ssd + w_mla + w_gdn, axis=2).astype(BF16),
        ssd_cw=ssd_conv_w, ssd_cb=row(ssd_conv_b),
        ssd_dtb=_pad_last(row(ssd_dt_bias)), ssd_alog=_pad_last(row(ssd_a_log)),
        ssd_dexp=row(jnp.repeat(ssd_d, SSD_HEAD_DIM, axis=1)), ssd_nw=row(ssd_norm_w),
        qnw=row(mla_q_norm_w), kvnw=row(mla_kv_norm_w), wuq=wuq,
        wuk_pairs=wuk_pairs, wuv_pairs=wuv_pairs,
        gdn_cw=gdn_conv_w, gdn_bias=_pad_last(row(gdn_dt_bias), before=GDN_HEADS),
        gdn_alog=_pad_last(row(gdn_a_log), before=GDN_HEADS), gdn_nw=row(gdn_norm_w),
        w_out=w_out.astype(BF16), ln_g=row(ln_g), ln_b=row(ln_b))


def _rope_tables(pos):
    inv = ROPE_THETA ** (-jnp.arange(ROPE_HALF, dtype=F32) / ROPE_HALF)
    ang = pos.astype(F32)[:, None] * inv[None, :]
    return jnp.tile(jnp.cos(ang), (1, MLA_HEADS)), jnp.tile(jnp.sin(ang), (1, MLA_HEADS))


def _head_expand_matrix():
    e = np.zeros((LANES, SSD_WIDTH), np.float32)
    for h in range(SSD_HEADS):
        e[h, h * SSD_HEAD_DIM:(h + 1) * SSD_HEAD_DIM] = 1.0
    return jnp.asarray(e, BF16)


def _rope_select_matrix():
    s = np.zeros((2 * LANES, MLA_ROPE), np.float32)
    for j in range(2 * LANES):
        s[j, (j // LANES) * ROPE_HALF + (j % ROPE_HALF)] = 1.0
    return jnp.asarray(s, BF16)


def _tail8(conv_state):
    return jnp.pad(conv_state, ((0, 0), (0, 0), (SUBLANES - (CONV_WIDTH - 1), 0), (0, 0)))


def _trunk(x, pos, tv, ssd_conv, ssd_state, gdn_conv, gdn_state, emb_g, emb_b, w, cfg, paged=None):
    b, t, _ = x.shape
    m = b * t
    tm = cfg["tm"]
    cos_t, sin_t = _rope_tables(pos)
    if cos_t.shape[0] < tm:
        reps = tm // cos_t.shape[0]
        cos_t, sin_t = jnp.tile(cos_t, (reps, 1)), jnp.tile(sin_t, (reps, 1))
    emat = _head_expand_matrix()
    sel = _rope_select_matrix()
    ssd_c8_in, gdn_c8_in = _tail8(ssd_conv), _tail8(gdn_conv)
    gw = SSD_WIDTH // SSD_GROUPS
    ssd_h_in = ssd_state.reshape(DEPTH, b, SSD_GROUPS, gw, SSD_STATE)
    h = _layer_norm(x.reshape(m, D_MODEL), emb_g, emb_b, tm)
    new_states = []
    for l in range(DEPTH):
        p_ssd, p_dt, p_mla, p_gdn, p_ba = _project(h, w["w_in"], l, cfg["tm_proj"])

        y_ssd, ssd_c8, ssd_h = _ssd_scan(
            p_ssd.reshape(b, t, -1), p_dt.reshape(b, t, LANES), ssd_c8_in, ssd_h_in, l,
            w["ssd_cw"], w["ssd_cb"], w["ssd_dtb"], w["ssd_alog"], w["ssd_dexp"], w["ssd_nw"], emat,
            L=cfg["ssd_chunk"], tv=min(tv, cfg["ssd_chunk"]))
        y_gdn, gdn_c8, gdn_s = _gdn_scan(
            p_gdn.reshape(b, t, -1), p_ba.reshape(b, t, LANES), gdn_c8_in, gdn_state, l,
            w["gdn_cw"], w["gdn_bias"], w["gdn_alog"], w["gdn_nw"],
            L=cfg["gdn_chunk"], tv=min(tv, cfg["gdn_chunk"]))

        q, kcat, kt, ckv, kr_t, gate = _mla_prep(p_mla, cos_t, sin_t, l, w["qnw"], w["kvnw"], w["wuq"],
                                                 w["wuk_pairs"], b, t, tm, tq_t=cfg["tq"] if paged is None else None)
        kcat = kcat.reshape(b, t, QK_WIDTH)
        if paged is None:
            o_lat = _flash_attention(q, kcat, kt, tq=cfg["tq"], tk=cfg["tk"])
        else:
            cache_lat, cache_rope_t, page_table = paged
            o_lat = _paged_attention(q, kcat, cache_lat, cache_rope_t, page_table, sel, tpad=t, tv=tv, layer=l)

        h = _out_proj(h, y_ssd.reshape(m, SSD_WIDTH), o_lat, gate, y_gdn.reshape(m, GDN_WIDTH), l,
                      w["wuv_pairs"], w["w_out"], w["ln_g"], w["ln_b"], tm)

        kr = jnp.concatenate([kr_t[:, 0:ROPE_HALF], kr_t[:, LANES:LANES + ROPE_HALF]], axis=1)
        sl = slice(SUBLANES - (CONV_WIDTH - 1), SUBLANES)
        new_states.append((ckv.reshape(b, t, MLA_KV_RANK), kr.reshape(b, t, MLA_ROPE), ssd_c8[:, sl],
                           ssd_h.reshape(b, SSD_HEADS, SSD_HEAD_DIM, SSD_STATE), gdn_c8[:, sl], gdn_s))
    return h.reshape(b, t, D_MODEL), tuple(jnp.stack(s) for s in zip(*new_states))


def kernel(x_prompt, x_sample, cache_kv_latent, cache_k_rope, state_ssd_conv, state_ssd, state_gdn_conv, state_gdn, page_table, emb_ln_g, emb_ln_b, w_in, ssd_conv_w, ssd_conv_b, ssd_dt_bias, ssd_a_log, ssd_d, ssd_norm_w, mla_q_norm_w, mla_w_uq, mla_kv_norm_w, mla_w_uk, mla_w_uv, gdn_conv_w, gdn_dt_bias, gdn_a_log, gdn_norm_w, w_out, ln_g, ln_b):
    weights = _prep_weights(w_in, ssd_conv_w, ssd_conv_b, ssd_dt_bias, ssd_a_log, ssd_d, ssd_norm_w,
                            mla_q_norm_w, mla_w_uq, mla_kv_norm_w, mla_w_uk, mla_w_uv,
                            gdn_conv_w, gdn_dt_bias, gdn_a_log, gdn_norm_w, w_out, ln_g, ln_b)

    bp, tp, _ = x_prompt.shape
    zeros = lambda *s: jnp.zeros(s, F32)
    ssd_chunk = SSD_CHUNK if tp % SSD_CHUNK == 0 else tp
    gdn_chunk = GDN_CHUNK if tp % GDN_CHUNK == 0 else tp
    cfg_p = dict(tm=min(512, bp * tp), tm_proj=min(512, bp * tp), ssd_chunk=ssd_chunk, gdn_chunk=gdn_chunk,
                 tq=min(256, tp), tk=min(512, tp))
    y_prompt, st_p = _trunk(
        x_prompt, jnp.arange(tp), tp,
        zeros(DEPTH, bp, CONV_WIDTH - 1, SSD_CONV_DIM), zeros(DEPTH, bp, SSD_HEADS, SSD_HEAD_DIM, SSD_STATE),
        zeros(DEPTH, bp, CONV_WIDTH - 1, GDN_CONV_DIM), zeros(DEPTH, bp, GDN_HEADS, GDN_HEAD_DIM, GDN_HEAD_DIM),
        emb_ln_g, emb_ln_b, weights, cfg_p)

    bs, ts, _ = x_sample.shape
    tpad = -(-ts // BF16_ROWS) * BF16_ROWS
    past_len = page_table.shape[1] * PAGE_SIZE
    xs = jnp.pad(x_sample, ((0, 0), (0, tpad - ts), (0, 0)))
    cfg_s = dict(tm=min(512, bs * tpad), tm_proj=min(512, bs * tpad), ssd_chunk=tpad, gdn_chunk=tpad,
                 tq=tpad, tk=tpad)
    y_s, st_s = _trunk(
        xs, past_len + jnp.arange(tpad), ts, state_ssd_conv, state_ssd, state_gdn_conv, state_gdn,
        emb_ln_g, emb_ln_b, weights, cfg_s,
        paged=(cache_kv_latent, jnp.swapaxes(cache_k_rope, 2, 3), page_table))
    y_sample = y_s[:, :ts]
    s_lat, s_rope, s_ssd_conv, s_ssd, s_gdn_conv, s_gdn = st_s
    return (y_prompt, y_sample) + tuple(st_p) + (s_lat[:, :, :ts], s_rope[:, :, :ts], s_ssd_conv, s_ssd, s_gdn_conv, s_gdn)
```

```python
import functools
import math

import jax
import jax.numpy as jnp
import numpy as np
from jax import lax
from jax.experimental import pallas as pl
from jax.experimental.pallas import tpu as pltpu

F32 = jnp.float32
BF16 = jnp.bfloat16

D_MODEL = 1024
CONV_WIDTH = 4
SSD_HEADS = 16
SSD_HEAD_DIM = 64
SSD_WIDTH = SSD_HEADS * SSD_HEAD_DIM
SSD_GROUPS = 2
SSD_STATE = 128
SSD_CONV_DIM = SSD_WIDTH + 2 * SSD_GROUPS * SSD_STATE
SSD_CHUNK = 128
MLA_HEADS = 8
MLA_NOPE = 64
MLA_ROPE = 32
MLA_V = 64
MLA_WIDTH = MLA_HEADS * MLA_V
MLA_Q_RANK = 384
MLA_KV_RANK = 256
MLA_SCALE = (MLA_NOPE + MLA_ROPE) ** -0.5
ROPE_THETA = 10000.0
GDN_HEADS = 4
GDN_HEAD_DIM = 128
GDN_WIDTH = GDN_HEADS * GDN_HEAD_DIM
GDN_CONV_DIM = 3 * GDN_WIDTH
GDN_CHUNK = 64
MIX_WIDTH = SSD_WIDTH + MLA_WIDTH + GDN_WIDTH
IN_SIZES = (SSD_WIDTH, SSD_CONV_DIM, SSD_HEADS, MLA_Q_RANK, MLA_KV_RANK, MLA_ROPE, MLA_WIDTH,
            GDN_CONV_DIM, GDN_WIDTH, GDN_HEADS, GDN_HEADS)
IN_OFFSETS = tuple(int(o) for o in np.cumsum(IN_SIZES)[:-1])
DEPTH = 2
DEEPNORM_ALPHA = (2 * DEPTH) ** 0.25
LN_EPS = 1e-5
RMS_EPS = 1e-6
L2_EPS = 1e-6
PAGE_SIZE = 128

LANES = 128
SUBLANES = 8
BF16_ROWS = 16
ROPE_HALF = MLA_ROPE // 2
P_SSD = SSD_WIDTH + SSD_CONV_DIM + LANES
P_MLA = MLA_Q_RANK + MLA_KV_RANK + 2 * LANES + MLA_WIDTH
P_GDN = GDN_CONV_DIM + GDN_WIDTH + LANES
QK_WIDTH = MLA_KV_RANK + 2 * LANES
Q_SCALE = MLA_SCALE * math.log2(math.e)
VMEM_LIMIT = 56 * 1024 * 1024
PAGES_PER_GROUP = 32
GDN_BATCH_PER_STEP = 4
SSD_BATCH_PER_STEP = 2

def _cparams(sem):
    return pltpu.CompilerParams(dimension_semantics=sem, vmem_limit_bytes=VMEM_LIMIT)


def _bdot(a, b):
    return jnp.dot(a.astype(BF16), b.astype(BF16), preferred_element_type=F32)


def _bdot_nt(a, b):
    return lax.dot_general(a.astype(BF16), b.astype(BF16), (((1,), (1,)), ((), ())),
                           preferred_element_type=F32)


def _fdot(a, b):
    return jnp.dot(a, b, precision=lax.Precision.HIGHEST, preferred_element_type=F32)


def _silu(x):
    return x * (1.0 / (1.0 + jnp.exp(-x)))


def _softplus(x):
    return jnp.maximum(x, 0.0) + jnp.log1p(jnp.exp(-jnp.abs(x)))


def _const_spec(shape):
    nd = len(shape)
    return pl.BlockSpec(shape, lambda *_: (0,) * nd)


def _layer_spec(stacked, l):
    rest = stacked.shape[1:]
    return pl.BlockSpec((None,) + rest, lambda *_: (l,) + (0,) * len(rest))


def _layer_norm_rows(x, g, b):
    mu = jnp.mean(x, -1, keepdims=True)
    xc = x - mu
    var = jnp.mean(xc * xc, -1, keepdims=True)
    return xc * lax.rsqrt(var + LN_EPS) * g + b


def _proj_kernel(x_ref, w_ref, *refs, embed):
    if embed:
        g_ref, b_ref, h_ref, ssd_ref, dt_ref, mla_ref, gdn_ref, ba_ref = refs
        h = _layer_norm_rows(x_ref[...], g_ref[...], b_ref[...])
        h_ref[...] = h
        x = h.astype(BF16)
    else:
        ssd_ref, dt_ref, mla_ref, gdn_ref, ba_ref = refs
        x = x_ref[...].astype(BF16)
    p = jnp.dot(x, w_ref[:, 0:P_SSD], preferred_element_type=F32)
    ssd_ref[...] = p[:, 0:P_SSD - LANES].astype(BF16)
    dt_ref[...] = p[:, P_SSD - LANES:P_SSD]
    mla_ref[...] = jnp.dot(x, w_ref[:, P_SSD:P_SSD + P_MLA], preferred_element_type=F32).astype(BF16)
    p = jnp.dot(x, w_ref[:, P_SSD + P_MLA:P_SSD + P_MLA + P_GDN], preferred_element_type=F32)
    gdn_ref[...] = p[:, 0:P_GDN - LANES].astype(BF16)
    ba_ref[...] = p[:, P_GDN - LANES:P_GDN]


def _project(x, w, l, tm, emb_ln=None):
    m, k = x.shape
    row = lambda i: (i, 0)
    widths = (P_SSD - LANES, LANES, P_MLA, P_GDN - LANES, LANES)
    dtypes = (BF16, F32, BF16, BF16, F32)
    operands, in_specs = [x, w], [pl.BlockSpec((tm, k), row), _layer_spec(w, l)]
    if emb_ln is not None:
        operands += [v.reshape(1, k) for v in emb_ln]
        in_specs += [_const_spec((1, k)), _const_spec((1, k))]
        widths, dtypes = (k,) + widths, (F32,) + dtypes
    return pl.pallas_call(
        functools.partial(_proj_kernel, embed=emb_ln is not None), grid=(m // tm,),
        in_specs=in_specs,
        out_specs=[pl.BlockSpec((tm, n), row) for n in widths],
        out_shape=[jax.ShapeDtypeStruct((m, n), d) for n, d in zip(widths, dtypes)],
        compiler_params=_cparams(("parallel",)), name="in_proj")(*operands)


def _conv_chunk(xbuf, x_new, cw_ref, L):
    xbuf[SUBLANES:SUBLANES + L, :] = x_new
    y = cw_ref[CONV_WIDTH - 1:CONV_WIDTH, :] * x_new
    for k in range(CONV_WIDTH - 1):
        off = SUBLANES - (CONV_WIDTH - 1) + k
        y = y + cw_ref[k:k + 1, :] * xbuf[off:off + L, :]
    return y


def _ssd_chunk(p_ref, dt_ref, cw_ref, cb_ref, dtb_ref, alog_ref, dexp_ref, nw_ref, e_ref,
               y_ref, cnew_ref, xbuf, ht, *, L, tv):
    z = p_ref[:, 0:SSD_WIDTH].astype(F32)
    xbc_raw = p_ref[:, SSD_WIDTH:SSD_WIDTH + SSD_CONV_DIM].astype(F32)
    dt_raw = dt_ref[...]

    xbc = _silu(_conv_chunk(xbuf, xbc_raw, cw_ref, L) + cb_ref[...])
    cnew_ref[...] = xbuf[tv:tv + SUBLANES, :]
    xbuf[0:SUBLANES, :] = xbuf[L:L + SUBLANES, :]
    yield

    xs = xbc[:, 0:SSD_WIDTH]
    gs = SSD_GROUPS * SSD_STATE
    bm = xbc[:, SSD_WIDTH:SSD_WIDTH + gs]
    cm = xbc[:, SSD_WIDTH + gs:SSD_WIDTH + 2 * gs]

    dt = _softplus(dt_raw + dtb_ref[...])
    if tv < L:
        rows = lax.broadcasted_iota(jnp.int32, (L, LANES), 0)
        dt = jnp.where(rows < tv, dt, 0.0)
    a = -jnp.exp(alog_ref[...])
    da = dt * a
    r_i = lax.broadcasted_iota(jnp.int32, (L, L), 0)
    c_i = lax.broadcasted_iota(jnp.int32, (L, L), 1)
    causal = c_i <= r_i
    tri = jnp.where(causal, 1.0, 0.0).astype(F32)
    acum = _fdot(tri, da)
    last = acum[L - 1:L, :]
    ea = jnp.exp(acum)
    wdec = jnp.exp(last - acum)

    def pad_rows(v):
        if L == LANES:
            return v
        return jnp.concatenate([v, jnp.zeros((LANES - L, v.shape[1]), v.dtype)], axis=0)

    acum_t = jnp.transpose(pad_rows(acum))
    yield

    def hilo(v):
        hi = v.astype(BF16)
        lo = (v - hi.astype(F32)).astype(BF16)
        return hi, lo

    parts = []
    for v in (dt, ea, wdec):
        parts.extend(hilo(v))
    stacked = jnp.concatenate(parts, axis=0)
    expanded = jnp.dot(stacked, e_ref[...], preferred_element_type=F32)
    dt_e = expanded[0:L] + expanded[L:2 * L]
    ea_e = expanded[2 * L:3 * L] + expanded[3 * L:4 * L]
    wd_e = expanded[4 * L:5 * L] + expanded[5 * L:6 * L]

    yield
    xdt = xs * dt_e
    xdt_b = xdt.astype(BF16)
    lane = lax.broadcasted_iota(jnp.int32, (L, LANES), 1)
    hpg = SSD_HEADS // SSD_GROUPS

    y_parts = []
    for g in range(SSD_GROUPS):
        bg = bm[:, g * SSD_STATE:(g + 1) * SSD_STATE]
        cg = cm[:, g * SSD_STATE:(g + 1) * SSD_STATE]
        cb = _bdot_nt(cg, bg)
        for j in range(hpg // 2):
            pair = g * (hpg // 2) + j
            xp = xdt_b[:, pair * LANES:(pair + 1) * LANES]
            ys = []
            for h in (2 * pair, 2 * pair + 1):
                seg = acum[:, h:h + 1] - acum_t[h:h + 1, 0:L]
                dec = jnp.exp(jnp.where(causal, seg, -jnp.inf))
                ys.append(_bdot(cb * dec, xp))
            y_parts.append(jnp.where(lane < SSD_HEAD_DIM, ys[0], ys[1]))
            yield
    y_in = jnp.concatenate(y_parts, axis=1)

    gw = hpg * SSD_HEAD_DIM
    y_st_parts = []
    for g in range(SSD_GROUPS):
        cg = cm[:, g * SSD_STATE:(g + 1) * SSD_STATE]
        y_st_parts.append(_bdot(cg, ht[g]))
    y_st = jnp.concatenate(y_st_parts, axis=1) * ea_e
    yield

    xw = (xdt * wd_e)
    for g in range(SSD_GROUPS):
        bg_t = jnp.transpose(pad_rows(bm[:, g * SSD_STATE:(g + 1) * SSD_STATE]))
        xw_g = pad_rows(xw[:, g * gw:(g + 1) * gw])
        ht[g] = ht[g] * ea_e[L - 1:L, g * gw:(g + 1) * gw] + _bdot(bg_t, xw_g)
    yield

    y = (y_in + y_st + dexp_ref[...] * xs) * _silu(z)
    outs = []
    for g in range(SSD_GROUPS):
        yg = y[:, g * gw:(g + 1) * gw]
        ms = jnp.mean(yg * yg, -1, keepdims=True)
        outs.append(yg * lax.rsqrt(ms + RMS_EPS) * nw_ref[:, g * gw:(g + 1) * gw])
    y_ref[...] = jnp.concatenate(outs, axis=1).astype(y_ref.dtype)


def _lockstep(gens):
    while gens:
        gens = [g for g in gens if next(g, StopIteration) is not StopIteration]


def _ssd_kernel(p_ref, dt_ref, cprev_ref, h0_ref, cw_ref, cb_ref, dtb_ref, alog_ref, dexp_ref, nw_ref, e_ref,
                y_ref, cnew_ref, hout_ref, xbuf, ht, *, L, tv, nbat):
    c = pl.program_id(1)
    nc = pl.num_programs(1)

    @pl.when(c == 0)
    def _():
        xbuf[:, 0:SUBLANES, :] = cprev_ref[...]
        for i in range(nbat):
            for g in range(SSD_GROUPS):
                ht[i, g] = jnp.transpose(h0_ref[i, g])

    _lockstep([_ssd_chunk(p_ref.at[i], dt_ref.at[i], cw_ref, cb_ref, dtb_ref, alog_ref, dexp_ref, nw_ref, e_ref,
                          y_ref.at[i], cnew_ref.at[i], xbuf.at[i], ht.at[i], L=L, tv=tv) for i in range(nbat)])

    @pl.when(c == nc - 1)
    def _():
        for i in range(nbat):
            for g in range(SSD_GROUPS):
                hout_ref[i, g] = jnp.transpose(ht[i, g])


def _ssd_scan(p_ssd, p_dt, conv_prev8, h0, l, cw, cb, dtb, alog, dexp, nw, emat, *, L, tv):
    b, t, _ = p_ssd.shape
    nc = t // L
    gw = SSD_WIDTH // SSD_GROUPS
    nbat = math.gcd(b, SSD_BATCH_PER_STEP)
    kern = functools.partial(_ssd_kernel, L=L, tv=tv, nbat=nbat)
    hshape = (nbat, SSD_GROUPS, gw, SSD_STATE)
    return pl.pallas_call(
        kern, grid=(b // nbat, nc),
        in_specs=[pl.BlockSpec((nbat, L, P_SSD - LANES), lambda i, c: (i, c, 0)),
                  pl.BlockSpec((nbat, L, LANES), lambda i, c: (i, c, 0)),
                  pl.BlockSpec((None, nbat, SUBLANES, SSD_CONV_DIM), lambda i, c: (l, i, 0, 0)),
                  pl.BlockSpec((None,) + hshape, lambda i, c: (l, i, 0, 0, 0)),
                  _layer_spec(cw, l), _layer_spec(cb, l), _layer_spec(dtb, l), _layer_spec(alog, l),
                  _layer_spec(dexp, l), _layer_spec(nw, l), _const_spec((LANES, SSD_WIDTH))],
        out_specs=[pl.BlockSpec((nbat, L, SSD_WIDTH), lambda i, c: (i, c, 0)),
                   pl.BlockSpec((nbat, SUBLANES, SSD_CONV_DIM), lambda i, c: (i, 0, 0)),
                   pl.BlockSpec(hshape, lambda i, c: (i, 0, 0, 0))],
        out_shape=[jax.ShapeDtypeStruct((b, t, SSD_WIDTH), BF16),
                   jax.ShapeDtypeStruct((b, SUBLANES, SSD_CONV_DIM), F32),
                   jax.ShapeDtypeStruct((b,) + hshape[1:], F32)],
        scratch_shapes=[pltpu.VMEM((nbat, L + SUBLANES, SSD_CONV_DIM), F32),
                        pltpu.VMEM((nbat, SSD_GROUPS, SSD_STATE, gw), F32)],
        compiler_params=_cparams(("parallel", "arbitrary")), name="ssd_scan",
    )(p_ssd, p_dt, conv_prev8, h0, cw, cb, dtb, alog, dexp, nw, emat)


def _gdn_chunk(p_ref, ba_ref, cw_ref, bias_ref, alog_ref, nw_ref, y_ref, cnew_ref, xbuf, st, *, L, tv):
    H = GDN_HEADS
    D = GDN_HEAD_DIM
    R = H * L
    S = max(R, LANES)

    qkv_raw = p_ref[:, 0:GDN_CONV_DIM].astype(F32)
    z = p_ref[:, GDN_CONV_DIM:GDN_CONV_DIM + GDN_WIDTH].astype(F32)
    ba = ba_ref[...]

    qkv = _silu(_conv_chunk(xbuf, qkv_raw, cw_ref, L))
    cnew_ref[...] = xbuf[tv:tv + SUBLANES, :]
    xbuf[0:SUBLANES, :] = xbuf[L:L + SUBLANES, :]
    yield

    beta_f = 1.0 / (1.0 + jnp.exp(-ba))
    g_f = -jnp.exp(alog_ref[...]) * _softplus(ba + bias_ref[...])
    if tv < L:
        rows = lax.broadcasted_iota(jnp.int32, (L, LANES), 0)
        beta_f = jnp.where(rows < tv, beta_f, 0.0)
        g_f = jnp.where(rows < tv, g_f, 0.0)
    r_i = lax.broadcasted_iota(jnp.int32, (L, L), 0)
    c_i = lax.broadcasted_iota(jnp.int32, (L, L), 1)
    tri = jnp.where(c_i <= r_i, 1.0, 0.0).astype(F32)
    gcum_f = _fdot(tri, g_f)
    glast_f = jnp.broadcast_to(gcum_f[L - 1:L, :], (L, LANES))

    def pad_s(v):
        if R == S:
            return v
        return jnp.concatenate([v, jnp.zeros((S - R, v.shape[1]), v.dtype)], axis=0)

    def stack(v):
        return pad_s(jnp.concatenate([v[:, h * D:(h + 1) * D] for h in range(H)], axis=0))

    def col(v, off):
        return pad_s(jnp.concatenate([v[:, off + h:off + h + 1] for h in range(H)], axis=0))

    q_s = stack(qkv[:, 0:GDN_WIDTH])
    k_s = stack(qkv[:, GDN_WIDTH:2 * GDN_WIDTH])
    v_s = stack(qkv[:, 2 * GDN_WIDTH:3 * GDN_WIDTH])
    z_s = stack(z)
    q_s = q_s * lax.rsqrt(jnp.sum(q_s * q_s, -1, keepdims=True) + L2_EPS) * (D ** -0.5)
    k_s = k_s * lax.rsqrt(jnp.sum(k_s * k_s, -1, keepdims=True) + L2_EPS)
    beta = col(beta_f, 0)
    gcum = col(gcum_f, H)
    glast = col(glast_f, H)
    yield

    cmat = jnp.broadcast_to(gcum, (S, S))
    diff = cmat - jnp.transpose(cmat)
    rs = lax.broadcasted_iota(jnp.int32, (S, S), 0)
    cs = lax.broadcasted_iota(jnp.int32, (S, S), 1)
    if L & (L - 1) == 0:
        sh = L.bit_length() - 1
        same = (rs >> sh) == (cs >> sh)
    else:
        same = (rs // L) == (cs // L)
    incl = same & (cs <= rs)
    strict = same & (cs < rs)
    dec = jnp.exp(jnp.where(incl, diff, -jnp.inf))

    kb = k_s * beta
    n_mat = -jnp.where(strict, _bdot_nt(kb, k_s) * dec, 0.0)
    eye = jnp.where(rs == cs, 1.0, 0.0).astype(F32)
    t_mat = eye + n_mat
    npow = n_mat
    yield
    span = 2
    while span < L:
        npow = _bdot(npow, npow)
        yield
        t_mat = t_mat + _bdot(t_mat, npow)
        yield
        span *= 2

    eg = jnp.exp(gcum)
    t_b = t_mat.astype(BF16)
    u = _bdot(t_b, v_s * beta)
    w = _bdot(t_b, kb * eg)
    qg = q_s * eg
    attn = jnp.where(incl, _bdot_nt(q_s, k_s) * dec, 0.0)
    yield
    ws = []
    qs_ = []
    for h in range(H):
        sh_b = st[h].astype(BF16)
        ws.append(_bdot(w[h * L:(h + 1) * L], sh_b))
        qs_.append(_bdot(qg[h * L:(h + 1) * L], sh_b))
    v_new = u - pad_s(jnp.concatenate(ws, axis=0))
    yield
    o = pad_s(jnp.concatenate(qs_, axis=0)) + _bdot(attn, v_new)

    kd = k_s * jnp.exp(glast - gcum)
    kd_t = jnp.transpose(kd).astype(BF16)
    row_head = lax.broadcasted_iota(jnp.int32, (S, D), 0)
    eg_last = jnp.exp(glast)
    for h in range(H):
        vm = jnp.where((row_head >= h * L) & (row_head < (h + 1) * L), v_new, 0.0)
        st[h] = st[h] * eg_last[h * L:h * L + 1, :] + _bdot(kd_t, vm)
    yield

    ms = jnp.mean(o * o, -1, keepdims=True)
    o = o * lax.rsqrt(ms + RMS_EPS) * nw_ref[...] * _silu(z_s)
    y_ref[...] = jnp.concatenate([o[h * L:(h + 1) * L] for h in range(H)], axis=1).astype(y_ref.dtype)


def _gdn_kernel(p_ref, ba_ref, cprev_ref, s0_ref, cw_ref, bias_ref, alog_ref, nw_ref,
                y_ref, cnew_ref, sout_ref, xbuf, st, *, L, tv, nbat):
    c = pl.program_id(1)
    nc = pl.num_programs(1)

    @pl.when(c == 0)
    def _():
        xbuf[:, 0:SUBLANES, :] = cprev_ref[...]
        st[...] = s0_ref[...]

    _lockstep([_gdn_chunk(p_ref.at[i], ba_ref.at[i], cw_ref, bias_ref, alog_ref, nw_ref, y_ref.at[i],
                          cnew_ref.at[i], xbuf.at[i], st.at[i], L=L, tv=tv) for i in range(nbat)])

    @pl.when(c == nc - 1)
    def _():
        sout_ref[...] = st[...]


def _gdn_scan(p_gdn, p_ba, conv_prev8, s0, l, cw, bias, alog, nw, *, L, tv):
    b, t, _ = p_gdn.shape
    nc = t // L
    nbat = math.gcd(b, GDN_BATCH_PER_STEP)
    kern = functools.partial(_gdn_kernel, L=L, tv=tv, nbat=nbat)
    sshape = (nbat, GDN_HEADS, GDN_HEAD_DIM, GDN_HEAD_DIM)
    return pl.pallas_call(
        kern, grid=(b // nbat, nc),
        in_specs=[pl.BlockSpec((nbat, L, P_GDN - LANES), lambda i, c: (i, c, 0)),
                  pl.BlockSpec((nbat, L, LANES), lambda i, c: (i, c, 0)),
                  pl.BlockSpec((None, nbat, SUBLANES, GDN_CONV_DIM), lambda i, c: (l, i, 0, 0)),
                  pl.BlockSpec((None,) + sshape, lambda i, c: (l, i, 0, 0, 0)),
                  _layer_spec(cw, l), _layer_spec(bias, l), _layer_spec(alog, l), _layer_spec(nw, l)],
        out_specs=[pl.BlockSpec((nbat, L, GDN_WIDTH), lambda i, c: (i, c, 0)),
                   pl.BlockSpec((nbat, SUBLANES, GDN_CONV_DIM), lambda i, c: (i, 0, 0)),
                   pl.BlockSpec(sshape, lambda i, c: (i, 0, 0, 0))],
        out_shape=[jax.ShapeDtypeStruct((b, t, GDN_WIDTH), BF16),
                   jax.ShapeDtypeStruct((b, SUBLANES, GDN_CONV_DIM), F32),
                   jax.ShapeDtypeStruct((b,) + sshape[1:], F32)],
        scratch_shapes=[pltpu.VMEM((nbat, L + SUBLANES, GDN_CONV_DIM), F32),
                        pltpu.VMEM(sshape, F32)],
        compiler_params=_cparams(("parallel", "arbitrary")), name="gdn_scan",
    )(p_gdn, p_ba, conv_prev8, s0, cw, bias, alog, nw)


def _mla_prep_kernel(p_ref, cos_ref, sin_ref, qnw_ref, kvnw_ref, wuq_ref, wuk_ref,
                     q_ref, kcat_ref, kt_ref, ckv_ref, kr_ref, gate_ref, *, nb, tt, tq_t):
    cq = p_ref[:, 0:MLA_Q_RANK].astype(F32)
    ckv = p_ref[:, MLA_Q_RANK:MLA_Q_RANK + MLA_KV_RANK].astype(F32)
    kr = p_ref[:, MLA_Q_RANK + MLA_KV_RANK:MLA_Q_RANK + MLA_KV_RANK + 2 * LANES].astype(F32)
    gate = p_ref[:, MLA_Q_RANK + MLA_KV_RANK + 2 * LANES:P_MLA].astype(F32)
    cos = cos_ref[...]
    sin = sin_ref[...]

    cqn = cq * lax.rsqrt(jnp.mean(cq * cq, -1, keepdims=True) + RMS_EPS) * qnw_ref[...]
    q = _bdot(cqn, wuq_ref[...])
    nope_w = MLA_HEADS * MLA_NOPE
    x1 = q[:, nope_w:nope_w + LANES]
    x2 = q[:, nope_w + LANES:nope_w + 2 * LANES]
    r_all = jnp.concatenate([x1 * cos - x2 * sin, x2 * cos + x1 * sin], axis=1) * Q_SCALE

    ckvn = ckv * lax.rsqrt(jnp.mean(ckv * ckv, -1, keepdims=True) + RMS_EPS) * kvnw_ref[...]
    k1 = kr[:, 0:LANES]
    k2 = kr[:, LANES:2 * LANES]
    kr_rot = jnp.concatenate([k1 * cos - k2 * sin, k2 * cos + k1 * sin], axis=1)
    ckv_ref[...] = ckvn
    kr_ref[...] = kr_rot
    kcat_ref[...] = jnp.concatenate([ckvn, kr_rot], axis=1).astype(BF16)
    kt_ref[...] = jnp.transpose(ckvn).astype(BF16)
    gate_ref[...] = _silu(gate).astype(gate_ref.dtype)

    tm = q.shape[0]
    if tq_t is None:
        lane = lax.broadcasted_iota(jnp.int32, (1, 2 * LANES), 1)
        head_of_lane = (lane & (LANES - 1)) >> 4
    else:
        r_all_t = jnp.transpose(r_all)
        row_i = lax.broadcasted_iota(jnp.int32, (2 * LANES, 1), 0)
        head_of_row = (row_i & (LANES - 1)) >> 4
    for pair in range(MLA_HEADS // 2):
        qn = q[:, pair * LANES:(pair + 1) * LANES] * Q_SCALE
        qlat = _bdot(qn, wuk_ref[pair])
        for i in range(2):
            h = 2 * pair + i
            ql = qlat[:, i * MLA_KV_RANK:(i + 1) * MLA_KV_RANK]
            if tq_t is None:
                qr = jnp.where(head_of_lane == h, r_all, 0.0).astype(BF16)
                q_ref[:, h, :, 0:MLA_KV_RANK] = ql.astype(BF16).reshape(nb, tt, MLA_KV_RANK)
                q_ref[:, h, :, MLA_KV_RANK:QK_WIDTH] = qr.reshape(nb, tt, 2 * LANES)
            else:
                ql_t = jnp.transpose(ql).astype(BF16)
                qr_t = jnp.where(head_of_row == h, r_all_t, 0.0).astype(BF16)
                for qb in range(tm // tq_t):
                    cols = slice(h * tq_t, (h + 1) * tq_t)
                    toks = slice(qb * tq_t, (qb + 1) * tq_t)
                    q_ref[0, qb, 0:MLA_KV_RANK, cols] = ql_t[:, toks]
                    q_ref[0, qb, MLA_KV_RANK:QK_WIDTH, cols] = qr_t[:, toks]


def _head_major_block(t, tm):
    if tm <= t:
        npb = t // tm
        return 1, tm, (lambda i: (i // npb, 0, i % npb, 0))
    return tm // t, t, (lambda i: (i, 0, 0, 0))


def _mla_prep(p_mla, cos_t, sin_t, l, qnw, kvnw, wuq, wuk_pairs, b, t, tm, tq_t=None):
    m = p_mla.shape[0]
    nt = cos_t.shape[0] // tm
    row = lambda i: (i, 0)
    tab = lambda i: (i % nt, 0)
    nb, tt, hm_map = _head_major_block(t, tm)
    kern = functools.partial(_mla_prep_kernel, nb=nb, tt=tt, tq_t=tq_t)
    if tq_t is None:
        q_spec = pl.BlockSpec((nb, MLA_HEADS, tt, QK_WIDTH), hm_map)
        q_shape = jax.ShapeDtypeStruct((b, MLA_HEADS, t, QK_WIDTH), BF16)
    else:
        npb = t // tm
        q_spec = pl.BlockSpec((1, tm // tq_t, QK_WIDTH, MLA_HEADS * tq_t), lambda i: (i // npb, i % npb, 0, 0))
        q_shape = jax.ShapeDtypeStruct((b, t // tq_t, QK_WIDTH, MLA_HEADS * tq_t), BF16)
    return pl.pallas_call(
        kern, grid=(m // tm,),
        in_specs=[pl.BlockSpec((tm, P_MLA), row), pl.BlockSpec((tm, LANES), tab), pl.BlockSpec((tm, LANES), tab),
                  _layer_spec(qnw, l), _layer_spec(kvnw, l), _layer_spec(wuq, l), _layer_spec(wuk_pairs, l)],
        out_specs=[q_spec, pl.BlockSpec((tm, QK_WIDTH), row),
                   pl.BlockSpec((MLA_KV_RANK, tm), lambda i: (0, i)),
                   pl.BlockSpec((tm, MLA_KV_RANK), row), pl.BlockSpec((tm, 2 * LANES), row),
                   pl.BlockSpec((tm, MLA_WIDTH), row)],
        out_shape=[q_shape,
                   jax.ShapeDtypeStruct((m, QK_WIDTH), BF16), jax.ShapeDtypeStruct((MLA_KV_RANK, m), BF16),
                   jax.ShapeDtypeStruct((m, MLA_KV_RANK), F32), jax.ShapeDtypeStruct((m, 2 * LANES), F32),
                   jax.ShapeDtypeStruct((m, MLA_WIDTH), BF16)],
        compiler_params=_cparams(("parallel",)), name="mla_prep",
    )(p_mla, cos_t, sin_t, qnw, kvnw, wuq, wuk_pairs)


def _softmax_rows(s, m_scr, l_scr):
    n = s.shape[1]
    m_prev = m_scr[...]
    m_new = jnp.maximum(m_prev, jnp.max(s, -1, keepdims=True))
    alpha = jnp.exp2(m_prev - m_new)
    m_wide = jnp.concatenate([m_new] * (n // LANES), axis=1) if n >= LANES else m_new[:, 0:n]
    p = jnp.exp2(s - m_wide)
    l_scr[...] = alpha * l_scr[...] + jnp.sum(p, -1, keepdims=True)
    m_scr[...] = m_new
    return p, alpha


def _softmax_init(m_scr, l_scr, acc_scr):
    m_scr[...] = jnp.full(m_scr.shape, -jnp.inf, F32)
    l_scr[...] = jnp.zeros(l_scr.shape, F32)
    acc_scr[...] = jnp.zeros(acc_scr.shape, F32)


def _flash_kernel(qi_ref, ki_ref, q_ref, k_ref, kt_ref, o_ref, m_scr, l_scr, acc_scr, *, tq, tk):
    step_i = pl.program_id(1)
    qi = qi_ref[step_i]
    ki = ki_ref[step_i]
    rows = tq * MLA_HEADS

    @pl.when(ki == 0)
    def _():
        _softmax_init(m_scr, l_scr, acc_scr)

    def step(masked, nkeys):
        s_t = jnp.dot(k_ref[0, 0:nkeys, :], q_ref[0, 0], preferred_element_type=F32)
        if masked:
            k_pos = ki * tk + lax.broadcasted_iota(jnp.int32, (nkeys, rows), 0)
            q_pos = qi * tq + (lax.broadcasted_iota(jnp.int32, (nkeys, rows), 1) & (tq - 1))
            s_t = jnp.where(k_pos <= q_pos, s_t, -jnp.inf)
        m_prev = m_scr[...]
        m_new = jnp.maximum(m_prev, jnp.max(s_t, axis=0, keepdims=True))
        alpha = jnp.exp2(m_prev - m_new)
        p_t = jnp.exp2(s_t - m_new)
        l_scr[...] = alpha * l_scr[...] + jnp.sum(p_t, axis=0, keepdims=True)
        m_scr[...] = m_new
        acc_scr[...] = acc_scr[...] * alpha + jnp.dot(kt_ref[:, 0:nkeys], p_t.astype(BF16),
                                                      preferred_element_type=F32)

    crosses = (ki + 1) * tk > qi * tq + 1
    short = (qi + 1) * tq - ki * tk
    if tq < tk and tk % tq == 0:
        use_short = crosses & (short == tq)

        @pl.when(use_short)
        def _():
            step(True, tq)
    else:
        use_short = False

    @pl.when(crosses & jnp.logical_not(use_short))
    def _():
        step(True, tk)

    @pl.when(jnp.logical_not(crosses))
    def _():
        step(False, tk)

    @pl.when(ki == ((qi + 1) * tq - 1) // tk)
    def _():
        o_t = acc_scr[...] * (1.0 / l_scr[...])
        o_ref[0] = jnp.transpose(o_t).astype(o_ref.dtype).reshape(MLA_HEADS, tq, MLA_KV_RANK)


def _flash_attention(q_t, kcat, kt, *, tq, tk):
    b, nq, _, rows = q_t.shape
    t = nq * tq
    nkb = t // tk
    kern = functools.partial(_flash_kernel, tq=tq, tk=tk)
    pairs = [(qi, ki) for qi in range(nq) for ki in range(((qi + 1) * tq - 1) // tk + 1)]
    qi_arr = jnp.asarray([p[0] for p in pairs], jnp.int32)
    ki_arr = jnp.asarray([p[1] for p in pairs], jnp.int32)
    grid_spec = pltpu.PrefetchScalarGridSpec(
        num_scalar_prefetch=2, grid=(b, len(pairs)),
        in_specs=[pl.BlockSpec((1, 1, QK_WIDTH, rows), lambda i, s, qa, ka: (i, qa[s], 0, 0)),
                  pl.BlockSpec((1, tk, QK_WIDTH), lambda i, s, qa, ka: (i, ka[s], 0)),
                  pl.BlockSpec((MLA_KV_RANK, tk), lambda i, s, qa, ka: (0, i * nkb + ka[s]))],
        out_specs=pl.BlockSpec((1, MLA_HEADS, tq, MLA_KV_RANK), lambda i, s, qa, ka: (i, 0, qa[s], 0)),
        scratch_shapes=[pltpu.VMEM((1, rows), F32), pltpu.VMEM((1, rows), F32),
                        pltpu.VMEM((MLA_KV_RANK, rows), F32)])
    return pl.pallas_call(
        kern, grid_spec=grid_spec,
        out_shape=jax.ShapeDtypeStruct((b, MLA_HEADS, t, MLA_KV_RANK), BF16),
        compiler_params=_cparams(("parallel", "arbitrary")), name="mla_flash",
    )(qi_arr, ki_arr, q_t, kcat, kt)


def _paged_kernel(pt_ref, q_ref, knew_ref, sel_ref, rowsel_ref, rowselt_ref, lat_hbm, rope_hbm, o_ref,
                  latbuf, ropebuf, sems, m_scr, l_scr, acc_scr, *, tpad, tv, layer, ngroups):
    bi = pl.program_id(0)
    nb = pl.num_programs(0)
    npg = PAGES_PER_GROUP
    rows = tpad * MLA_HEADS
    rows_v = tv * MLA_HEADS
    reps = MLA_KV_RANK // LANES

    def copies(seq, grp, slot):
        out = []
        for i in range(npg):
            page = pt_ref[seq, grp * npg + i]
            dst = pl.ds(i * PAGE_SIZE, PAGE_SIZE)
            out.append(pltpu.make_async_copy(lat_hbm.at[layer, page], latbuf.at[slot, dst], sems.at[0, slot, i]))
            out.append(pltpu.make_async_copy(rope_hbm.at[layer, page], ropebuf.at[slot, :, dst], sems.at[1, slot, i]))
        return out

    def start(seq, grp, slot):
        for cp in copies(seq, grp, slot):
            cp.start()

    def wait(seq, grp, slot):
        for cp in copies(seq, grp, slot):
            cp.wait()

    @pl.when(bi == 0)
    def _():
        for g in range(ngroups):
            start(0, g, g)

    for a in range(2):
        _softmax_init(m_scr.at[a], l_scr.at[a], acc_scr.at[a])
    q_v = jnp.dot(rowsel_ref[...], q_ref[0].reshape(rows, QK_WIDTH), preferred_element_type=F32).astype(BF16)
    q_lat = q_v[:, 0:MLA_KV_RANK]
    q_rd = jnp.dot(q_v[:, MLA_KV_RANK:QK_WIDTH], sel_ref[...], preferred_element_type=F32).astype(BF16)

    nt = (((1,), (1,)), ((), ()))

    def accumulate(a, s, v_b):
        p, alpha = _softmax_rows(s, m_scr.at[a], l_scr.at[a])
        yield
        acc_scr[a] = (acc_scr[a] * jnp.concatenate([alpha] * reps, axis=1)
                      + jnp.dot(p.astype(BF16), v_b, preferred_element_type=F32))

    def consume(a, slot):
        lat_b = latbuf[slot].astype(BF16)
        rope_b = ropebuf[slot].astype(BF16)
        yield
        s = (lax.dot_general(q_lat, lat_b, nt, preferred_element_type=F32)
             + jnp.dot(q_rd, rope_b, preferred_element_type=F32))
        yield
        yield from accumulate(a, s, lat_b)

    k_b = knew_ref[0]
    s2 = lax.dot_general(q_v, k_b, nt, preferred_element_type=F32)
    q_pos = lax.rem(lax.broadcasted_iota(jnp.int32, (rows_v, tpad), 0), tv)
    k_pos = lax.broadcasted_iota(jnp.int32, (rows_v, tpad), 1)
    _lockstep([accumulate(0, jnp.where(k_pos <= q_pos, s2, -jnp.inf), k_b[:, 0:MLA_KV_RANK])])

    for g0 in range(0, ngroups, 2):
        wait(bi, g0, g0)
        wait(bi, g0 + 1, g0 + 1)
        _lockstep([consume(0, g0), consume(1, g0 + 1)])

        @pl.when(bi + 1 < nb)
        def _():
            start(bi + 1, g0, g0)
            start(bi + 1, g0 + 1, g0 + 1)

    m0, m1 = m_scr[0], m_scr[1]
    m = jnp.maximum(m0, m1)
    a0, a1 = jnp.exp2(m0 - m), jnp.exp2(m1 - m)
    inv_l = 1.0 / (a0 * l_scr[0] + a1 * l_scr[1])
    wide = lambda v: jnp.concatenate([v] * reps, axis=1)
    o_v = ((acc_scr[0] * wide(a0) + acc_scr[1] * wide(a1)) * wide(inv_l)).astype(BF16)
    o = jnp.dot(rowselt_ref[...], o_v, preferred_element_type=F32)
    o_ref[0] = o.astype(o_ref.dtype).reshape(MLA_HEADS, tpad, MLA_KV_RANK)


def _row_select_matrix(tpad, tv):
    s = np.zeros((MLA_HEADS * tv, MLA_HEADS * tpad), np.float32)
    for h in range(MLA_HEADS):
        for t in range(tv):
            s[h * tv + t, h * tpad + t] = 1.0
    return s


def _paged_attention(q, knew, cache_lat, cache_rope_t, page_table, sel, *, tpad, tv, layer):
    b = q.shape[0]
    n_pages = page_table.shape[1]
    npg = PAGES_PER_GROUP
    ngroups = n_pages // npg
    assert n_pages % (2 * npg) == 0, "page groups are consumed in slot pairs"
    rows = tpad * MLA_HEADS
    rows_v = tv * MLA_HEADS
    gk = npg * PAGE_SIZE
    rowsel = _row_select_matrix(tpad, tv)
    kern = functools.partial(_paged_kernel, tpad=tpad, tv=tv, layer=layer, ngroups=ngroups)
    stat = pltpu.VMEM((2, rows_v, LANES), F32)
    grid_spec = pltpu.PrefetchScalarGridSpec(
        num_scalar_prefetch=1, grid=(b,),
        in_specs=[pl.BlockSpec((1, MLA_HEADS, tpad, QK_WIDTH), lambda bi, pt: (bi, 0, 0, 0)),
                  pl.BlockSpec((1, tpad, QK_WIDTH), lambda bi, pt: (bi, 0, 0)),
                  pl.BlockSpec(sel.shape, lambda bi, pt: (0, 0)),
                  pl.BlockSpec((rows_v, rows), lambda bi, pt: (0, 0)),
                  pl.BlockSpec((rows, rows_v), lambda bi, pt: (0, 0)),
                  pl.BlockSpec(memory_space=pl.ANY), pl.BlockSpec(memory_space=pl.ANY)],
        out_specs=pl.BlockSpec((1, MLA_HEADS, tpad, MLA_KV_RANK), lambda bi, pt: (bi, 0, 0, 0)),
        scratch_shapes=[pltpu.VMEM((ngroups, gk, MLA_KV_RANK), F32), pltpu.VMEM((ngroups, MLA_ROPE, gk), F32),
                        pltpu.SemaphoreType.DMA((2, ngroups, npg)),
                        stat, stat, pltpu.VMEM((2, rows_v, MLA_KV_RANK), F32)])
    return pl.pallas_call(
        kern, grid_spec=grid_spec,
        out_shape=jax.ShapeDtypeStruct((b, MLA_HEADS, tpad, MLA_KV_RANK), BF16),
        compiler_params=_cparams(("arbitrary",)), name="mla_paged",
    )(page_table, q, knew, sel, jnp.asarray(rowsel, BF16), jnp.asarray(rowsel.T, BF16), cache_lat, cache_rope_t)


def _out_kernel(x_ref, yssd_ref, olat_ref, gate_ref, ygdn_ref, wuv_ref, wout_ref, g_ref, b_ref, o_ref, *, tm):
    y_mla = []
    for pair in range(MLA_HEADS // 2):
        o_pair = None
        for i in range(2):
            o_h = olat_ref[:, 2 * pair + i].reshape(tm, MLA_KV_RANK)
            part = jnp.dot(o_h, wuv_ref[pair, i * MLA_KV_RANK:(i + 1) * MLA_KV_RANK, :], preferred_element_type=F32)
            o_pair = part if o_pair is None else o_pair + part
        y_mla.append(o_pair * gate_ref[:, pair * LANES:(pair + 1) * LANES])
    y_mla = jnp.concatenate(y_mla, axis=1)
    mm = (_bdot(yssd_ref[...], wout_ref[0:SSD_WIDTH, :])
          + _bdot(y_mla, wout_ref[SSD_WIDTH:SSD_WIDTH + MLA_WIDTH, :])
          + _bdot(ygdn_ref[...], wout_ref[SSD_WIDTH + MLA_WIDTH:MIX_WIDTH, :]))
    o_ref[...] = _layer_norm_rows(DEEPNORM_ALPHA * x_ref[...] + mm, g_ref[...], b_ref[...])


def _out_proj(x, y_ssd, o_lat, gate, y_gdn, l, wuv_pairs, wout, g, b, tm):
    m = x.shape[0]
    t = o_lat.shape[2]
    row = lambda i: (i, 0)
    nb, tt, hm_map = _head_major_block(t, tm)
    return pl.pallas_call(
        functools.partial(_out_kernel, tm=tm), grid=(m // tm,),
        in_specs=[pl.BlockSpec((tm, D_MODEL), row), pl.BlockSpec((tm, SSD_WIDTH), row),
                  pl.BlockSpec((nb, MLA_HEADS, tt, MLA_KV_RANK), hm_map), pl.BlockSpec((tm, MLA_WIDTH), row),
                  pl.BlockSpec((tm, GDN_WIDTH), row),
                  _layer_spec(wuv_pairs, l), _layer_spec(wout, l), _layer_spec(g, l), _layer_spec(b, l)],
        out_specs=pl.BlockSpec((tm, D_MODEL), row),
        out_shape=jax.ShapeDtypeStruct((m, D_MODEL), F32),
        compiler_params=_cparams(("parallel",)), name="out_proj",
    )(x, y_ssd, o_lat, gate, y_gdn, wuv_pairs, wout, g, b)


def _pad_last(v, n=LANES, before=0):
    return jnp.pad(v, ((0, 0),) * (v.ndim - 1) + ((before, n - before - v.shape[-1]),))


def _prep_weights(w_in, ssd_conv_w, ssd_conv_b, ssd_dt_bias, ssd_a_log, ssd_d, ssd_norm_w,
                  mla_q_norm_w, mla_w_uq, mla_kv_norm_w, mla_w_uk, mla_w_uv,
                  gdn_conv_w, gdn_dt_bias, gdn_a_log, gdn_norm_w, w_out, ln_g, ln_b):
    depth = w_in.shape[0]
    (w_z, w_xbc, w_dt, w_cq, w_ckv, w_kr, w_gate, w_qkv, w_gz, w_gb, w_ga) = jnp.split(w_in, IN_OFFSETS, axis=2)
    w_ssd = [w_z, w_xbc, _pad_last(w_dt)]
    kr_tiled = jnp.concatenate([jnp.tile(w_kr[..., :ROPE_HALF], (1, 1, MLA_HEADS)),
                                jnp.tile(w_kr[..., ROPE_HALF:], (1, 1, MLA_HEADS))], axis=2)
    w_mla = [w_cq, w_ckv, kr_tiled, w_gate]
    w_gdn = [w_qkv, w_gz, _pad_last(jnp.concatenate([w_gb, w_ga], axis=2))]

    uq = mla_w_uq.reshape(depth, MLA_Q_RANK, MLA_HEADS, MLA_NOPE + MLA_ROPE)
    wuq = jnp.concatenate([uq[..., :MLA_NOPE].reshape(depth, MLA_Q_RANK, -1),
                           uq[..., MLA_NOPE:MLA_NOPE + ROPE_HALF].reshape(depth, MLA_Q_RANK, -1),
                           uq[..., MLA_NOPE + ROPE_HALF:].reshape(depth, MLA_Q_RANK, -1)], axis=2).astype(BF16)
    hp = MLA_HEADS // 2
    uk = jnp.transpose(mla_w_uk, (0, 2, 3, 1)).reshape(depth, hp, 2, MLA_NOPE, MLA_KV_RANK)
    uv = jnp.transpose(mla_w_uv, (0, 2, 1, 3)).reshape(depth, hp, 2, MLA_KV_RANK, MLA_V)
    zk = jnp.zeros_like(uk[:, :, 0])
    zv = jnp.zeros_like(uv[:, :, 0])
    wuk_pairs = jnp.concatenate([jnp.concatenate([uk[:, :, 0], zk], axis=-1),
                                 jnp.concatenate([zk, uk[:, :, 1]], axis=-1)], axis=-2).astype(BF16)
    wuv_pairs = jnp.concatenate([jnp.concatenate([uv[:, :, 0], zv], axis=-1),
                                 jnp.concatenate([zv, uv[:, :, 1]], axis=-1)], axis=-2).astype(BF16)

    row = lambda v: v[:, None, :]
    return dict(
        w_in=jnp.concatenate(w_ssd + w_mla + w_gdn, axis=2).astype(BF16),
        ssd_cw=ssd_conv_w, ssd_cb=row(ssd_conv_b),
        ssd_dtb=_pad_last(row(ssd_dt_bias)), ssd_alog=_pad_last(row(ssd_a_log)),
        ssd_dexp=row(jnp.repeat(ssd_d, SSD_HEAD_DIM, axis=1)), ssd_nw=row(ssd_norm_w),
        qnw=row(mla_q_norm_w), kvnw=row(mla_kv_norm_w), wuq=wuq,
        wuk_pairs=wuk_pairs, wuv_pairs=wuv_pairs,
        gdn_cw=gdn_conv_w, gdn_bias=_pad_last(row(gdn_dt_bias), before=GDN_HEADS),
        gdn_alog=_pad_last(row(gdn_a_log), before=GDN_HEADS), gdn_nw=row(gdn_norm_w),
        w_out=w_out.astype(BF16), ln_g=row(ln_g), ln_b=row(ln_b))


def _rope_tables(pos):
    inv = ROPE_THETA ** (-jnp.arange(ROPE_HALF, dtype=F32) / ROPE_HALF)
    ang = pos.astype(F32)[:, None] * inv[None, :]
    return jnp.tile(jnp.cos(ang), (1, MLA_HEADS)), jnp.tile(jnp.sin(ang), (1, MLA_HEADS))


def _head_expand_matrix():
    e = np.zeros((LANES, SSD_WIDTH), np.float32)
    for h in range(SSD_HEADS):
        e[h, h * SSD_HEAD_DIM:(h + 1) * SSD_HEAD_DIM] = 1.0
    return jnp.asarray(e, BF16)


def _rope_select_matrix():
    s = np.zeros((2 * LANES, MLA_ROPE), np.float32)
    for j in range(2 * LANES):
        s[j, (j // LANES) * ROPE_HALF + (j % ROPE_HALF)] = 1.0
    return jnp.asarray(s, BF16)


def _tail8(conv_state):
    return jnp.pad(conv_state, ((0, 0), (0, 0), (SUBLANES - (CONV_WIDTH - 1), 0), (0, 0)))


def _trunk(x, pos, tv, ssd_conv, ssd_state, gdn_conv, gdn_state, emb_g, emb_b, w, cfg, paged=None):
    b, t, _ = x.shape
    m = b * t
    tm = cfg["tm"]
    cos_t, sin_t = _rope_tables(pos)
    if cos_t.shape[0] < tm:
        reps = tm // cos_t.shape[0]
        cos_t, sin_t = jnp.tile(cos_t, (reps, 1)), jnp.tile(sin_t, (reps, 1))
    emat = _head_expand_matrix()
    sel = _rope_select_matrix()
    ssd_c8_in, gdn_c8_in = _tail8(ssd_conv), _tail8(gdn_conv)
    gw = SSD_WIDTH // SSD_GROUPS
    ssd_h_in = ssd_state.reshape(DEPTH, b, SSD_GROUPS, gw, SSD_STATE)
    h = x.reshape(m, D_MODEL)
    new_states = []
    for l in range(DEPTH):
        if l == 0:
            h, p_ssd, p_dt, p_mla, p_gdn, p_ba = _project(h, w["w_in"], l, cfg["tm_proj"], emb_ln=(emb_g, emb_b))
        else:
            p_ssd, p_dt, p_mla, p_gdn, p_ba = _project(h, w["w_in"], l, cfg["tm_proj"])

        y_ssd, ssd_c8, ssd_h = _ssd_scan(
            p_ssd.reshape(b, t, -1), p_dt.reshape(b, t, LANES), ssd_c8_in, ssd_h_in, l,
            w["ssd_cw"], w["ssd_cb"], w["ssd_dtb"], w["ssd_alog"], w["ssd_dexp"], w["ssd_nw"], emat,
            L=cfg["ssd_chunk"], tv=min(tv, cfg["ssd_chunk"]))
        y_gdn, gdn_c8, gdn_s = _gdn_scan(
            p_gdn.reshape(b, t, -1), p_ba.reshape(b, t, LANES), gdn_c8_in, gdn_state, l,
            w["gdn_cw"], w["gdn_bias"], w["gdn_alog"], w["gdn_nw"],
            L=cfg["gdn_chunk"], tv=min(tv, cfg["gdn_chunk"]))

        q, kcat, kt, ckv, kr_t, gate = _mla_prep(p_mla, cos_t, sin_t, l, w["qnw"], w["kvnw"], w["wuq"],
                                                 w["wuk_pairs"], b, t, tm, tq_t=cfg["tq"] if paged is None else None)
        kcat = kcat.reshape(b, t, QK_WIDTH)
        if paged is None:
            o_lat = _flash_attention(q, kcat, kt, tq=cfg["tq"], tk=cfg["tk"])
        else:
            cache_lat, cache_rope_t, page_table = paged
            o_lat = _paged_attention(q, kcat, cache_lat, cache_rope_t, page_table, sel, tpad=t, tv=tv, layer=l)

        h = _out_proj(h, y_ssd.reshape(m, SSD_WIDTH), o_lat, gate, y_gdn.reshape(m, GDN_WIDTH), l,
                      w["wuv_pairs"], w["w_out"], w["ln_g"], w["ln_b"], tm)

        kr = jnp.concatenate([kr_t[:, 0:ROPE_HALF], kr_t[:, LANES:LANES + ROPE_HALF]], axis=1)
        sl = slice(SUBLANES - (CONV_WIDTH - 1), SUBLANES)
        new_states.append((ckv.reshape(b, t, MLA_KV_RANK), kr.reshape(b, t, MLA_ROPE), ssd_c8[:, sl],
                           ssd_h.reshape(b, SSD_HEADS, SSD_HEAD_DIM, SSD_STATE), gdn_c8[:, sl], gdn_s))
    return h.reshape(b, t, D_MODEL), tuple(jnp.stack(s) for s in zip(*new_states))


def kernel(x_prompt, x_sample, cache_kv_latent, cache_k_rope, state_ssd_conv, state_ssd, state_gdn_conv, state_gdn, page_table, emb_ln_g, emb_ln_b, w_in, ssd_conv_w, ssd_conv_b, ssd_dt_bias, ssd_a_log, ssd_d, ssd_norm_w, mla_q_norm_w, mla_w_uq, mla_kv_norm_w, mla_w_uk, mla_w_uv, gdn_conv_w, gdn_dt_bias, gdn_a_log, gdn_norm_w, w_out, ln_g, ln_b):
    weights = _prep_weights(w_in, ssd_conv_w, ssd_conv_b, ssd_dt_bias, ssd_a_log, ssd_d, ssd_norm_w,
                            mla_q_norm_w, mla_w_uq, mla_kv_norm_w, mla_w_uk, mla_w_uv,
                            gdn_conv_w, gdn_dt_bias, gdn_a_log, gdn_norm_w, w_out, ln_g, ln_b)

    bp, tp, _ = x_prompt.shape
    zeros = lambda *s: jnp.zeros(s, F32)
    ssd_chunk = SSD_CHUNK if tp % SSD_CHUNK == 0 else tp
    gdn_chunk = GDN_CHUNK if tp % GDN_CHUNK == 0 else tp
    cfg_p = dict(tm=min(512, bp * tp), tm_proj=min(512, bp * tp), ssd_chunk=ssd_chunk, gdn_chunk=gdn_chunk,
                 tq=min(256, tp), tk=min(512, tp))
    y_prompt, st_p = _trunk(
        x_prompt, jnp.arange(tp), tp,
        zeros(DEPTH, bp, CONV_WIDTH - 1, SSD_CONV_DIM), zeros(DEPTH, bp, SSD_HEADS, SSD_HEAD_DIM, SSD_STATE),
        zeros(DEPTH, bp, CONV_WIDTH - 1, GDN_CONV_DIM), zeros(DEPTH, bp, GDN_HEADS, GDN_HEAD_DIM, GDN_HEAD_DIM),
        emb_ln_g, emb_ln_b, weights, cfg_p)

    bs, ts, _ = x_sample.shape
    tpad = -(-ts // BF16_ROWS) * BF16_ROWS
    past_len = page_table.shape[1] * PAGE_SIZE
    xs = jnp.pad(x_sample, ((0, 0), (0, tpad - ts), (0, 0)))
    cfg_s = dict(tm=min(512, bs * tpad), tm_proj=min(512, bs * tpad), ssd_chunk=tpad, gdn_chunk=tpad,
                 tq=tpad, tk=tpad)
    y_s, st_s = _trunk(
        xs, past_len + jnp.arange(tpad), ts, state_ssd_conv, state_ssd, state_gdn_conv, state_gdn,
        emb_ln_g, emb_ln_b, weights, cfg_s,
        paged=(cache_kv_latent, jnp.swapaxes(cache_k_rope, 2, 3), page_table))
    y_sample = y_s[:, :ts]
    s_lat, s_rope, s_ssd_conv, s_ssd, s_gdn_conv, s_gdn = st_s
    return (y_prompt, y_sample) + tuple(st_p) + (s_lat[:, :, :ts], s_rope[:, :, :ts], s_ssd_conv, s_ssd, s_gdn_conv, s_gdn)
```

```python
import functools
import math

import jax
import jax.numpy as jnp
import numpy as np
from jax import lax
from jax.experimental import pallas as pl
from jax.experimental.pallas import tpu as pltpu

F32 = jnp.float32
BF16 = jnp.bfloat16

D_MODEL = 1024
CONV_WIDTH = 4
SSD_HEADS = 16
SSD_HEAD_DIM = 64
SSD_WIDTH = SSD_HEADS * SSD_HEAD_DIM
SSD_GROUPS = 2
SSD_STATE = 128
SSD_CONV_DIM = SSD_WIDTH + 2 * SSD_GROUPS * SSD_STATE
SSD_CHUNK = 128
MLA_HEADS = 8
MLA_NOPE = 64
MLA_ROPE = 32
MLA_V = 64
MLA_WIDTH = MLA_HEADS * MLA_V
MLA_Q_RANK = 384
MLA_KV_RANK = 256
MLA_SCALE = (MLA_NOPE + MLA_ROPE) ** -0.5
ROPE_THETA = 10000.0
GDN_HEADS = 4
GDN_HEAD_DIM = 128
GDN_WIDTH = GDN_HEADS * GDN_HEAD_DIM
GDN_CONV_DIM = 3 * GDN_WIDTH
GDN_CHUNK = 64
MIX_WIDTH = SSD_WIDTH + MLA_WIDTH + GDN_WIDTH
IN_SIZES = (SSD_WIDTH, SSD_CONV_DIM, SSD_HEADS, MLA_Q_RANK, MLA_KV_RANK, MLA_ROPE, MLA_WIDTH,
            GDN_CONV_DIM, GDN_WIDTH, GDN_HEADS, GDN_HEADS)
IN_OFFSETS = tuple(int(o) for o in np.cumsum(IN_SIZES)[:-1])
DEPTH = 2
DEEPNORM_ALPHA = (2 * DEPTH) ** 0.25
LN_EPS = 1e-5
RMS_EPS = 1e-6
L2_EPS = 1e-6
PAGE_SIZE = 128

LANES = 128
SUBLANES = 8
BF16_ROWS = 16
ROPE_HALF = MLA_ROPE // 2
P_SSD = SSD_WIDTH + SSD_CONV_DIM + LANES
P_MLA = MLA_Q_RANK + MLA_KV_RANK + 2 * LANES + MLA_WIDTH
P_GDN = GDN_CONV_DIM + GDN_WIDTH + LANES
QK_WIDTH = MLA_KV_RANK + 2 * LANES
Q_SCALE = MLA_SCALE * math.log2(math.e)
VMEM_LIMIT = 56 * 1024 * 1024
PAGES_PER_GROUP = 32
GDN_BATCH_PER_STEP = 4
SSD_BATCH_PER_STEP = 4

def _cparams(sem):
    return pltpu.CompilerParams(dimension_semantics=sem, vmem_limit_bytes=VMEM_LIMIT)


def _bdot(a, b):
    return jnp.dot(a.astype(BF16), b.astype(BF16), preferred_element_type=F32)


def _bdot_nt(a, b):
    return lax.dot_general(a.astype(BF16), b.astype(BF16), (((1,), (1,)), ((), ())),
                           preferred_element_type=F32)


def _fdot(a, b):
    return jnp.dot(a, b, precision=lax.Precision.HIGHEST, preferred_element_type=F32)


def _silu(x):
    return x * (1.0 / (1.0 + jnp.exp(-x)))


def _softplus(x):
    return jnp.maximum(x, 0.0) + jnp.log1p(jnp.exp(-jnp.abs(x)))


def _const_spec(shape):
    nd = len(shape)
    return pl.BlockSpec(shape, lambda *_: (0,) * nd)


def _layer_spec(stacked, l):
    rest = stacked.shape[1:]
    return pl.BlockSpec((None,) + rest, lambda *_: (l,) + (0,) * len(rest))


def _layer_norm_rows(x, g, b):
    mu = jnp.mean(x, -1, keepdims=True)
    xc = x - mu
    var = jnp.mean(xc * xc, -1, keepdims=True)
    return xc * lax.rsqrt(var + LN_EPS) * g + b


def _proj_kernel(x_ref, w_ref, *refs, embed):
    if embed:
        g_ref, b_ref, h_ref, ssd_ref, dt_ref, mla_ref, gdn_ref, ba_ref = refs
        h = _layer_norm_rows(x_ref[...], g_ref[...], b_ref[...])
        h_ref[...] = h
        x = h.astype(BF16)
    else:
        ssd_ref, dt_ref, mla_ref, gdn_ref, ba_ref = refs
        x = x_ref[...].astype(BF16)
    p = jnp.dot(x, w_ref[:, 0:P_SSD], preferred_element_type=F32)
    ssd_ref[...] = p[:, 0:P_SSD - LANES].astype(BF16)
    dt_ref[...] = p[:, P_SSD - LANES:P_SSD]
    mla_ref[...] = jnp.dot(x, w_ref[:, P_SSD:P_SSD + P_MLA], preferred_element_type=F32).astype(BF16)
    p = jnp.dot(x, w_ref[:, P_SSD + P_MLA:P_SSD + P_MLA + P_GDN], preferred_element_type=F32)
    gdn_ref[...] = p[:, 0:P_GDN - LANES].astype(BF16)
    ba_ref[...] = p[:, P_GDN - LANES:P_GDN]


def _project(x, w, l, tm, emb_ln=None):
    m, k = x.shape
    row = lambda i: (i, 0)
    widths = (P_SSD - LANES, LANES, P_MLA, P_GDN - LANES, LANES)
    dtypes = (BF16, F32, BF16, BF16, F32)
    operands, in_specs = [x, w], [pl.BlockSpec((tm, k), row), _layer_spec(w, l)]
    if emb_ln is not None:
        operands += [v.reshape(1, k) for v in emb_ln]
        in_specs += [_const_spec((1, k)), _const_spec((1, k))]
        widths, dtypes = (k,) + widths, (F32,) + dtypes
    return pl.pallas_call(
        functools.partial(_proj_kernel, embed=emb_ln is not None), grid=(m // tm,),
        in_specs=in_specs,
        out_specs=[pl.BlockSpec((tm, n), row) for n in widths],
        out_shape=[jax.ShapeDtypeStruct((m, n), d) for n, d in zip(widths, dtypes)],
        compiler_params=_cparams(("parallel",)), name="in_proj")(*operands)


def _conv_chunk(xbuf, x_new, cw_ref, L):
    xbuf[SUBLANES:SUBLANES + L, :] = x_new
    y = cw_ref[CONV_WIDTH - 1:CONV_WIDTH, :] * x_new
    for k in range(CONV_WIDTH - 1):
        off = SUBLANES - (CONV_WIDTH - 1) + k
        y = y + cw_ref[k:k + 1, :] * xbuf[off:off + L, :]
    return y


def _ssd_chunk(p_ref, dt_ref, cw_ref, cb_ref, dtb_ref, alog_ref, dexp_ref, nw_ref, e_ref,
               y_ref, cnew_ref, xbuf, ht, *, L, tv):
    z = p_ref[:, 0:SSD_WIDTH].astype(F32)
    xbc_raw = p_ref[:, SSD_WIDTH:SSD_WIDTH + SSD_CONV_DIM].astype(F32)
    dt_raw = dt_ref[...]

    xbc = _silu(_conv_chunk(xbuf, xbc_raw, cw_ref, L) + cb_ref[...])
    cnew_ref[...] = xbuf[tv:tv + SUBLANES, :]
    xbuf[0:SUBLANES, :] = xbuf[L:L + SUBLANES, :]
    yield

    xs = xbc[:, 0:SSD_WIDTH]
    gs = SSD_GROUPS * SSD_STATE
    bm = xbc[:, SSD_WIDTH:SSD_WIDTH + gs]
    cm = xbc[:, SSD_WIDTH + gs:SSD_WIDTH + 2 * gs]

    dt = _softplus(dt_raw + dtb_ref[...])
    if tv < L:
        rows = lax.broadcasted_iota(jnp.int32, (L, LANES), 0)
        dt = jnp.where(rows < tv, dt, 0.0)
    a = -jnp.exp(alog_ref[...])
    da = dt * a
    r_i = lax.broadcasted_iota(jnp.int32, (L, L), 0)
    c_i = lax.broadcasted_iota(jnp.int32, (L, L), 1)
    causal = c_i <= r_i
    tri = jnp.where(causal, 1.0, 0.0).astype(F32)
    acum = _fdot(tri, da)
    last = acum[L - 1:L, :]
    ea = jnp.exp(acum)
    wdec = jnp.exp(last - acum)

    def pad_rows(v):
        if L == LANES:
            return v
        return jnp.concatenate([v, jnp.zeros((LANES - L, v.shape[1]), v.dtype)], axis=0)

    acum_t = jnp.transpose(pad_rows(acum))
    yield

    def hilo(v):
        hi = v.astype(BF16)
        lo = (v - hi.astype(F32)).astype(BF16)
        return hi, lo

    parts = []
    for v in (dt, ea, wdec):
        parts.extend(hilo(v))
    stacked = jnp.concatenate(parts, axis=0)
    expanded = jnp.dot(stacked, e_ref[...], preferred_element_type=F32)
    dt_e = expanded[0:L] + expanded[L:2 * L]
    ea_e = expanded[2 * L:3 * L] + expanded[3 * L:4 * L]
    wd_e = expanded[4 * L:5 * L] + expanded[5 * L:6 * L]

    yield
    xdt = xs * dt_e
    xdt_b = xdt.astype(BF16)
    lane = lax.broadcasted_iota(jnp.int32, (L, LANES), 1)
    hpg = SSD_HEADS // SSD_GROUPS

    y_parts = []
    for g in range(SSD_GROUPS):
        bg = bm[:, g * SSD_STATE:(g + 1) * SSD_STATE]
        cg = cm[:, g * SSD_STATE:(g + 1) * SSD_STATE]
        cb = _bdot_nt(cg, bg)
        for j in range(hpg // 2):
            pair = g * (hpg // 2) + j
            xp = xdt_b[:, pair * LANES:(pair + 1) * LANES]
            ys = []
            for h in (2 * pair, 2 * pair + 1):
                seg = acum[:, h:h + 1] - acum_t[h:h + 1, 0:L]
                dec = jnp.exp(jnp.where(causal, seg, -jnp.inf))
                ys.append(_bdot(cb * dec, xp))
            y_parts.append(jnp.where(lane < SSD_HEAD_DIM, ys[0], ys[1]))
            yield
    y_in = jnp.concatenate(y_parts, axis=1)

    gw = hpg * SSD_HEAD_DIM
    y_st_parts = []
    for g in range(SSD_GROUPS):
        cg = cm[:, g * SSD_STATE:(g + 1) * SSD_STATE]
        y_st_parts.append(_bdot(cg, ht[g]))
    y_st = jnp.concatenate(y_st_parts, axis=1) * ea_e
    yield

    xw = (xdt * wd_e)
    for g in range(SSD_GROUPS):
        bg_t = jnp.transpose(pad_rows(bm[:, g * SSD_STATE:(g + 1) * SSD_STATE]))
        xw_g = pad_rows(xw[:, g * gw:(g + 1) * gw])
        ht[g] = ht[g] * ea_e[L - 1:L, g * gw:(g + 1) * gw] + _bdot(bg_t, xw_g)
    yield

    y = (y_in + y_st + dexp_ref[...] * xs) * _silu(z)
    outs = []
    for g in range(SSD_GROUPS):
        yg = y[:, g * gw:(g + 1) * gw]
        ms = jnp.mean(yg * yg, -1, keepdims=True)
        outs.append(yg * lax.rsqrt(ms + RMS_EPS) * nw_ref[:, g * gw:(g + 1) * gw])
    y_ref[...] = jnp.concatenate(outs, axis=1).astype(y_ref.dtype)


def _lockstep(gens):
    while gens:
        gens = [g for g in gens if next(g, StopIteration) is not StopIteration]


def _ssd_kernel(p_ref, dt_ref, cprev_ref, h0_ref, cw_ref, cb_ref, dtb_ref, alog_ref, dexp_ref, nw_ref, e_ref,
                y_ref, cnew_ref, hout_ref, xbuf, ht, *, L, tv, nbat):
    c = pl.program_id(1)
    nc = pl.num_programs(1)

    @pl.when(c == 0)
    def _():
        xbuf[:, 0:SUBLANES, :] = cprev_ref[...]
        for i in range(nbat):
            for g in range(SSD_GROUPS):
                ht[i, g] = jnp.transpose(h0_ref[i, g])

    _lockstep([_ssd_chunk(p_ref.at[i], dt_ref.at[i], cw_ref, cb_ref, dtb_ref, alog_ref, dexp_ref, nw_ref, e_ref,
                          y_ref.at[i], cnew_ref.at[i], xbuf.at[i], ht.at[i], L=L, tv=tv) for i in range(nbat)])

    @pl.when(c == nc - 1)
    def _():
        for i in range(nbat):
            for g in range(SSD_GROUPS):
                hout_ref[i, g] = jnp.transpose(ht[i, g])


def _ssd_scan(p_ssd, p_dt, conv_prev8, h0, l, cw, cb, dtb, alog, dexp, nw, emat, *, L, tv):
    b, t, _ = p_ssd.shape
    nc = t // L
    gw = SSD_WIDTH // SSD_GROUPS
    nbat = math.gcd(b, SSD_BATCH_PER_STEP)
    kern = functools.partial(_ssd_kernel, L=L, tv=tv, nbat=nbat)
    hshape = (nbat, SSD_GROUPS, gw, SSD_STATE)
    return pl.pallas_call(
        kern, grid=(b // nbat, nc),
        in_specs=[pl.BlockSpec((nbat, L, P_SSD - LANES), lambda i, c: (i, c, 0)),
                  pl.BlockSpec((nbat, L, LANES), lambda i, c: (i, c, 0)),
                  pl.BlockSpec((None, nbat, SUBLANES, SSD_CONV_DIM), lambda i, c: (l, i, 0, 0)),
                  pl.BlockSpec((None,) + hshape, lambda i, c: (l, i, 0, 0, 0)),
                  _layer_spec(cw, l), _layer_spec(cb, l), _layer_spec(dtb, l), _layer_spec(alog, l),
                  _layer_spec(dexp, l), _layer_spec(nw, l), _const_spec((LANES, SSD_WIDTH))],
        out_specs=[pl.BlockSpec((nbat, L, SSD_WIDTH), lambda i, c: (i, c, 0)),
                   pl.BlockSpec((nbat, SUBLANES, SSD_CONV_DIM), lambda i, c: (i, 0, 0)),
                   pl.BlockSpec(hshape, lambda i, c: (i, 0, 0, 0))],
        out_shape=[jax.ShapeDtypeStruct((b, t, SSD_WIDTH), BF16),
                   jax.ShapeDtypeStruct((b, SUBLANES, SSD_CONV_DIM), F32),
                   jax.ShapeDtypeStruct((b,) + hshape[1:], F32)],
        scratch_shapes=[pltpu.VMEM((nbat, L + SUBLANES, SSD_CONV_DIM), F32),
                        pltpu.VMEM((nbat, SSD_GROUPS, SSD_STATE, gw), F32)],
        compiler_params=_cparams(("parallel", "arbitrary")), name="ssd_scan",
    )(p_ssd, p_dt, conv_prev8, h0, cw, cb, dtb, alog, dexp, nw, emat)


def _gdn_chunk(p_ref, ba_ref, cw_ref, bias_ref, alog_ref, nw_ref, y_ref, cnew_ref, xbuf, st, *, L, tv):
    H = GDN_HEADS
    D = GDN_HEAD_DIM
    R = H * L
    S = max(R, LANES)

    qkv_raw = p_ref[:, 0:GDN_CONV_DIM].astype(F32)
    z = p_ref[:, GDN_CONV_DIM:GDN_CONV_DIM + GDN_WIDTH].astype(F32)
    ba = ba_ref[...]

    qkv = _silu(_conv_chunk(xbuf, qkv_raw, cw_ref, L))
    cnew_ref[...] = xbuf[tv:tv + SUBLANES, :]
    xbuf[0:SUBLANES, :] = xbuf[L:L + SUBLANES, :]
    yield

    beta_f = 1.0 / (1.0 + jnp.exp(-ba))
    g_f = -jnp.exp(alog_ref[...]) * _softplus(ba + bias_ref[...])
    if tv < L:
        rows = lax.broadcasted_iota(jnp.int32, (L, LANES), 0)
        beta_f = jnp.where(rows < tv, beta_f, 0.0)
        g_f = jnp.where(rows < tv, g_f, 0.0)
    r_i = lax.broadcasted_iota(jnp.int32, (L, L), 0)
    c_i = lax.broadcasted_iota(jnp.int32, (L, L), 1)
    tri = jnp.where(c_i <= r_i, 1.0, 0.0).astype(F32)
    gcum_f = _fdot(tri, g_f)
    glast_f = jnp.broadcast_to(gcum_f[L - 1:L, :], (L, LANES))

    def pad_s(v):
        if R == S:
            return v
        return jnp.concatenate([v, jnp.zeros((S - R, v.shape[1]), v.dtype)], axis=0)

    def stack(v):
        return pad_s(jnp.concatenate([v[:, h * D:(h + 1) * D] for h in range(H)], axis=0))

    def col(v, off):
        return pad_s(jnp.concatenate([v[:, off + h:off + h + 1] for h in range(H)], axis=0))

    q_s = stack(qkv[:, 0:GDN_WIDTH])
    k_s = stack(qkv[:, GDN_WIDTH:2 * GDN_WIDTH])
    v_s = stack(qkv[:, 2 * GDN_WIDTH:3 * GDN_WIDTH])
    z_s = stack(z)
    q_s = q_s * lax.rsqrt(jnp.sum(q_s * q_s, -1, keepdims=True) + L2_EPS) * (D ** -0.5)
    k_s = k_s * lax.rsqrt(jnp.sum(k_s * k_s, -1, keepdims=True) + L2_EPS)
    beta = col(beta_f, 0)
    gcum = col(gcum_f, H)
    glast = col(glast_f, H)
    yield

    cmat = jnp.broadcast_to(gcum, (S, S))
    diff = cmat - jnp.transpose(cmat)
    rs = lax.broadcasted_iota(jnp.int32, (S, S), 0)
    cs = lax.broadcasted_iota(jnp.int32, (S, S), 1)
    if L & (L - 1) == 0:
        sh = L.bit_length() - 1
        same = (rs >> sh) == (cs >> sh)
    else:
        same = (rs // L) == (cs // L)
    incl = same & (cs <= rs)
    strict = same & (cs < rs)
    dec = jnp.exp(jnp.where(incl, diff, -jnp.inf))

    kb = k_s * beta
    n_mat = -jnp.where(strict, _bdot_nt(kb, k_s) * dec, 0.0)
    eye = jnp.where(rs == cs, 1.0, 0.0).astype(F32)
    t_mat = eye + n_mat
    npow = n_mat
    yield
    span = 2
    while span < L:
        npow = _bdot(npow, npow)
        yield
        t_mat = t_mat + _bdot(t_mat, npow)
        yield
        span *= 2

    eg = jnp.exp(gcum)
    t_b = t_mat.astype(BF16)
    u = _bdot(t_b, v_s * beta)
    w = _bdot(t_b, kb * eg)
    qg = q_s * eg
    attn = jnp.where(incl, _bdot_nt(q_s, k_s) * dec, 0.0)
    yield
    ws = []
    qs_ = []
    for h in range(H):
        sh_b = st[h].astype(BF16)
        ws.append(_bdot(w[h * L:(h + 1) * L], sh_b))
        qs_.append(_bdot(qg[h * L:(h + 1) * L], sh_b))
    v_new = u - pad_s(jnp.concatenate(ws, axis=0))
    yield
    o = pad_s(jnp.concatenate(qs_, axis=0)) + _bdot(attn, v_new)

    kd = k_s * jnp.exp(glast - gcum)
    kd_t = jnp.transpose(kd).astype(BF16)
    row_head = lax.broadcasted_iota(jnp.int32, (S, D), 0)
    eg_last = jnp.exp(glast)
    for h in range(H):
        vm = jnp.where((row_head >= h * L) & (row_head < (h + 1) * L), v_new, 0.0)
        st[h] = st[h] * eg_last[h * L:h * L + 1, :] + _bdot(kd_t, vm)
    yield

    ms = jnp.mean(o * o, -1, keepdims=True)
    o = o * lax.rsqrt(ms + RMS_EPS) * nw_ref[...] * _silu(z_s)
    y_ref[...] = jnp.concatenate([o[h * L:(h + 1) * L] for h in range(H)], axis=1).astype(y_ref.dtype)


def _gdn_kernel(p_ref, ba_ref, cprev_ref, s0_ref, cw_ref, bias_ref, alog_ref, nw_ref,
                y_ref, cnew_ref, sout_ref, xbuf, st, *, L, tv, nbat):
    c = pl.program_id(1)
    nc = pl.num_programs(1)

    @pl.when(c == 0)
    def _():
        xbuf[:, 0:SUBLANES, :] = cprev_ref[...]
        st[...] = s0_ref[...]

    _lockstep([_gdn_chunk(p_ref.at[i], ba_ref.at[i], cw_ref, bias_ref, alog_ref, nw_ref, y_ref.at[i],
                          cnew_ref.at[i], xbuf.at[i], st.at[i], L=L, tv=tv) for i in range(nbat)])

    @pl.when(c == nc - 1)
    def _():
        sout_ref[...] = st[...]


def _gdn_scan(p_gdn, p_ba, conv_prev8, s0, l, cw, bias, alog, nw, *, L, tv):
    b, t, _ = p_gdn.shape
    nc = t // L
    nbat = math.gcd(b, GDN_BATCH_PER_STEP)
    kern = functools.partial(_gdn_kernel, L=L, tv=tv, nbat=nbat)
    sshape = (nbat, GDN_HEADS, GDN_HEAD_DIM, GDN_HEAD_DIM)
    return pl.pallas_call(
        kern, grid=(b // nbat, nc),
        in_specs=[pl.BlockSpec((nbat, L, P_GDN - LANES), lambda i, c: (i, c, 0)),
                  pl.BlockSpec((nbat, L, LANES), lambda i, c: (i, c, 0)),
                  pl.BlockSpec((None, nbat, SUBLANES, GDN_CONV_DIM), lambda i, c: (l, i, 0, 0)),
                  pl.BlockSpec((None,) + sshape, lambda i, c: (l, i, 0, 0, 0)),
                  _layer_spec(cw, l), _layer_spec(bias, l), _layer_spec(alog, l), _layer_spec(nw, l)],
        out_specs=[pl.BlockSpec((nbat, L, GDN_WIDTH), lambda i, c: (i, c, 0)),
                   pl.BlockSpec((nbat, SUBLANES, GDN_CONV_DIM), lambda i, c: (i, 0, 0)),
                   pl.BlockSpec(sshape, lambda i, c: (i, 0, 0, 0))],
        out_shape=[jax.ShapeDtypeStruct((b, t, GDN_WIDTH), BF16),
                   jax.ShapeDtypeStruct((b, SUBLANES, GDN_CONV_DIM), F32),
                   jax.ShapeDtypeStruct((b,) + sshape[1:], F32)],
        scratch_shapes=[pltpu.VMEM((nbat, L + SUBLANES, GDN_CONV_DIM), F32),
                        pltpu.VMEM(sshape, F32)],
        compiler_params=_cparams(("parallel", "arbitrary")), name="gdn_scan",
    )(p_gdn, p_ba, conv_prev8, s0, cw, bias, alog, nw)


def _mla_prep_kernel(p_ref, cos_ref, sin_ref, qnw_ref, kvnw_ref, wuq_ref, wuk_ref,
                     q_ref, kcat_ref, kt_ref, ckv_ref, kr_ref, gate_ref, *, nb, tt, tq_t):
    cq = p_ref[:, 0:MLA_Q_RANK].astype(F32)
    ckv = p_ref[:, MLA_Q_RANK:MLA_Q_RANK + MLA_KV_RANK].astype(F32)
    kr = p_ref[:, MLA_Q_RANK + MLA_KV_RANK:MLA_Q_RANK + MLA_KV_RANK + 2 * LANES].astype(F32)
    gate = p_ref[:, MLA_Q_RANK + MLA_KV_RANK + 2 * LANES:P_MLA].astype(F32)
    cos = cos_ref[...]
    sin = sin_ref[...]

    cqn = cq * lax.rsqrt(jnp.mean(cq * cq, -1, keepdims=True) + RMS_EPS) * qnw_ref[...]
    q = _bdot(cqn, wuq_ref[...])
    nope_w = MLA_HEADS * MLA_NOPE
    x1 = q[:, nope_w:nope_w + LANES]
    x2 = q[:, nope_w + LANES:nope_w + 2 * LANES]
    r_all = jnp.concatenate([x1 * cos - x2 * sin, x2 * cos + x1 * sin], axis=1) * Q_SCALE

    ckvn = ckv * lax.rsqrt(jnp.mean(ckv * ckv, -1, keepdims=True) + RMS_EPS) * kvnw_ref[...]
    k1 = kr[:, 0:LANES]
    k2 = kr[:, LANES:2 * LANES]
    kr_rot = jnp.concatenate([k1 * cos - k2 * sin, k2 * cos + k1 * sin], axis=1)
    ckv_ref[...] = ckvn
    kr_ref[...] = kr_rot
    kcat_ref[...] = jnp.concatenate([ckvn, kr_rot], axis=1).astype(BF16)
    kt_ref[...] = jnp.transpose(ckvn).astype(BF16)
    gate_ref[...] = _silu(gate).astype(gate_ref.dtype)

    tm = q.shape[0]
    if tq_t is None:
        lane = lax.broadcasted_iota(jnp.int32, (1, 2 * LANES), 1)
        head_of_lane = (lane & (LANES - 1)) >> 4
    else:
        r_all_t = jnp.transpose(r_all)
        row_i = lax.broadcasted_iota(jnp.int32, (2 * LANES, 1), 0)
        head_of_row = (row_i & (LANES - 1)) >> 4
    for pair in range(MLA_HEADS // 2):
        qn = q[:, pair * LANES:(pair + 1) * LANES] * Q_SCALE
        qlat = _bdot(qn, wuk_ref[pair])
        for i in range(2):
            h = 2 * pair + i
            ql = qlat[:, i * MLA_KV_RANK:(i + 1) * MLA_KV_RANK]
            if tq_t is None:
                qr = jnp.where(head_of_lane == h, r_all, 0.0).astype(BF16)
                q_ref[:, h, :, 0:MLA_KV_RANK] = ql.astype(BF16).reshape(nb, tt, MLA_KV_RANK)
                q_ref[:, h, :, MLA_KV_RANK:QK_WIDTH] = qr.reshape(nb, tt, 2 * LANES)
            else:
                ql_t = jnp.transpose(ql).astype(BF16)
                qr_t = jnp.where(head_of_row == h, r_all_t, 0.0).astype(BF16)
                for qb in range(tm // tq_t):
                    cols = slice(h * tq_t, (h + 1) * tq_t)
                    toks = slice(qb * tq_t, (qb + 1) * tq_t)
                    q_ref[0, qb, 0:MLA_KV_RANK, cols] = ql_t[:, toks]
                    q_ref[0, qb, MLA_KV_RANK:QK_WIDTH, cols] = qr_t[:, toks]


def _head_major_block(t, tm):
    if tm <= t:
        npb = t // tm
        return 1, tm, (lambda i: (i // npb, 0, i % npb, 0))
    return tm // t, t, (lambda i: (i, 0, 0, 0))


def _mla_prep(p_mla, cos_t, sin_t, l, qnw, kvnw, wuq, wuk_pairs, b, t, tm, tq_t=None):
    m = p_mla.shape[0]
    nt = cos_t.shape[0] // tm
    row = lambda i: (i, 0)
    tab = lambda i: (i % nt, 0)
    nb, tt, hm_map = _head_major_block(t, tm)
    kern = functools.partial(_mla_prep_kernel, nb=nb, tt=tt, tq_t=tq_t)
    if tq_t is None:
        q_spec = pl.BlockSpec((nb, MLA_HEADS, tt, QK_WIDTH), hm_map)
        q_shape = jax.ShapeDtypeStruct((b, MLA_HEADS, t, QK_WIDTH), BF16)
    else:
        npb = t // tm
        q_spec = pl.BlockSpec((1, tm // tq_t, QK_WIDTH, MLA_HEADS * tq_t), lambda i: (i // npb, i % npb, 0, 0))
        q_shape = jax.ShapeDtypeStruct((b, t // tq_t, QK_WIDTH, MLA_HEADS * tq_t), BF16)
    return pl.pallas_call(
        kern, grid=(m // tm,),
        in_specs=[pl.BlockSpec((tm, P_MLA), row), pl.BlockSpec((tm, LANES), tab), pl.BlockSpec((tm, LANES), tab),
                  _layer_spec(qnw, l), _layer_spec(kvnw, l), _layer_spec(wuq, l), _layer_spec(wuk_pairs, l)],
        out_specs=[q_spec, pl.BlockSpec((tm, QK_WIDTH), row),
                   pl.BlockSpec((MLA_KV_RANK, tm), lambda i: (0, i)),
                   pl.BlockSpec((tm, MLA_KV_RANK), row), pl.BlockSpec((tm, 2 * LANES), row),
                   pl.BlockSpec((tm, MLA_WIDTH), row)],
        out_shape=[q_shape,
                   jax.ShapeDtypeStruct((m, QK_WIDTH), BF16), jax.ShapeDtypeStruct((MLA_KV_RANK, m), BF16),
                   jax.ShapeDtypeStruct((m, MLA_KV_RANK), F32), jax.ShapeDtypeStruct((m, 2 * LANES), F32),
                   jax.ShapeDtypeStruct((m, MLA_WIDTH), BF16)],
        compiler_params=_cparams(("parallel",)), name="mla_prep",
    )(p_mla, cos_t, sin_t, qnw, kvnw, wuq, wuk_pairs)


def _softmax_rows(s, m_scr, l_scr):
    n = s.shape[1]
    m_prev = m_scr[...]
    m_new = jnp.maximum(m_prev, jnp.max(s, -1, keepdims=True))
    alpha = jnp.exp2(m_prev - m_new)
    m_wide = jnp.concatenate([m_new] * (n // LANES), axis=1) if n >= LANES else m_new[:, 0:n]
    p = jnp.exp2(s - m_wide)
    l_scr[...] = alpha * l_scr[...] + jnp.sum(p, -1, keepdims=True)
    m_scr[...] = m_new
    return p, alpha


def _softmax_init(m_scr, l_scr, acc_scr):
    m_scr[...] = jnp.full(m_scr.shape, -jnp.inf, F32)
    l_scr[...] = jnp.zeros(l_scr.shape, F32)
    acc_scr[...] = jnp.zeros(acc_scr.shape, F32)


def _flash_kernel(qi_ref, ki_ref, q_ref, k_ref, kt_ref, o_ref, m_scr, l_scr, acc_scr, *, tq, tk):
    step_i = pl.program_id(1)
    qi = qi_ref[step_i]
    ki = ki_ref[step_i]
    rows = tq * MLA_HEADS

    @pl.when(ki == 0)
    def _():
        _softmax_init(m_scr, l_scr, acc_scr)

    def step(masked, nkeys):
        s_t = jnp.dot(k_ref[0, 0:nkeys, :], q_ref[0, 0], preferred_element_type=F32)
        if masked:
            k_pos = ki * tk + lax.broadcasted_iota(jnp.int32, (nkeys, rows), 0)
            q_pos = qi * tq + (lax.broadcasted_iota(jnp.int32, (nkeys, rows), 1) & (tq - 1))
            s_t = jnp.where(k_pos <= q_pos, s_t, -jnp.inf)
        m_prev = m_scr[...]
        m_new = jnp.maximum(m_prev, jnp.max(s_t, axis=0, keepdims=True))
        alpha = jnp.exp2(m_prev - m_new)
        p_t = jnp.exp2(s_t - m_new)
        l_scr[...] = alpha * l_scr[...] + jnp.sum(p_t, axis=0, keepdims=True)
        m_scr[...] = m_new
        acc_scr[...] = acc_scr[...] * alpha + jnp.dot(kt_ref[:, 0:nkeys], p_t.astype(BF16),
                                                      preferred_element_type=F32)

    crosses = (ki + 1) * tk > qi * tq + 1
    short = (qi + 1) * tq - ki * tk
    if tq < tk and tk % tq == 0:
        use_short = crosses & (short == tq)

        @pl.when(use_short)
        def _():
            step(True, tq)
    else:
        use_short = False

    @pl.when(crosses & jnp.logical_not(use_short))
    def _():
        step(True, tk)

    @pl.when(jnp.logical_not(crosses))
    def _():
        step(False, tk)

    @pl.when(ki == ((qi + 1) * tq - 1) // tk)
    def _():
        o_t = acc_scr[...] * (1.0 / l_scr[...])
        o_ref[0] = jnp.transpose(o_t).astype(o_ref.dtype).reshape(MLA_HEADS, tq, MLA_KV_RANK)


def _flash_attention(q_t, kcat, kt, *, tq, tk):
    b, nq, _, rows = q_t.shape
    t = nq * tq
    nkb = t // tk
    kern = functools.partial(_flash_kernel, tq=tq, tk=tk)
    pairs = [(qi, ki) for qi in range(nq) for ki in range(((qi + 1) * tq - 1) // tk + 1)]
    qi_arr = jnp.asarray([p[0] for p in pairs], jnp.int32)
    ki_arr = jnp.asarray([p[1] for p in pairs], jnp.int32)
    grid_spec = pltpu.PrefetchScalarGridSpec(
        num_scalar_prefetch=2, grid=(b, len(pairs)),
        in_specs=[pl.BlockSpec((1, 1, QK_WIDTH, rows), lambda i, s, qa, ka: (i, qa[s], 0, 0)),
                  pl.BlockSpec((1, tk, QK_WIDTH), lambda i, s, qa, ka: (i, ka[s], 0)),
                  pl.BlockSpec((MLA_KV_RANK, tk), lambda i, s, qa, ka: (0, i * nkb + ka[s]))],
        out_specs=pl.BlockSpec((1, MLA_HEADS, tq, MLA_KV_RANK), lambda i, s, qa, ka: (i, 0, qa[s], 0)),
        scratch_shapes=[pltpu.VMEM((1, rows), F32), pltpu.VMEM((1, rows), F32),
                        pltpu.VMEM((MLA_KV_RANK, rows), F32)])
    return pl.pallas_call(
        kern, grid_spec=grid_spec,
        out_shape=jax.ShapeDtypeStruct((b, MLA_HEADS, t, MLA_KV_RANK), BF16),
        compiler_params=_cparams(("parallel", "arbitrary")), name="mla_flash",
    )(qi_arr, ki_arr, q_t, kcat, kt)


def _paged_kernel(pt_ref, q_ref, knew_ref, sel_ref, rowsel_ref, rowselt_ref, lat_hbm, rope_hbm, o_ref,
                  latbuf, ropebuf, sems, m_scr, l_scr, acc_scr, *, tpad, tv, layer, ngroups):
    bi = pl.program_id(0)
    nb = pl.num_programs(0)
    npg = PAGES_PER_GROUP
    rows = tpad * MLA_HEADS
    rows_v = tv * MLA_HEADS
    reps = MLA_KV_RANK // LANES

    def copies(seq, grp, slot):
        out = []
        for i in range(npg):
            page = pt_ref[seq, grp * npg + i]
            dst = pl.ds(i * PAGE_SIZE, PAGE_SIZE)
            out.append(pltpu.make_async_copy(lat_hbm.at[layer, page], latbuf.at[slot, dst], sems.at[0, slot, i]))
            out.append(pltpu.make_async_copy(rope_hbm.at[layer, page], ropebuf.at[slot, :, dst], sems.at[1, slot, i]))
        return out

    def start(seq, grp, slot):
        for cp in copies(seq, grp, slot):
            cp.start()

    def wait(seq, grp, slot):
        for cp in copies(seq, grp, slot):
            cp.wait()

    @pl.when(bi == 0)
    def _():
        for g in range(ngroups):
            start(0, g, g)

    for a in range(2):
        _softmax_init(m_scr.at[a], l_scr.at[a], acc_scr.at[a])
    q_v = jnp.dot(rowsel_ref[...], q_ref[0].reshape(rows, QK_WIDTH), preferred_element_type=F32).astype(BF16)
    q_lat = q_v[:, 0:MLA_KV_RANK]
    q_rd = jnp.dot(q_v[:, MLA_KV_RANK:QK_WIDTH], sel_ref[...], preferred_element_type=F32).astype(BF16)

    nt = (((1,), (1,)), ((), ()))

    def accumulate(a, s, v_b):
        p, alpha = _softmax_rows(s, m_scr.at[a], l_scr.at[a])
        yield
        acc_scr[a] = (acc_scr[a] * jnp.concatenate([alpha] * reps, axis=1)
                      + jnp.dot(p.astype(BF16), v_b, preferred_element_type=F32))

    def consume(a, slot):
        lat_b = latbuf[slot].astype(BF16)
        rope_b = ropebuf[slot].astype(BF16)
        yield
        s = (lax.dot_general(q_lat, lat_b, nt, preferred_element_type=F32)
             + jnp.dot(q_rd, rope_b, preferred_element_type=F32))
        yield
        yield from accumulate(a, s, lat_b)

    k_b = knew_ref[0]
    s2 = lax.dot_general(q_v, k_b, nt, preferred_element_type=F32)
    q_pos = lax.rem(lax.broadcasted_iota(jnp.int32, (rows_v, tpad), 0), tv)
    k_pos = lax.broadcasted_iota(jnp.int32, (rows_v, tpad), 1)
    _lockstep([accumulate(0, jnp.where(k_pos <= q_pos, s2, -jnp.inf), k_b[:, 0:MLA_KV_RANK])])

    for g0 in range(0, ngroups, 2):
        wait(bi, g0, g0)
        wait(bi, g0 + 1, g0 + 1)
        _lockstep([consume(0, g0), consume(1, g0 + 1)])

        @pl.when(bi + 1 < nb)
        def _():
            start(bi + 1, g0, g0)
            start(bi + 1, g0 + 1, g0 + 1)

    m0, m1 = m_scr[0], m_scr[1]
    m = jnp.maximum(m0, m1)
    a0, a1 = jnp.exp2(m0 - m), jnp.exp2(m1 - m)
    inv_l = 1.0 / (a0 * l_scr[0] + a1 * l_scr[1])
    wide = lambda v: jnp.concatenate([v] * reps, axis=1)
    o_v = ((acc_scr[0] * wide(a0) + acc_scr[1] * wide(a1)) * wide(inv_l)).astype(BF16)
    o = jnp.dot(rowselt_ref[...], o_v, preferred_element_type=F32)
    o_ref[0] = o.astype(o_ref.dtype).reshape(MLA_HEADS, tpad, MLA_KV_RANK)


def _row_select_matrix(tpad, tv):
    s = np.zeros((MLA_HEADS * tv, MLA_HEADS * tpad), np.float32)
    for h in range(MLA_HEADS):
        for t in range(tv):
            s[h * tv + t, h * tpad + t] = 1.0
    return s


def _paged_attention(q, knew, cache_lat, cache_rope_t, page_table, sel, *, tpad, tv, layer):
    b = q.shape[0]
    n_pages = page_table.shape[1]
    npg = PAGES_PER_GROUP
    ngroups = n_pages // npg
    assert n_pages % (2 * npg) == 0, "page groups are consumed in slot pairs"
    rows = tpad * MLA_HEADS
    rows_v = tv * MLA_HEADS
    gk = npg * PAGE_SIZE
    rowsel = _row_select_matrix(tpad, tv)
    kern = functools.partial(_paged_kernel, tpad=tpad, tv=tv, layer=layer, ngroups=ngroups)
    stat = pltpu.VMEM((2, rows_v, LANES), F32)
    grid_spec = pltpu.PrefetchScalarGridSpec(
        num_scalar_prefetch=1, grid=(b,),
        in_specs=[pl.BlockSpec((1, MLA_HEADS, tpad, QK_WIDTH), lambda bi, pt: (bi, 0, 0, 0)),
                  pl.BlockSpec((1, tpad, QK_WIDTH), lambda bi, pt: (bi, 0, 0)),
                  pl.BlockSpec(sel.shape, lambda bi, pt: (0, 0)),
                  pl.BlockSpec((rows_v, rows), lambda bi, pt: (0, 0)),
                  pl.BlockSpec((rows, rows_v), lambda bi, pt: (0, 0)),
                  pl.BlockSpec(memory_space=pl.ANY), pl.BlockSpec(memory_space=pl.ANY)],
        out_specs=pl.BlockSpec((1, MLA_HEADS, tpad, MLA_KV_RANK), lambda bi, pt: (bi, 0, 0, 0)),
        scratch_shapes=[pltpu.VMEM((ngroups, gk, MLA_KV_RANK), F32), pltpu.VMEM((ngroups, MLA_ROPE, gk), F32),
                        pltpu.SemaphoreType.DMA((2, ngroups, npg)),
                        stat, stat, pltpu.VMEM((2, rows_v, MLA_KV_RANK), F32)])
    return pl.pallas_call(
        kern, grid_spec=grid_spec,
        out_shape=jax.ShapeDtypeStruct((b, MLA_HEADS, tpad, MLA_KV_RANK), BF16),
        compiler_params=_cparams(("arbitrary",)), name="mla_paged",
    )(page_table, q, knew, sel, jnp.asarray(rowsel, BF16), jnp.asarray(rowsel.T, BF16), cache_lat, cache_rope_t)


def _out_kernel(x_ref, yssd_ref, olat_ref, gate_ref, ygdn_ref, wuv_ref, wout_ref, g_ref, b_ref, o_ref, *, tm):
    y_mla = []
    for pair in range(MLA_HEADS // 2):
        o_pair = None
        for i in range(2):
            o_h = olat_ref[:, 2 * pair + i].reshape(tm, MLA_KV_RANK)
            part = jnp.dot(o_h, wuv_ref[pair, i * MLA_KV_RANK:(i + 1) * MLA_KV_RANK, :], preferred_element_type=F32)
            o_pair = part if o_pair is None else o_pair + part
        y_mla.append(o_pair * gate_ref[:, pair * LANES:(pair + 1) * LANES])
    y_mla = jnp.concatenate(y_mla, axis=1)
    mm = (_bdot(yssd_ref[...], wout_ref[0:SSD_WIDTH, :])
          + _bdot(y_mla, wout_ref[SSD_WIDTH:SSD_WIDTH + MLA_WIDTH, :])
          + _bdot(ygdn_ref[...], wout_ref[SSD_WIDTH + MLA_WIDTH:MIX_WIDTH, :]))
    o_ref[...] = _layer_norm_rows(DEEPNORM_ALPHA * x_ref[...] + mm, g_ref[...], b_ref[...])


def _out_proj(x, y_ssd, o_lat, gate, y_gdn, l, wuv_pairs, wout, g, b, tm):
    m = x.shape[0]
    t = o_lat.shape[2]
    row = lambda i: (i, 0)
    nb, tt, hm_map = _head_major_block(t, tm)
    return pl.pallas_call(
        functools.partial(_out_kernel, tm=tm), grid=(m // tm,),
        in_specs=[pl.BlockSpec((tm, D_MODEL), row), pl.BlockSpec((tm, SSD_WIDTH), row),
                  pl.BlockSpec((nb, MLA_HEADS, tt, MLA_KV_RANK), hm_map), pl.BlockSpec((tm, MLA_WIDTH), row),
                  pl.BlockSpec((tm, GDN_WIDTH), row),
                  _layer_spec(wuv_pairs, l), _layer_spec(wout, l), _layer_spec(g, l), _layer_spec(b, l)],
        out_specs=pl.BlockSpec((tm, D_MODEL), row),
        out_shape=jax.ShapeDtypeStruct((m, D_MODEL), F32),
        compiler_params=_cparams(("parallel",)), name="out_proj",
    )(x, y_ssd, o_lat, gate, y_gdn, wuv_pairs, wout, g, b)


def _pad_last(v, n=LANES, before=0):
    return jnp.pad(v, ((0, 0),) * (v.ndim - 1) + ((before, n - before - v.shape[-1]),))


def _prep_weights(w_in, ssd_conv_w, ssd_conv_b, ssd_dt_bias, ssd_a_log, ssd_d, ssd_norm_w,
                  mla_q_norm_w, mla_w_uq, mla_kv_norm_w, mla_w_uk, mla_w_uv,
                  gdn_conv_w, gdn_dt_bias, gdn_a_log, gdn_norm_w, w_out, ln_g, ln_b):
    depth = w_in.shape[0]
    (w_z, w_xbc, w_dt, w_cq, w_ckv, w_kr, w_gate, w_qkv, w_gz, w_gb, w_ga) = jnp.split(w_in, IN_OFFSETS, axis=2)
    w_ssd = [w_z, w_xbc, _pad_last(w_dt)]
    kr_tiled = jnp.concatenate([jnp.tile(w_kr[..., :ROPE_HALF], (1, 1, MLA_HEADS)),
                                jnp.tile(w_kr[..., ROPE_HALF:], (1, 1, MLA_HEADS))], axis=2)
    w_mla = [w_cq, w_ckv, kr_tiled, w_gate]
    w_gdn = [w_qkv, w_gz, _pad_last(jnp.concatenate([w_gb, w_ga], axis=2))]

    uq = mla_w_uq.reshape(depth, MLA_Q_RANK, MLA_HEADS, MLA_NOPE + MLA_ROPE)
    wuq = jnp.concatenate([uq[..., :MLA_NOPE].reshape(depth, MLA_Q_RANK, -1),
                           uq[..., MLA_NOPE:MLA_NOPE + ROPE_HALF].reshape(depth, MLA_Q_RANK, -1),
                           uq[..., MLA_NOPE + ROPE_HALF:].reshape(depth, MLA_Q_RANK, -1)], axis=2).astype(BF16)
    hp = MLA_HEADS // 2
    uk = jnp.transpose(mla_w_uk, (0, 2, 3, 1)).reshape(depth, hp, 2, MLA_NOPE, MLA_KV_RANK)
    uv = jnp.transpose(mla_w_uv, (0, 2, 1, 3)).reshape(depth, hp, 2, MLA_KV_RANK, MLA_V)
    zk = jnp.zeros_like(uk[:, :, 0])
    zv = jnp.zeros_like(uv[:, :, 0])
    wuk_pairs = jnp.concatenate([jnp.concatenate([uk[:, :, 0], zk], axis=-1),
                                 jnp.concatenate([zk, uk[:, :, 1]], axis=-1)], axis=-2).astype(BF16)
    wuv_pairs = jnp.concatenate([jnp.concatenate([uv[:, :, 0], zv], axis=-1),
                                 jnp.concatenate([zv, uv[:, :, 1]], axis=-1)], axis=-2).astype(BF16)

    row = lambda v: v[:, None, :]
    return dict(
        w_in=jnp.concatenate(w_ssd + w_mla + w_gdn, axis=2).astype(BF16),
        ssd_cw=ssd_conv_w, ssd_cb=row(ssd_conv_b),
        ssd_dtb=_pad_last(row(ssd_dt_bias)), ssd_alog=_pad_last(row(ssd_a_log)),
        ssd_dexp=row(jnp.repeat(ssd_d, SSD_HEAD_DIM, axis=1)), ssd_nw=row(ssd_norm_w),
        qnw=row(mla_q_norm_w), kvnw=row(mla_kv_norm_w), wuq=wuq,
        wuk_pairs=wuk_pairs, wuv_pairs=wuv_pairs,
        gdn_cw=gdn_conv_w, gdn_bias=_pad_last(row(gdn_dt_bias), before=GDN_HEADS),
        gdn_alog=_pad_last(row(gdn_a_log), before=GDN_HEADS), gdn_nw=row(gdn_norm_w),
        w_out=w_out.astype(BF16), ln_g=row(ln_g), ln_b=row(ln_b))


def _rope_tables(pos):
    inv = ROPE_THETA ** (-jnp.arange(ROPE_HALF, dtype=F32) / ROPE_HALF)
    ang = pos.astype(F32)[:, None] * inv[None, :]
    return jnp.tile(jnp.cos(ang), (1, MLA_HEADS)), jnp.tile(jnp.sin(ang), (1, MLA_HEADS))


def _head_expand_matrix():
    e = np.zeros((LANES, SSD_WIDTH), np.float32)
    for h in range(SSD_HEADS):
        e[h, h * SSD_HEAD_DIM:(h + 1) * SSD_HEAD_DIM] = 1.0
    return jnp.asarray(e, BF16)


def _rope_select_matrix():
    s = np.zeros((2 * LANES, MLA_ROPE), np.float32)
    for j in range(2 * LANES):
        s[j, (j // LANES) * ROPE_HALF + (j % ROPE_HALF)] = 1.0
    return jnp.asarray(s, BF16)


def _tail8(conv_state):
    return jnp.pad(conv_state, ((0, 0), (0, 0), (SUBLANES - (CONV_WIDTH - 1), 0), (0, 0)))


def _trunk(x, pos, tv, ssd_conv, ssd_state, gdn_conv, gdn_state, emb_g, emb_b, w, cfg, paged=None):
    b, t, _ = x.shape
    m = b * t
    tm = cfg["tm"]
    cos_t, sin_t = _rope_tables(pos)
    if cos_t.shape[0] < tm:
        reps = tm // cos_t.shape[0]
        cos_t, sin_t = jnp.tile(cos_t, (reps, 1)), jnp.tile(sin_t, (reps, 1))
    emat = _head_expand_matrix()
    sel = _rope_select_matrix()
    ssd_c8_in, gdn_c8_in = _tail8(ssd_conv), _tail8(gdn_conv)
    gw = SSD_WIDTH // SSD_GROUPS
    ssd_h_in = ssd_state.reshape(DEPTH, b, SSD_GROUPS, gw, SSD_STATE)
    h = x.reshape(m, D_MODEL)
    new_states = []
    for l in range(DEPTH):
        if l == 0:
            h, p_ssd, p_dt, p_mla, p_gdn, p_ba = _project(h, w["w_in"], l, cfg["tm_proj"], emb_ln=(emb_g, emb_b))
        else:
            p_ssd, p_dt, p_mla, p_gdn, p_ba = _project(h, w["w_in"], l, cfg["tm_proj"])

        y_ssd, ssd_c8, ssd_h = _ssd_scan(
            p_ssd.reshape(b, t, -1), p_dt.reshape(b, t, LANES), ssd_c8_in, ssd_h_in, l,
            w["ssd_cw"], w["ssd_cb"], w["ssd_dtb"], w["ssd_alog"], w["ssd_dexp"], w["ssd_nw"], emat,
            L=cfg["ssd_chunk"], tv=min(tv, cfg["ssd_chunk"]))
        y_gdn, gdn_c8, gdn_s = _gdn_scan(
            p_gdn.reshape(b, t, -1), p_ba.reshape(b, t, LANES), gdn_c8_in, gdn_state, l,
            w["gdn_cw"], w["gdn_bias"], w["gdn_alog"], w["gdn_nw"],
            L=cfg["gdn_chunk"], tv=min(tv, cfg["gdn_chunk"]))

        q, kcat, kt, ckv, kr_t, gate = _mla_prep(p_mla, cos_t, sin_t, l, w["qnw"], w["kvnw"], w["wuq"],
                                                 w["wuk_pairs"], b, t, tm, tq_t=cfg["tq"] if paged is None else None)
        kcat = kcat.reshape(b, t, QK_WIDTH)
        if paged is None:
            o_lat = _flash_attention(q, kcat, kt, tq=cfg["tq"], tk=cfg["tk"])
        else:
            cache_lat, cache_rope_t, page_table = paged
            o_lat = _paged_attention(q, kcat, cache_lat, cache_rope_t, page_table, sel, tpad=t, tv=tv, layer=l)

        h = _out_proj(h, y_ssd.reshape(m, SSD_WIDTH), o_lat, gate, y_gdn.reshape(m, GDN_WIDTH), l,
                      w["wuv_pairs"], w["w_out"], w["ln_g"], w["ln_b"], tm)

        kr = jnp.concatenate([kr_t[:, 0:ROPE_HALF], kr_t[:, LANES:LANES + ROPE_HALF]], axis=1)
        sl = slice(SUBLANES - (CONV_WIDTH - 1), SUBLANES)
        new_states.append((ckv.reshape(b, t, MLA_KV_RANK), kr.reshape(b, t, MLA_ROPE), ssd_c8[:, sl],
                           ssd_h.reshape(b, SSD_HEADS, SSD_HEAD_DIM, SSD_STATE), gdn_c8[:, sl], gdn_s))
    return h.reshape(b, t, D_MODEL), tuple(jnp.stack(s) for s in zip(*new_states))


def kernel(x_prompt, x_sample, cache_kv_latent, cache_k_rope, state_ssd_conv, state_ssd, state_gdn_conv, state_gdn, page_table, emb_ln_g, emb_ln_b, w_in, ssd_conv_w, ssd_conv_b, ssd_dt_bias, ssd_a_log, ssd_d, ssd_norm_w, mla_q_norm_w, mla_w_uq, mla_kv_norm_w, mla_w_uk, mla_w_uv, gdn_conv_w, gdn_dt_bias, gdn_a_log, gdn_norm_w, w_out, ln_g, ln_b):
    weights = _prep_weights(w_in, ssd_conv_w, ssd_conv_b, ssd_dt_bias, ssd_a_log, ssd_d, ssd_norm_w,
                            mla_q_norm_w, mla_w_uq, mla_kv_norm_w, mla_w_uk, mla_w_uv,
                            gdn_conv_w, gdn_dt_bias, gdn_a_log, gdn_norm_w, w_out, ln_g, ln_b)

    bp, tp, _ = x_prompt.shape
    zeros = lambda *s: jnp.zeros(s, F32)
    ssd_chunk = SSD_CHUNK if tp % SSD_CHUNK == 0 else tp
    gdn_chunk = GDN_CHUNK if tp % GDN_CHUNK == 0 else tp
    cfg_p = dict(tm=min(512, bp * tp), tm_proj=min(512, bp * tp), ssd_chunk=ssd_chunk, gdn_chunk=gdn_chunk,
                 tq=min(256, tp), tk=min(512, tp))
    y_prompt, st_p = _trunk(
        x_prompt, jnp.arange(tp), tp,
        zeros(DEPTH, bp, CONV_WIDTH - 1, SSD_CONV_DIM), zeros(DEPTH, bp, SSD_HEADS, SSD_HEAD_DIM, SSD_STATE),
        zeros(DEPTH, bp, CONV_WIDTH - 1, GDN_CONV_DIM), zeros(DEPTH, bp, GDN_HEADS, GDN_HEAD_DIM, GDN_HEAD_DIM),
        emb_ln_g, emb_ln_b, weights, cfg_p)

    bs, ts, _ = x_sample.shape
    tpad = -(-ts // BF16_ROWS) * BF16_ROWS
    past_len = page_table.shape[1] * PAGE_SIZE
    xs = jnp.pad(x_sample, ((0, 0), (0, tpad - ts), (0, 0)))
    cfg_s = dict(tm=min(512, bs * tpad), tm_proj=min(512, bs * tpad), ssd_chunk=tpad, gdn_chunk=tpad,
                 tq=tpad, tk=tpad)
    y_s, st_s = _trunk(
        xs, past_len + jnp.arange(tpad), ts, state_ssd_conv, state_ssd, state_gdn_conv, state_gdn,
        emb_ln_g, emb_ln_b, weights, cfg_s,
        paged=(cache_kv_latent, jnp.swapaxes(cache_k_rope, 2, 3), page_table))
    y_sample = y_s[:, :ts]
    s_lat, s_rope, s_ssd_conv, s_ssd, s_gdn_conv, s_gdn = st_s
    return (y_prompt, y_sample) + tuple(st_p) + (s_lat[:, :, :ts], s_rope[:, :, :ts], s_ssd_conv, s_ssd, s_gdn_conv, s_gdn)
```

```python
import functools
import math

import jax
import jax.numpy as jnp
import numpy as np
from jax import lax
from jax.experimental import pallas as pl
from jax.experimental.pallas import tpu as pltpu

F32 = jnp.float32
BF16 = jnp.bfloat16

D_MODEL = 1024
CONV_WIDTH = 4
SSD_HEADS = 16
SSD_HEAD_DIM = 64
SSD_WIDTH = SSD_HEADS * SSD_HEAD_DIM
SSD_GROUPS = 2
SSD_STATE = 128
SSD_CONV_DIM = SSD_WIDTH + 2 * SSD_GROUPS * SSD_STATE
SSD_CHUNK = 128
MLA_HEADS = 8
MLA_NOPE = 64
MLA_ROPE = 32
MLA_V = 64
MLA_WIDTH = MLA_HEADS * MLA_V
MLA_Q_RANK = 384
MLA_KV_RANK = 256
MLA_SCALE = (MLA_NOPE + MLA_ROPE) ** -0.5
ROPE_THETA = 10000.0
GDN_HEADS = 4
GDN_HEAD_DIM = 128
GDN_WIDTH = GDN_HEADS * GDN_HEAD_DIM
GDN_CONV_DIM = 3 * GDN_WIDTH
GDN_CHUNK = 64
MIX_WIDTH = SSD_WIDTH + MLA_WIDTH + GDN_WIDTH
IN_SIZES = (SSD_WIDTH, SSD_CONV_DIM, SSD_HEADS, MLA_Q_RANK, MLA_KV_RANK, MLA_ROPE, MLA_WIDTH,
            GDN_CONV_DIM, GDN_WIDTH, GDN_HEADS, GDN_HEADS)
IN_OFFSETS = tuple(int(o) for o in np.cumsum(IN_SIZES)[:-1])
DEPTH = 2
DEEPNORM_ALPHA = (2 * DEPTH) ** 0.25
LN_EPS = 1e-5
RMS_EPS = 1e-6
L2_EPS = 1e-6
PAGE_SIZE = 128

LANES = 128
SUBLANES = 8
BF16_ROWS = 16
ROPE_HALF = MLA_ROPE // 2
P_SSD = SSD_WIDTH + SSD_CONV_DIM + LANES
P_MLA = MLA_Q_RANK + MLA_KV_RANK + 2 * LANES + MLA_WIDTH
P_GDN = GDN_CONV_DIM + GDN_WIDTH + LANES
QK_WIDTH = MLA_KV_RANK + 2 * LANES
Q_SCALE = MLA_SCALE * math.log2(math.e)
VMEM_LIMIT = 56 * 1024 * 1024
PAGES_PER_GROUP = 32
GDN_BATCH_PER_STEP = 4
SSD_BATCH_PER_STEP = 4

def _cparams(sem):
    return pltpu.CompilerParams(dimension_semantics=sem, vmem_limit_bytes=VMEM_LIMIT)


def _bdot(a, b):
    return jnp.dot(a.astype(BF16), b.astype(BF16), preferred_element_type=F32)


def _bdot_nt(a, b):
    return lax.dot_general(a.astype(BF16), b.astype(BF16), (((1,), (1,)), ((), ())),
                           preferred_element_type=F32)


def _fdot(a, b):
    return jnp.dot(a, b, precision=lax.Precision.HIGHEST, preferred_element_type=F32)


def _silu(x):
    return x * (1.0 / (1.0 + jnp.exp(-x)))


def _softplus(x):
    return jnp.maximum(x, 0.0) + jnp.log1p(jnp.exp(-jnp.abs(x)))


def _const_spec(shape):
    nd = len(shape)
    return pl.BlockSpec(shape, lambda *_: (0,) * nd)


def _layer_spec(stacked, l):
    rest = stacked.shape[1:]
    return pl.BlockSpec((None,) + rest, lambda *_: (l,) + (0,) * len(rest))


def _layer_norm_rows(x, g, b):
    mu = jnp.mean(x, -1, keepdims=True)
    xc = x - mu
    var = jnp.mean(xc * xc, -1, keepdims=True)
    return xc * lax.rsqrt(var + LN_EPS) * g + b


def _proj_kernel(x_ref, w_ref, *refs, embed):
    if embed:
        g_ref, b_ref, h_ref, ssd_ref, dt_ref, mla_ref, gdn_ref, ba_ref = refs
        h = _layer_norm_rows(x_ref[...], g_ref[...], b_ref[...])
        h_ref[...] = h
        x = h.astype(BF16)
    else:
        ssd_ref, dt_ref, mla_ref, gdn_ref, ba_ref = refs
        x = x_ref[...].astype(BF16)
    p = jnp.dot(x, w_ref[:, 0:P_SSD], preferred_element_type=F32)
    ssd_ref[...] = p[:, 0:P_SSD - LANES].astype(BF16)
    dt_ref[...] = p[:, P_SSD - LANES:P_SSD]
    mla_ref[...] = jnp.dot(x, w_ref[:, P_SSD:P_SSD + P_MLA], preferred_element_type=F32).astype(BF16)
    p = jnp.dot(x, w_ref[:, P_SSD + P_MLA:P_SSD + P_MLA + P_GDN], preferred_element_type=F32)
    gdn_ref[...] = p[:, 0:P_GDN - LANES].astype(BF16)
    ba_ref[...] = p[:, P_GDN - LANES:P_GDN]


def _project(x, w, l, tm, emb_ln=None):
    m, k = x.shape
    row = lambda i: (i, 0)
    widths = (P_SSD - LANES, LANES, P_MLA, P_GDN - LANES, LANES)
    dtypes = (BF16, F32, BF16, BF16, F32)
    operands, in_specs = [x, w], [pl.BlockSpec((tm, k), row), _layer_spec(w, l)]
    if emb_ln is not None:
        operands += [v.reshape(1, k) for v in emb_ln]
        in_specs += [_const_spec((1, k)), _const_spec((1, k))]
        widths, dtypes = (k,) + widths, (F32,) + dtypes
    return pl.pallas_call(
        functools.partial(_proj_kernel, embed=emb_ln is not None), grid=(m // tm,),
        in_specs=in_specs,
        out_specs=[pl.BlockSpec((tm, n), row) for n in widths],
        out_shape=[jax.ShapeDtypeStruct((m, n), d) for n, d in zip(widths, dtypes)],
        compiler_params=_cparams(("parallel",)), name="in_proj")(*operands)


def _conv_chunk(xbuf, x_new, cw_ref, L):
    xbuf[SUBLANES:SUBLANES + L, :] = x_new
    y = cw_ref[CONV_WIDTH - 1:CONV_WIDTH, :] * x_new
    for k in range(CONV_WIDTH - 1):
        off = SUBLANES - (CONV_WIDTH - 1) + k
        y = y + cw_ref[k:k + 1, :] * xbuf[off:off + L, :]
    return y


def _ssd_chunk(p_ref, dt_ref, cw_ref, cb_ref, dtb_ref, alog_ref, dexp_ref, nw_ref, e_ref,
               y_ref, cnew_ref, xbuf, ht, *, L, tv):
    z = p_ref[:, 0:SSD_WIDTH].astype(F32)
    xbc_raw = p_ref[:, SSD_WIDTH:SSD_WIDTH + SSD_CONV_DIM].astype(F32)
    dt_raw = dt_ref[...]

    xbc = _silu(_conv_chunk(xbuf, xbc_raw, cw_ref, L) + cb_ref[...])
    cnew_ref[...] = xbuf[tv:tv + SUBLANES, :]
    xbuf[0:SUBLANES, :] = xbuf[L:L + SUBLANES, :]
    yield

    xs = xbc[:, 0:SSD_WIDTH]
    gs = SSD_GROUPS * SSD_STATE
    bm = xbc[:, SSD_WIDTH:SSD_WIDTH + gs]
    cm = xbc[:, SSD_WIDTH + gs:SSD_WIDTH + 2 * gs]

    dt = _softplus(dt_raw + dtb_ref[...])
    if tv < L:
        rows = lax.broadcasted_iota(jnp.int32, (L, LANES), 0)
        dt = jnp.where(rows < tv, dt, 0.0)
    a = -jnp.exp(alog_ref[...])
    da = dt * a
    r_i = lax.broadcasted_iota(jnp.int32, (L, L), 0)
    c_i = lax.broadcasted_iota(jnp.int32, (L, L), 1)
    causal = c_i <= r_i
    tri = jnp.where(causal, 1.0, 0.0).astype(F32)
    acum = _fdot(tri, da)
    last = acum[L - 1:L, :]
    ea = jnp.exp(acum)
    wdec = jnp.exp(last - acum)

    def pad_rows(v):
        if L == LANES:
            return v
        return jnp.concatenate([v, jnp.zeros((LANES - L, v.shape[1]), v.dtype)], axis=0)

    acum_t = jnp.transpose(pad_rows(acum))
    yield

    def hilo(v):
        hi = v.astype(BF16)
        lo = (v - hi.astype(F32)).astype(BF16)
        return hi, lo

    parts = []
    for v in (dt, ea, wdec):
        parts.extend(hilo(v))
    stacked = jnp.concatenate(parts, axis=0)
    expanded = jnp.dot(stacked, e_ref[...], preferred_element_type=F32)
    dt_e = expanded[0:L] + expanded[L:2 * L]
    ea_e = expanded[2 * L:3 * L] + expanded[3 * L:4 * L]
    wd_e = expanded[4 * L:5 * L] + expanded[5 * L:6 * L]

    yield
    xdt = xs * dt_e
    xdt_b = xdt.astype(BF16)
    lane = lax.broadcasted_iota(jnp.int32, (L, LANES), 1)
    hpg = SSD_HEADS // SSD_GROUPS

    y_parts = []
    for g in range(SSD_GROUPS):
        bg = bm[:, g * SSD_STATE:(g + 1) * SSD_STATE]
        cg = cm[:, g * SSD_STATE:(g + 1) * SSD_STATE]
        cb = _bdot_nt(cg, bg)
        for j in range(hpg // 2):
            pair = g * (hpg // 2) + j
            xp = xdt_b[:, pair * LANES:(pair + 1) * LANES]
            ys = []
            for h in (2 * pair, 2 * pair + 1):
                seg = acum[:, h:h + 1] - acum_t[h:h + 1, 0:L]
                dec = jnp.exp(jnp.where(causal, seg, -jnp.inf))
                ys.append(_bdot(cb * dec, xp))
            y_parts.append(jnp.where(lane < SSD_HEAD_DIM, ys[0], ys[1]))
            yield
    y_in = jnp.concatenate(y_parts, axis=1)

    gw = hpg * SSD_HEAD_DIM
    y_st_parts = []
    for g in range(SSD_GROUPS):
        cg = cm[:, g * SSD_STATE:(g + 1) * SSD_STATE]
        y_st_parts.append(_bdot(cg, ht[g]))
    y_st = jnp.concatenate(y_st_parts, axis=1) * ea_e
    yield

    xw = (xdt * wd_e)
    for g in range(SSD_GROUPS):
        bg_t = jnp.transpose(pad_rows(bm[:, g * SSD_STATE:(g + 1) * SSD_STATE]))
        xw_g = pad_rows(xw[:, g * gw:(g + 1) * gw])
        ht[g] = ht[g] * ea_e[L - 1:L, g * gw:(g + 1) * gw] + _bdot(bg_t, xw_g)
    yield

    y = (y_in + y_st + dexp_ref[...] * xs) * _silu(z)
    outs = []
    for g in range(SSD_GROUPS):
        yg = y[:, g * gw:(g + 1) * gw]
        ms = jnp.mean(yg * yg, -1, keepdims=True)
        outs.append(yg * lax.rsqrt(ms + RMS_EPS) * nw_ref[:, g * gw:(g + 1) * gw])
    y_ref[...] = jnp.concatenate(outs, axis=1).astype(y_ref.dtype)


def _lockstep(gens):
    while gens:
        gens = [g for g in gens if next(g, StopIteration) is not StopIteration]


def _ssd_kernel(p_ref, dt_ref, cprev_ref, h0_ref, cw_ref, cb_ref, dtb_ref, alog_ref, dexp_ref, nw_ref, e_ref,
                y_ref, cnew_ref, hout_ref, xbuf, ht, *, L, tv, nbat):
    c = pl.program_id(1)
    nc = pl.num_programs(1)

    @pl.when(c == 0)
    def _():
        xbuf[:, 0:SUBLANES, :] = cprev_ref[...]
        for i in range(nbat):
            for g in range(SSD_GROUPS):
                ht[i, g] = jnp.transpose(h0_ref[i, g])

    _lockstep([_ssd_chunk(p_ref.at[i], dt_ref.at[i], cw_ref, cb_ref, dtb_ref, alog_ref, dexp_ref, nw_ref, e_ref,
                          y_ref.at[i], cnew_ref.at[i], xbuf.at[i], ht.at[i], L=L, tv=tv) for i in range(nbat)])

    @pl.when(c == nc - 1)
    def _():
        for i in range(nbat):
            for g in range(SSD_GROUPS):
                hout_ref[i, g] = jnp.transpose(ht[i, g])


def _ssd_scan(p_ssd, p_dt, conv_prev8, h0, l, cw, cb, dtb, alog, dexp, nw, emat, *, L, tv):
    b, t, _ = p_ssd.shape
    nc = t // L
    gw = SSD_WIDTH // SSD_GROUPS
    nbat = math.gcd(b, SSD_BATCH_PER_STEP)
    kern = functools.partial(_ssd_kernel, L=L, tv=tv, nbat=nbat)
    hshape = (nbat, SSD_GROUPS, gw, SSD_STATE)
    return pl.pallas_call(
        kern, grid=(b // nbat, nc),
        in_specs=[pl.BlockSpec((nbat, L, P_SSD - LANES), lambda i, c: (i, c, 0)),
                  pl.BlockSpec((nbat, L, LANES), lambda i, c: (i, c, 0)),
                  pl.BlockSpec((None, nbat, SUBLANES, SSD_CONV_DIM), lambda i, c: (l, i, 0, 0)),
                  pl.BlockSpec((None,) + hshape, lambda i, c: (l, i, 0, 0, 0)),
                  _layer_spec(cw, l), _layer_spec(cb, l), _layer_spec(dtb, l), _layer_spec(alog, l),
                  _layer_spec(dexp, l), _layer_spec(nw, l), _const_spec((LANES, SSD_WIDTH))],
        out_specs=[pl.BlockSpec((nbat, L, SSD_WIDTH), lambda i, c: (i, c, 0)),
                   pl.BlockSpec((nbat, SUBLANES, SSD_CONV_DIM), lambda i, c: (i, 0, 0)),
                   pl.BlockSpec(hshape, lambda i, c: (i, 0, 0, 0))],
        out_shape=[jax.ShapeDtypeStruct((b, t, SSD_WIDTH), BF16),
                   jax.ShapeDtypeStruct((b, SUBLANES, SSD_CONV_DIM), F32),
                   jax.ShapeDtypeStruct((b,) + hshape[1:], F32)],
        scratch_shapes=[pltpu.VMEM((nbat, L + SUBLANES, SSD_CONV_DIM), F32),
                        pltpu.VMEM((nbat, SSD_GROUPS, SSD_STATE, gw), F32)],
        compiler_params=_cparams(("parallel", "arbitrary")), name="ssd_scan",
    )(p_ssd, p_dt, conv_prev8, h0, cw, cb, dtb, alog, dexp, nw, emat)


def _gdn_chunk(p_ref, ba_ref, cw_ref, bias_ref, alog_ref, nw_ref, y_ref, cnew_ref, xbuf, st, *, L, tv):
    H = GDN_HEADS
    D = GDN_HEAD_DIM
    R = H * L
    S = max(R, LANES)

    qkv_raw = p_ref[:, 0:GDN_CONV_DIM].astype(F32)
    z = p_ref[:, GDN_CONV_DIM:GDN_CONV_DIM + GDN_WIDTH].astype(F32)
    ba = ba_ref[...]

    qkv = _silu(_conv_chunk(xbuf, qkv_raw, cw_ref, L))
    cnew_ref[...] = xbuf[tv:tv + SUBLANES, :]
    xbuf[0:SUBLANES, :] = xbuf[L:L + SUBLANES, :]
    yield

    beta_f = 1.0 / (1.0 + jnp.exp(-ba))
    g_f = -jnp.exp(alog_ref[...]) * _softplus(ba + bias_ref[...])
    if tv < L:
        rows = lax.broadcasted_iota(jnp.int32, (L, LANES), 0)
        beta_f = jnp.where(rows < tv, beta_f, 0.0)
        g_f = jnp.where(rows < tv, g_f, 0.0)
    r_i = lax.broadcasted_iota(jnp.int32, (L, L), 0)
    c_i = lax.broadcasted_iota(jnp.int32, (L, L), 1)
    tri = jnp.where(c_i <= r_i, 1.0, 0.0).astype(F32)
    gcum_f = _fdot(tri, g_f)
    glast_f = jnp.broadcast_to(gcum_f[L - 1:L, :], (L, LANES))

    def pad_s(v):
        if R == S:
            return v
        return jnp.concatenate([v, jnp.zeros((S - R, v.shape[1]), v.dtype)], axis=0)

    def stack(v):
        return pad_s(jnp.concatenate([v[:, h * D:(h + 1) * D] for h in range(H)], axis=0))

    def col(v, off):
        return pad_s(jnp.concatenate([v[:, off + h:off + h + 1] for h in range(H)], axis=0))

    q_s = stack(qkv[:, 0:GDN_WIDTH])
    k_s = stack(qkv[:, GDN_WIDTH:2 * GDN_WIDTH])
    v_s = stack(qkv[:, 2 * GDN_WIDTH:3 * GDN_WIDTH])
    z_s = stack(z)
    q_s = q_s * lax.rsqrt(jnp.sum(q_s * q_s, -1, keepdims=True) + L2_EPS) * (D ** -0.5)
    k_s = k_s * lax.rsqrt(jnp.sum(k_s * k_s, -1, keepdims=True) + L2_EPS)
    beta = col(beta_f, 0)
    gcum = col(gcum_f, H)
    glast = col(glast_f, H)
    yield

    cmat = jnp.broadcast_to(gcum, (S, S))
    diff = cmat - jnp.transpose(cmat)
    rs = lax.broadcasted_iota(jnp.int32, (S, S), 0)
    cs = lax.broadcasted_iota(jnp.int32, (S, S), 1)
    if L & (L - 1) == 0:
        sh = L.bit_length() - 1
        same = (rs >> sh) == (cs >> sh)
    else:
        same = (rs // L) == (cs // L)
    incl = same & (cs <= rs)
    strict = same & (cs < rs)
    dec = jnp.exp(jnp.where(incl, diff, -jnp.inf))

    kb = k_s * beta
    n_mat = -jnp.where(strict, _bdot_nt(kb, k_s) * dec, 0.0)
    eye = jnp.where(rs == cs, 1.0, 0.0).astype(F32)
    t_mat = eye + n_mat
    npow = n_mat
    yield
    span = 2
    while span < L:
        npow = _bdot(npow, npow)
        yield
        t_mat = t_mat + _bdot(t_mat, npow)
        yield
        span *= 2

    eg = jnp.exp(gcum)
    t_b = t_mat.astype(BF16)
    u = _bdot(t_b, v_s * beta)
    w = _bdot(t_b, kb * eg)
    qg = q_s * eg
    attn = jnp.where(incl, _bdot_nt(q_s, k_s) * dec, 0.0)
    yield
    ws = []
    qs_ = []
    for h in range(H):
        sh_b = st[h].astype(BF16)
        ws.append(_bdot(w[h * L:(h + 1) * L], sh_b))
        qs_.append(_bdot(qg[h * L:(h + 1) * L], sh_b))
    v_new = u - pad_s(jnp.concatenate(ws, axis=0))
    yield
    o = pad_s(jnp.concatenate(qs_, axis=0)) + _bdot(attn, v_new)

    kd = k_s * jnp.exp(glast - gcum)
    kd_t = jnp.transpose(kd).astype(BF16)
    row_head = lax.broadcasted_iota(jnp.int32, (S, D), 0)
    eg_last = jnp.exp(glast)
    for h in range(H):
        vm = jnp.where((row_head >= h * L) & (row_head < (h + 1) * L), v_new, 0.0)
        st[h] = st[h] * eg_last[h * L:h * L + 1, :] + _bdot(kd_t, vm)
    yield

    ms = jnp.mean(o * o, -1, keepdims=True)
    o = o * lax.rsqrt(ms + RMS_EPS) * nw_ref[...] * _silu(z_s)
    y_ref[...] = jnp.concatenate([o[h * L:(h + 1) * L] for h in range(H)], axis=1).astype(y_ref.dtype)


def _gdn_kernel(p_ref, ba_ref, cprev_ref, s0_ref, cw_ref, bias_ref, alog_ref, nw_ref,
                y_ref, cnew_ref, sout_ref, xbuf, st, *, L, tv, nbat):
    c = pl.program_id(1)
    nc = pl.num_programs(1)

    @pl.when(c == 0)
    def _():
        xbuf[:, 0:SUBLANES, :] = cprev_ref[...]
        st[...] = s0_ref[...]

    _lockstep([_gdn_chunk(p_ref.at[i], ba_ref.at[i], cw_ref, bias_ref, alog_ref, nw_ref, y_ref.at[i],
                          cnew_ref.at[i], xbuf.at[i], st.at[i], L=L, tv=tv) for i in range(nbat)])

    @pl.when(c == nc - 1)
    def _():
        sout_ref[...] = st[...]


def _gdn_scan(p_gdn, p_ba, conv_prev8, s0, l, cw, bias, alog, nw, *, L, tv):
    b, t, _ = p_gdn.shape
    nc = t // L
    nbat = math.gcd(b, GDN_BATCH_PER_STEP)
    kern = functools.partial(_gdn_kernel, L=L, tv=tv, nbat=nbat)
    sshape = (nbat, GDN_HEADS, GDN_HEAD_DIM, GDN_HEAD_DIM)
    return pl.pallas_call(
        kern, grid=(b // nbat, nc),
        in_specs=[pl.BlockSpec((nbat, L, P_GDN - LANES), lambda i, c: (i, c, 0)),
                  pl.BlockSpec((nbat, L, LANES), lambda i, c: (i, c, 0)),
                  pl.BlockSpec((None, nbat, SUBLANES, GDN_CONV_DIM), lambda i, c: (l, i, 0, 0)),
                  pl.BlockSpec((None,) + sshape, lambda i, c: (l, i, 0, 0, 0)),
                  _layer_spec(cw, l), _layer_spec(bias, l), _layer_spec(alog, l), _layer_spec(nw, l)],
        out_specs=[pl.BlockSpec((nbat, L, GDN_WIDTH), lambda i, c: (i, c, 0)),
                   pl.BlockSpec((nbat, SUBLANES, GDN_CONV_DIM), lambda i, c: (i, 0, 0)),
                   pl.BlockSpec(sshape, lambda i, c: (i, 0, 0, 0))],
        out_shape=[jax.ShapeDtypeStruct((b, t, GDN_WIDTH), BF16),
                   jax.ShapeDtypeStruct((b, SUBLANES, GDN_CONV_DIM), F32),
                   jax.ShapeDtypeStruct((b,) + sshape[1:], F32)],
        scratch_shapes=[pltpu.VMEM((nbat, L + SUBLANES, GDN_CONV_DIM), F32),
                        pltpu.VMEM(sshape, F32)],
        compiler_params=_cparams(("parallel", "arbitrary")), name="gdn_scan",
    )(p_gdn, p_ba, conv_prev8, s0, cw, bias, alog, nw)


def _mla_prep_kernel(p_ref, cos_ref, sin_ref, qnw_ref, kvnw_ref, wuq_ref, wuk_ref,
                     q_ref, kcat_ref, kt_ref, ckv_ref, kr_ref, gate_ref, *, nb, tt, tq_t):
    cq = p_ref[:, 0:MLA_Q_RANK].astype(F32)
    ckv = p_ref[:, MLA_Q_RANK:MLA_Q_RANK + MLA_KV_RANK].astype(F32)
    kr = p_ref[:, MLA_Q_RANK + MLA_KV_RANK:MLA_Q_RANK + MLA_KV_RANK + 2 * LANES].astype(F32)
    gate = p_ref[:, MLA_Q_RANK + MLA_KV_RANK + 2 * LANES:P_MLA].astype(F32)
    cos = cos_ref[...]
    sin = sin_ref[...]

    cqn = cq * lax.rsqrt(jnp.mean(cq * cq, -1, keepdims=True) + RMS_EPS) * qnw_ref[...]
    q = _bdot(cqn, wuq_ref[...])
    nope_w = MLA_HEADS * MLA_NOPE
    x1 = q[:, nope_w:nope_w + LANES]
    x2 = q[:, nope_w + LANES:nope_w + 2 * LANES]
    r_all = jnp.concatenate([x1 * cos - x2 * sin, x2 * cos + x1 * sin], axis=1) * Q_SCALE

    ckvn = ckv * lax.rsqrt(jnp.mean(ckv * ckv, -1, keepdims=True) + RMS_EPS) * kvnw_ref[...]
    k1 = kr[:, 0:LANES]
    k2 = kr[:, LANES:2 * LANES]
    kr_rot = jnp.concatenate([k1 * cos - k2 * sin, k2 * cos + k1 * sin], axis=1)
    ckv_ref[...] = ckvn
    kr_ref[...] = kr_rot
    kcat_ref[...] = jnp.concatenate([ckvn, kr_rot], axis=1).astype(BF16)
    kt_ref[...] = jnp.transpose(ckvn).astype(BF16)
    gate_ref[...] = _silu(gate).astype(gate_ref.dtype)

    tm = q.shape[0]
    if tq_t is None:
        lane = lax.broadcasted_iota(jnp.int32, (1, 2 * LANES), 1)
        head_of_lane = (lane & (LANES - 1)) >> 4
    else:
        r_all_t = jnp.transpose(r_all)
        row_i = lax.broadcasted_iota(jnp.int32, (2 * LANES, 1), 0)
        head_of_row = (row_i & (LANES - 1)) >> 4
    for pair in range(MLA_HEADS // 2):
        qn = q[:, pair * LANES:(pair + 1) * LANES] * Q_SCALE
        qlat = _bdot(qn, wuk_ref[pair])
        for i in range(2):
            h = 2 * pair + i
            ql = qlat[:, i * MLA_KV_RANK:(i + 1) * MLA_KV_RANK]
            if tq_t is None:
                qr = jnp.where(head_of_lane == h, r_all, 0.0).astype(BF16)
                q_ref[:, h, :, 0:MLA_KV_RANK] = ql.astype(BF16).reshape(nb, tt, MLA_KV_RANK)
                q_ref[:, h, :, MLA_KV_RANK:QK_WIDTH] = qr.reshape(nb, tt, 2 * LANES)
            else:
                ql_t = jnp.transpose(ql).astype(BF16)
                qr_t = jnp.where(head_of_row == h, r_all_t, 0.0).astype(BF16)
                for qb in range(tm // tq_t):
                    cols = slice(h * tq_t, (h + 1) * tq_t)
                    toks = slice(qb * tq_t, (qb + 1) * tq_t)
                    q_ref[0, qb, 0:MLA_KV_RANK, cols] = ql_t[:, toks]
                    q_ref[0, qb, MLA_KV_RANK:QK_WIDTH, cols] = qr_t[:, toks]


def _head_major_block(t, tm):
    if tm <= t:
        npb = t // tm
        return 1, tm, (lambda i: (i // npb, 0, i % npb, 0))
    return tm // t, t, (lambda i: (i, 0, 0, 0))


def _mla_prep(p_mla, cos_t, sin_t, l, qnw, kvnw, wuq, wuk_pairs, b, t, tm, tq_t=None):
    m = p_mla.shape[0]
    nt = cos_t.shape[0] // tm
    row = lambda i: (i, 0)
    tab = lambda i: (i % nt, 0)
    nb, tt, hm_map = _head_major_block(t, tm)
    kern = functools.partial(_mla_prep_kernel, nb=nb, tt=tt, tq_t=tq_t)
    if tq_t is None:
        q_spec = pl.BlockSpec((nb, MLA_HEADS, tt, QK_WIDTH), hm_map)
        q_shape = jax.ShapeDtypeStruct((b, MLA_HEADS, t, QK_WIDTH), BF16)
    else:
        npb = t // tm
        q_spec = pl.BlockSpec((1, tm // tq_t, QK_WIDTH, MLA_HEADS * tq_t), lambda i: (i // npb, i % npb, 0, 0))
        q_shape = jax.ShapeDtypeStruct((b, t // tq_t, QK_WIDTH, MLA_HEADS * tq_t), BF16)
    return pl.pallas_call(
        kern, grid=(m // tm,),
        in_specs=[pl.BlockSpec((tm, P_MLA), row), pl.BlockSpec((tm, LANES), tab), pl.BlockSpec((tm, LANES), tab),
                  _layer_spec(qnw, l), _layer_spec(kvnw, l), _layer_spec(wuq, l), _layer_spec(wuk_pairs, l)],
        out_specs=[q_spec, pl.BlockSpec((tm, QK_WIDTH), row),
                   pl.BlockSpec((MLA_KV_RANK, tm), lambda i: (0, i)),
                   pl.BlockSpec((tm, MLA_KV_RANK), row), pl.BlockSpec((tm, 2 * LANES), row),
                   pl.BlockSpec((tm, MLA_WIDTH), row)],
        out_shape=[q_shape,
                   jax.ShapeDtypeStruct((m, QK_WIDTH), BF16), jax.ShapeDtypeStruct((MLA_KV_RANK, m), BF16),
                   jax.ShapeDtypeStruct((m, MLA_KV_RANK), F32), jax.ShapeDtypeStruct((m, 2 * LANES), F32),
                   jax.ShapeDtypeStruct((m, MLA_WIDTH), BF16)],
        compiler_params=_cparams(("parallel",)), name="mla_prep",
    )(p_mla, cos_t, sin_t, qnw, kvnw, wuq, wuk_pairs)


def _softmax_rows(s, m_scr, l_scr):
    n = s.shape[1]
    m_prev = m_scr[...]
    m_new = jnp.maximum(m_prev, jnp.max(s, -1, keepdims=True))
    alpha = jnp.exp2(m_prev - m_new)
    m_wide = jnp.concatenate([m_new] * (n // LANES), axis=1) if n >= LANES else m_new[:, 0:n]
    p = jnp.exp2(s - m_wide)
    l_scr[...] = alpha * l_scr[...] + jnp.sum(p, -1, keepdims=True)
    m_scr[...] = m_new
    return p, alpha


def _softmax_init(m_scr, l_scr, acc_scr):
    m_scr[...] = jnp.full(m_scr.shape, -jnp.inf, F32)
    l_scr[...] = jnp.zeros(l_scr.shape, F32)
    acc_scr[...] = jnp.zeros(acc_scr.shape, F32)


def _flash_kernel(qi_ref, ki_ref, q_ref, k_ref, kt_ref, o_ref, m_scr, l_scr, acc_scr, *, tq, tk):
    step_i = pl.program_id(1)
    qi = qi_ref[step_i]
    ki = ki_ref[step_i]
    rows = tq * MLA_HEADS

    @pl.when(ki == 0)
    def _():
        _softmax_init(m_scr, l_scr, acc_scr)

    def step(masked, nkeys):
        s_t = jnp.dot(k_ref[0, 0:nkeys, :], q_ref[0, 0], preferred_element_type=F32)
        if masked:
            k_pos = ki * tk + lax.broadcasted_iota(jnp.int32, (nkeys, rows), 0)
            q_pos = qi * tq + (lax.broadcasted_iota(jnp.int32, (nkeys, rows), 1) & (tq - 1))
            s_t = jnp.where(k_pos <= q_pos, s_t, -jnp.inf)
        m_prev = m_scr[...]
        m_new = jnp.maximum(m_prev, jnp.max(s_t, axis=0, keepdims=True))
        alpha = jnp.exp2(m_prev - m_new)
        p_t = jnp.exp2(s_t - m_new)
        l_scr[...] = alpha * l_scr[...] + jnp.sum(p_t, axis=0, keepdims=True)
        m_scr[...] = m_new
        acc_scr[...] = acc_scr[...] * alpha + jnp.dot(kt_ref[:, 0:nkeys], p_t.astype(BF16),
                                                      preferred_element_type=F32)

    crosses = (ki + 1) * tk > qi * tq + 1
    short = (qi + 1) * tq - ki * tk
    if tq < tk and tk % tq == 0:
        use_short = crosses & (short == tq)

        @pl.when(use_short)
        def _():
            step(True, tq)
    else:
        use_short = False

    @pl.when(crosses & jnp.logical_not(use_short))
    def _():
        step(True, tk)

    @pl.when(jnp.logical_not(crosses))
    def _():
        step(False, tk)

    @pl.when(ki == ((qi + 1) * tq - 1) // tk)
    def _():
        o_t = acc_scr[...] * (1.0 / l_scr[...])
        o_ref[0] = jnp.transpose(o_t).astype(o_ref.dtype).reshape(MLA_HEADS, tq, MLA_KV_RANK)


def _flash_attention(q_t, kcat, kt, *, tq, tk):
    b, nq, _, rows = q_t.shape
    t = nq * tq
    nkb = t // tk
    kern = functools.partial(_flash_kernel, tq=tq, tk=tk)
    pairs = [(qi, ki) for qi in range(nq) for ki in range(((qi + 1) * tq - 1) // tk + 1)]
    qi_arr = jnp.asarray([p[0] for p in pairs], jnp.int32)
    ki_arr = jnp.asarray([p[1] for p in pairs], jnp.int32)
    grid_spec = pltpu.PrefetchScalarGridSpec(
        num_scalar_prefetch=2, grid=(b, len(pairs)),
        in_specs=[pl.BlockSpec((1, 1, QK_WIDTH, rows), lambda i, s, qa, ka: (i, qa[s], 0, 0)),
                  pl.BlockSpec((1, tk, QK_WIDTH), lambda i, s, qa, ka: (i, ka[s], 0)),
                  pl.BlockSpec((MLA_KV_RANK, tk), lambda i, s, qa, ka: (0, i * nkb + ka[s]))],
        out_specs=pl.BlockSpec((1, MLA_HEADS, tq, MLA_KV_RANK), lambda i, s, qa, ka: (i, 0, qa[s], 0)),
        scratch_shapes=[pltpu.VMEM((1, rows), F32), pltpu.VMEM((1, rows), F32),
                        pltpu.VMEM((MLA_KV_RANK, rows), F32)])
    return pl.pallas_call(
        kern, grid_spec=grid_spec,
        out_shape=jax.ShapeDtypeStruct((b, MLA_HEADS, t, MLA_KV_RANK), BF16),
        compiler_params=_cparams(("parallel", "arbitrary")), name="mla_flash",
    )(qi_arr, ki_arr, q_t, kcat, kt)


def _paged_kernel(pt_ref, q_ref, knew_ref, sel_ref, rowsel_ref, rowselt_ref, lat_hbm, rope_hbm, o_ref,
                  latbuf, ropebuf, sems, m_scr, l_scr, acc_scr, *, tpad, tv, layer, ngroups):
    bi = pl.program_id(0)
    nb = pl.num_programs(0)
    npg = PAGES_PER_GROUP
    rows = tpad * MLA_HEADS
    rows_v = tv * MLA_HEADS
    reps = MLA_KV_RANK // LANES

    def copies(seq, grp, slot):
        out = []
        for i in range(npg):
            page = pt_ref[seq, grp * npg + i]
            dst = pl.ds(i * PAGE_SIZE, PAGE_SIZE)
            out.append(pltpu.make_async_copy(lat_hbm.at[layer, page], latbuf.at[slot, dst], sems.at[0, slot, i]))
            out.append(pltpu.make_async_copy(rope_hbm.at[layer, page], ropebuf.at[slot, :, dst], sems.at[1, slot, i]))
        return out

    def start(seq, grp, slot):
        for cp in copies(seq, grp, slot):
            cp.start()

    def wait(seq, grp, slot):
        for cp in copies(seq, grp, slot):
            cp.wait()

    @pl.when(bi == 0)
    def _():
        for g in range(ngroups):
            start(0, g, g)

    for a in range(2):
        _softmax_init(m_scr.at[a], l_scr.at[a], acc_scr.at[a])
    q_v = jnp.dot(rowsel_ref[...], q_ref[0].reshape(rows, QK_WIDTH), preferred_element_type=F32).astype(BF16)
    q_lat = q_v[:, 0:MLA_KV_RANK]
    q_rd = jnp.dot(q_v[:, MLA_KV_RANK:QK_WIDTH], sel_ref[...], preferred_element_type=F32).astype(BF16)

    nt = (((1,), (1,)), ((), ()))

    def accumulate(a, s, v_b):
        p, alpha = _softmax_rows(s, m_scr.at[a], l_scr.at[a])
        yield
        acc_scr[a] = (acc_scr[a] * jnp.concatenate([alpha] * reps, axis=1)
                      + jnp.dot(p.astype(BF16), v_b, preferred_element_type=F32))

    def consume(a, slot):
        lat_b = latbuf[slot].astype(BF16)
        rope_b = ropebuf[slot].astype(BF16)
        yield
        s = (lax.dot_general(q_lat, lat_b, nt, preferred_element_type=F32)
             + jnp.dot(q_rd, rope_b, preferred_element_type=F32))
        yield
        yield from accumulate(a, s, lat_b)

    k_b = knew_ref[0]
    s2 = lax.dot_general(q_v, k_b, nt, preferred_element_type=F32)
    q_pos = lax.rem(lax.broadcasted_iota(jnp.int32, (rows_v, tpad), 0), tv)
    k_pos = lax.broadcasted_iota(jnp.int32, (rows_v, tpad), 1)
    _lockstep([accumulate(0, jnp.where(k_pos <= q_pos, s2, -jnp.inf), k_b[:, 0:MLA_KV_RANK])])

    for g0 in range(0, ngroups, 2):
        wait(bi, g0, g0)
        wait(bi, g0 + 1, g0 + 1)
        _lockstep([consume(0, g0), consume(1, g0 + 1)])

        @pl.when(bi + 1 < nb)
        def _():
            start(bi + 1, g0, g0)
            start(bi + 1, g0 + 1, g0 + 1)

    m0, m1 = m_scr[0], m_scr[1]
    m = jnp.maximum(m0, m1)
    a0, a1 = jnp.exp2(m0 - m), jnp.exp2(m1 - m)
    inv_l = 1.0 / (a0 * l_scr[0] + a1 * l_scr[1])
    wide = lambda v: jnp.concatenate([v] * reps, axis=1)
    o_v = ((acc_scr[0] * wide(a0) + acc_scr[1] * wide(a1)) * wide(inv_l)).astype(BF16)
    o = jnp.dot(rowselt_ref[...], o_v, preferred_element_type=F32)
    o_ref[0] = o.astype(o_ref.dtype).reshape(MLA_HEADS, tpad, MLA_KV_RANK)


def _row_select_matrix(tpad, tv):
    s = np.zeros((MLA_HEADS * tv, MLA_HEADS * tpad), np.float32)
    for h in range(MLA_HEADS):
        for t in range(tv):
            s[h * tv + t, h * tpad + t] = 1.0
    return s


def _paged_attention(q, knew, cache_lat, cache_rope_t, page_table, sel, *, tpad, tv, layer):
    b = q.shape[0]
    n_pages = page_table.shape[1]
    npg = PAGES_PER_GROUP
    ngroups = n_pages // npg
    assert n_pages % (2 * npg) == 0, "page groups are consumed in slot pairs"
    rows = tpad * MLA_HEADS
    rows_v = tv * MLA_HEADS
    gk = npg * PAGE_SIZE
    rowsel = _row_select_matrix(tpad, tv)
    kern = functools.partial(_paged_kernel, tpad=tpad, tv=tv, layer=layer, ngroups=ngroups)
    stat = pltpu.VMEM((2, rows_v, LANES), F32)
    grid_spec = pltpu.PrefetchScalarGridSpec(
        num_scalar_prefetch=1, grid=(b,),
        in_specs=[pl.BlockSpec((1, MLA_HEADS, tpad, QK_WIDTH), lambda bi, pt: (bi, 0, 0, 0)),
                  pl.BlockSpec((1, tpad, QK_WIDTH), lambda bi, pt: (bi, 0, 0)),
                  pl.BlockSpec(sel.shape, lambda bi, pt: (0, 0)),
                  pl.BlockSpec((rows_v, rows), lambda bi, pt: (0, 0)),
                  pl.BlockSpec((rows, rows_v), lambda bi, pt: (0, 0)),
                  pl.BlockSpec(memory_space=pl.ANY), pl.BlockSpec(memory_space=pl.ANY)],
        out_specs=pl.BlockSpec((1, MLA_HEADS, tpad, MLA_KV_RANK), lambda bi, pt: (bi, 0, 0, 0)),
        scratch_shapes=[pltpu.VMEM((ngroups, gk, MLA_KV_RANK), F32), pltpu.VMEM((ngroups, MLA_ROPE, gk), F32),
                        pltpu.SemaphoreType.DMA((2, ngroups, npg)),
                        stat, stat, pltpu.VMEM((2, rows_v, MLA_KV_RANK), F32)])
    return pl.pallas_call(
        kern, grid_spec=grid_spec,
        out_shape=jax.ShapeDtypeStruct((b, MLA_HEADS, tpad, MLA_KV_RANK), BF16),
        compiler_params=_cparams(("arbitrary",)), name="mla_paged",
    )(page_table, q, knew, sel, jnp.asarray(rowsel, BF16), jnp.asarray(rowsel.T, BF16), cache_lat, cache_rope_t)


def _out_kernel(x_ref, yssd_ref, olat_ref, gate_ref, ygdn_ref, wuv_ref, wout_ref, g_ref, b_ref, o_ref, *, tm):
    y_mla = []
    for pair in range(MLA_HEADS // 2):
        o_pair = None
        for i in range(2):
            o_h = olat_ref[:, 2 * pair + i].reshape(tm, MLA_KV_RANK)
            part = jnp.dot(o_h, wuv_ref[pair, i * MLA_KV_RANK:(i + 1) * MLA_KV_RANK, :], preferred_element_type=F32)
            o_pair = part if o_pair is None else o_pair + part
        y_mla.append(o_pair * gate_ref[:, pair * LANES:(pair + 1) * LANES])
    y_mla = jnp.concatenate(y_mla, axis=1)
    mm = (_bdot(yssd_ref[...], wout_ref[0:SSD_WIDTH, :])
          + _bdot(y_mla, wout_ref[SSD_WIDTH:SSD_WIDTH + MLA_WIDTH, :])
          + _bdot(ygdn_ref[...], wout_ref[SSD_WIDTH + MLA_WIDTH:MIX_WIDTH, :]))
    o_ref[...] = _layer_norm_rows(DEEPNORM_ALPHA * x_ref[...] + mm, g_ref[...], b_ref[...])


def _out_proj(x, y_ssd, o_lat, gate, y_gdn, l, wuv_pairs, wout, g, b, tm):
    m = x.shape[0]
    t = o_lat.shape[2]
    row = lambda i: (i, 0)
    nb, tt, hm_map = _head_major_block(t, tm)
    return pl.pallas_call(
        functools.partial(_out_kernel, tm=tm), grid=(m // tm,),
        in_specs=[pl.BlockSpec((tm, D_MODEL), row), pl.BlockSpec((tm, SSD_WIDTH), row),
                  pl.BlockSpec((nb, MLA_HEADS, tt, MLA_KV_RANK), hm_map), pl.BlockSpec((tm, MLA_WIDTH), row),
                  pl.BlockSpec((tm, GDN_WIDTH), row),
                  _layer_spec(wuv_pairs, l), _layer_spec(wout, l), _layer_spec(g, l), _layer_spec(b, l)],
        out_specs=pl.BlockSpec((tm, D_MODEL), row),
        out_shape=jax.ShapeDtypeStruct((m, D_MODEL), F32),
        compiler_params=_cparams(("parallel",)), name="out_proj",
    )(x, y_ssd, o_lat, gate, y_gdn, wuv_pairs, wout, g, b)


def _pad_last(v, n=LANES, before=0):
    return jnp.pad(v, ((0, 0),) * (v.ndim - 1) + ((before, n - before - v.shape[-1]),))


def _prep_weights(w_in, ssd_conv_w, ssd_conv_b, ssd_dt_bias, ssd_a_log, ssd_d, ssd_norm_w,
                  mla_q_norm_w, mla_w_uq, mla_kv_norm_w, mla_w_uk, mla_w_uv,
                  gdn_conv_w, gdn_dt_bias, gdn_a_log, gdn_norm_w, w_out, ln_g, ln_b):
    depth = w_in.shape[0]
    (w_z, w_xbc, w_dt, w_cq, w_ckv, w_kr, w_gate, w_qkv, w_gz, w_gb, w_ga) = jnp.split(w_in, IN_OFFSETS, axis=2)
    w_ssd = [w_z, w_xbc, _pad_last(w_dt)]
    kr_tiled = jnp.concatenate([jnp.tile(w_kr[..., :ROPE_HALF], (1, 1, MLA_HEADS)),
                                jnp.tile(w_kr[..., ROPE_HALF:], (1, 1, MLA_HEADS))], axis=2)
    w_mla = [w_cq, w_ckv, kr_tiled, w_gate]
    w_gdn = [w_qkv, w_gz, _pad_last(jnp.concatenate([w_gb, w_ga], axis=2))]

    uq = mla_w_uq.reshape(depth, MLA_Q_RANK, MLA_HEADS, MLA_NOPE + MLA_ROPE)
    wuq = jnp.concatenate([uq[..., :MLA_NOPE].reshape(depth, MLA_Q_RANK, -1),
                           uq[..., MLA_NOPE:MLA_NOPE + ROPE_HALF].reshape(depth, MLA_Q_RANK, -1),
                           uq[..., MLA_NOPE + ROPE_HALF:].reshape(depth, MLA_Q_RANK, -1)], axis=2).astype(BF16)
    hp = MLA_HEADS // 2
    uk = jnp.transpose(mla_w_uk, (0, 2, 3, 1)).reshape(depth, hp, 2, MLA_NOPE, MLA_KV_RANK)
    uv = jnp.transpose(mla_w_uv, (0, 2, 1, 3)).reshape(depth, hp, 2, MLA_KV_RANK, MLA_V)
    zk = jnp.zeros_like(uk[:, :, 0])
    zv = jnp.zeros_like(uv[:, :, 0])
    wuk_pairs = jnp.concatenate([jnp.concatenate([uk[:, :, 0], zk], axis=-1),
                                 jnp.concatenate([zk, uk[:, :, 1]], axis=-1)], axis=-2).astype(BF16)
    wuv_pairs = jnp.concatenate([jnp.concatenate([uv[:, :, 0], zv], axis=-1),
                                 jnp.concatenate([zv, uv[:, :, 1]], axis=-1)], axis=-2).astype(BF16)

    row = lambda v: v[:, None, :]
    return dict(
        w_in=jnp.concatenate(w_ssd + w_mla + w_gdn, axis=2).astype(BF16),
        ssd_cw=ssd_conv_w, ssd_cb=row(ssd_conv_b),
        ssd_dtb=_pad_last(row(ssd_dt_bias)), ssd_alog=_pad_last(row(ssd_a_log)),
        ssd_dexp=row(jnp.repeat(ssd_d, SSD_HEAD_DIM, axis=1)), ssd_nw=row(ssd_norm_w),
        qnw=row(mla_q_norm_w), kvnw=row(mla_kv_norm_w), wuq=wuq,
        wuk_pairs=wuk_pairs, wuv_pairs=wuv_pairs,
        gdn_cw=gdn_conv_w, gdn_bias=_pad_last(row(gdn_dt_bias), before=GDN_HEADS),
        gdn_alog=_pad_last(row(gdn_a_log), before=GDN_HEADS), gdn_nw=row(gdn_norm_w),
        w_out=w_out.astype(BF16), ln_g=row(ln_g), ln_b=row(ln_b))


def _rope_tables(pos):
    inv = ROPE_THETA ** (-jnp.arange(ROPE_HALF, dtype=F32) / ROPE_HALF)
    ang = pos.astype(F32)[:, None] * inv[None, :]
    return jnp.tile(jnp.cos(ang), (1, MLA_HEADS)), jnp.tile(jnp.sin(ang), (1, MLA_HEADS))


def _head_expand_matrix():
    e = np.zeros((LANES, SSD_WIDTH), np.float32)
    for h in range(SSD_HEADS):
        e[h, h * SSD_HEAD_DIM:(h + 1) * SSD_HEAD_DIM] = 1.0
    return jnp.asarray(e, BF16)


def _rope_select_matrix():
    s = np.zeros((2 * LANES, MLA_ROPE), np.float32)
    for j in range(2 * LANES):
        s[j, (j // LANES) * ROPE_HALF + (j % ROPE_HALF)] = 1.0
    return jnp.asarray(s, BF16)


def _tail8(conv_state):
    return jnp.pad(conv_state, ((0, 0), (0, 0), (SUBLANES - (CONV_WIDTH - 1), 0), (0, 0)))


def _trunk(x, pos, tv, ssd_conv, ssd_state, gdn_conv, gdn_state, emb_g, emb_b, w, cfg, paged=None):
    b, t, _ = x.shape
    m = b * t
    tm = cfg["tm"]
    cos_t, sin_t = _rope_tables(pos)
    if cos_t.shape[0] < tm:
        reps = tm // cos_t.shape[0]
        cos_t, sin_t = jnp.tile(cos_t, (reps, 1)), jnp.tile(sin_t, (reps, 1))
    emat = _head_expand_matrix()
    sel = _rope_select_matrix()
    ssd_c8_in, gdn_c8_in = _tail8(ssd_conv), _tail8(gdn_conv)
    gw = SSD_WIDTH // SSD_GROUPS
    ssd_h_in = ssd_state.reshape(DEPTH, b, SSD_GROUPS, gw, SSD_STATE)
    h = x.reshape(m, D_MODEL)
    new_states = []
    for l in range(DEPTH):
        if l == 0:
            h, p_ssd, p_dt, p_mla, p_gdn, p_ba = _project(h, w["w_in"], l, cfg["tm_proj"], emb_ln=(emb_g, emb_b))
        else:
            p_ssd, p_dt, p_mla, p_gdn, p_ba = _project(h, w["w_in"], l, cfg["tm_proj"])

        y_ssd, ssd_c8, ssd_h = _ssd_scan(
            p_ssd.reshape(b, t, -1), p_dt.reshape(b, t, LANES), ssd_c8_in, ssd_h_in, l,
            w["ssd_cw"], w["ssd_cb"], w["ssd_dtb"], w["ssd_alog"], w["ssd_dexp"], w["ssd_nw"], emat,
            L=cfg["ssd_chunk"], tv=min(tv, cfg["ssd_chunk"]))
        y_gdn, gdn_c8, gdn_s = _gdn_scan(
            p_gdn.reshape(b, t, -1), p_ba.reshape(b, t, LANES), gdn_c8_in, gdn_state, l,
            w["gdn_cw"], w["gdn_bias"], w["gdn_alog"], w["gdn_nw"],
            L=cfg["gdn_chunk"], tv=min(tv, cfg["gdn_chunk"]))

        q, kcat, kt, ckv, kr_t, gate = _mla_prep(p_mla, cos_t, sin_t, l, w["qnw"], w["kvnw"], w["wuq"],
                                                 w["wuk_pairs"], b, t, tm, tq_t=cfg["tq"] if paged is None else None)
        kcat = kcat.reshape(b, t, QK_WIDTH)
        if paged is None:
            o_lat = _flash_attention(q, kcat, kt, tq=cfg["tq"], tk=cfg["tk"])
        else:
            cache_lat, cache_rope_t, page_table = paged
            o_lat = _paged_attention(q, kcat, cache_lat, cache_rope_t, page_table, sel, tpad=t, tv=tv, layer=l)

        h = _out_proj(h, y_ssd.reshape(m, SSD_WIDTH), o_lat, gate, y_gdn.reshape(m, GDN_WIDTH), l,
                      w["wuv_pairs"], w["w_out"], w["ln_g"], w["ln_b"], tm)

        kr = jnp.concatenate([kr_t[:, 0:ROPE_HALF], kr_t[:, LANES:LANES + ROPE_HALF]], axis=1)
        sl = slice(SUBLANES - (CONV_WIDTH - 1), SUBLANES)
        new_states.append((ckv.reshape(b, t, MLA_KV_RANK), kr.reshape(b, t, MLA_ROPE), ssd_c8[:, sl],
                           ssd_h.reshape(b, SSD_HEADS, SSD_HEAD_DIM, SSD_STATE), gdn_c8[:, sl], gdn_s))
    return h.reshape(b, t, D_MODEL), tuple(jnp.stack(s) for s in zip(*new_states))


def kernel(x_prompt, x_sample, cache_kv_latent, cache_k_rope, state_ssd_conv, state_ssd, state_gdn_conv, state_gdn, page_table, emb_ln_g, emb_ln_b, w_in, ssd_conv_w, ssd_conv_b, ssd_dt_bias, ssd_a_log, ssd_d, ssd_norm_w, mla_q_norm_w, mla_w_uq, mla_kv_norm_w, mla_w_uk, mla_w_uv, gdn_conv_w, gdn_dt_bias, gdn_a_log, gdn_norm_w, w_out, ln_g, ln_b):
    weights = _prep_weights(w_in, ssd_conv_w, ssd_conv_b, ssd_dt_bias, ssd_a_log, ssd_d, ssd_norm_w,
                            mla_q_norm_w, mla_w_uq, mla_kv_norm_w, mla_w_uk, mla_w_uv,
                            gdn_conv_w, gdn_dt_bias, gdn_a_log, gdn_norm_w, w_out, ln_g, ln_b)

    bp, tp, _ = x_prompt.shape
    zeros = lambda *s: jnp.zeros(s, F32)
    ssd_chunk = SSD_CHUNK if tp % SSD_CHUNK == 0 else tp
    gdn_chunk = GDN_CHUNK if tp % GDN_CHUNK == 0 else tp
    cfg_p = dict(tm=min(1024, bp * tp), tm_proj=min(512, bp * tp), ssd_chunk=ssd_chunk, gdn_chunk=gdn_chunk,
                 tq=min(256, tp), tk=min(512, tp))
    y_prompt, st_p = _trunk(
        x_prompt, jnp.arange(tp), tp,
        zeros(DEPTH, bp, CONV_WIDTH - 1, SSD_CONV_DIM), zeros(DEPTH, bp, SSD_HEADS, SSD_HEAD_DIM, SSD_STATE),
        zeros(DEPTH, bp, CONV_WIDTH - 1, GDN_CONV_DIM), zeros(DEPTH, bp, GDN_HEADS, GDN_HEAD_DIM, GDN_HEAD_DIM),
        emb_ln_g, emb_ln_b, weights, cfg_p)

    bs, ts, _ = x_sample.shape
    tpad = -(-ts // BF16_ROWS) * BF16_ROWS
    past_len = page_table.shape[1] * PAGE_SIZE
    xs = jnp.pad(x_sample, ((0, 0), (0, tpad - ts), (0, 0)))
    cfg_s = dict(tm=min(512, bs * tpad), tm_proj=min(512, bs * tpad), ssd_chunk=tpad, gdn_chunk=tpad,
                 tq=tpad, tk=tpad)
    y_s, st_s = _trunk(
        xs, past_len + jnp.arange(tpad), ts, state_ssd_conv, state_ssd, state_gdn_conv, state_gdn,
        emb_ln_g, emb_ln_b, weights, cfg_s,
        paged=(cache_kv_latent, jnp.swapaxes(cache_k_rope, 2, 3), page_table))
    y_sample = y_s[:, :ts]
    s_lat, s_rope, s_ssd_conv, s_ssd, s_gdn_conv, s_gdn = st_s
    return (y_prompt, y_sample) + tuple(st_p) + (s_lat[:, :, :ts], s_rope[:, :, :ts], s_ssd_conv, s_ssd, s_gdn_conv, s_gdn)
```

```python
import functools
import math

import jax
import jax.numpy as jnp
import numpy as np
from jax import lax
from jax.experimental import pallas as pl
from jax.experimental.pallas import tpu as pltpu

F32 = jnp.float32
BF16 = jnp.bfloat16

D_MODEL = 1024
CONV_WIDTH = 4
SSD_HEADS = 16
SSD_HEAD_DIM = 64
SSD_WIDTH = SSD_HEADS * SSD_HEAD_DIM
SSD_GROUPS = 2
SSD_STATE = 128
SSD_CONV_DIM = SSD_WIDTH + 2 * SSD_GROUPS * SSD_STATE
SSD_CHUNK = 128
MLA_HEADS = 8
MLA_NOPE = 64
MLA_ROPE = 32
MLA_V = 64
MLA_WIDTH = MLA_HEADS * MLA_V
MLA_Q_RANK = 384
MLA_KV_RANK = 256
MLA_SCALE = (MLA_NOPE + MLA_ROPE) ** -0.5
ROPE_THETA = 10000.0
GDN_HEADS = 4
GDN_HEAD_DIM = 128
GDN_WIDTH = GDN_HEADS * GDN_HEAD_DIM
GDN_CONV_DIM = 3 * GDN_WIDTH
GDN_CHUNK = 64
MIX_WIDTH = SSD_WIDTH + MLA_WIDTH + GDN_WIDTH
IN_SIZES = (SSD_WIDTH, SSD_CONV_DIM, SSD_HEADS, MLA_Q_RANK, MLA_KV_RANK, MLA_ROPE, MLA_WIDTH,
            GDN_CONV_DIM, GDN_WIDTH, GDN_HEADS, GDN_HEADS)
IN_OFFSETS = tuple(int(o) for o in np.cumsum(IN_SIZES)[:-1])
DEPTH = 2
DEEPNORM_ALPHA = (2 * DEPTH) ** 0.25
LN_EPS = 1e-5
RMS_EPS = 1e-6
L2_EPS = 1e-6
PAGE_SIZE = 128

LANES = 128
SUBLANES = 8
BF16_ROWS = 16
ROPE_HALF = MLA_ROPE // 2
P_SSD = SSD_WIDTH + SSD_CONV_DIM + LANES
P_MLA = MLA_Q_RANK + MLA_KV_RANK + 2 * LANES + MLA_WIDTH
P_GDN = GDN_CONV_DIM + GDN_WIDTH + LANES
QK_WIDTH = MLA_KV_RANK + 2 * LANES
Q_SCALE = MLA_SCALE * math.log2(math.e)
VMEM_LIMIT = 56 * 1024 * 1024
PAGES_PER_GROUP = 32
GDN_BATCH_PER_STEP = 4
SSD_BATCH_PER_STEP = 4

def _cparams(sem):
    return pltpu.CompilerParams(dimension_semantics=sem, vmem_limit_bytes=VMEM_LIMIT)


def _bdot(a, b):
    return jnp.dot(a.astype(BF16), b.astype(BF16), preferred_element_type=F32)


def _bdot_nt(a, b):
    return lax.dot_general(a.astype(BF16), b.astype(BF16), (((1,), (1,)), ((), ())),
                           preferred_element_type=F32)


def _fdot(a, b):
    return jnp.dot(a, b, precision=lax.Precision.HIGHEST, preferred_element_type=F32)


def _silu(x):
    return x * (1.0 / (1.0 + jnp.exp(-x)))


def _softplus(x):
    return jnp.maximum(x, 0.0) + jnp.log1p(jnp.exp(-jnp.abs(x)))


def _const_spec(shape):
    nd = len(shape)
    return pl.BlockSpec(shape, lambda *_: (0,) * nd)


def _layer_spec(stacked, l):
    rest = stacked.shape[1:]
    return pl.BlockSpec((None,) + rest, lambda *_: (l,) + (0,) * len(rest))


def _layer_norm_rows(x, g, b):
    mu = jnp.mean(x, -1, keepdims=True)
    xc = x - mu
    var = jnp.mean(xc * xc, -1, keepdims=True)
    return xc * lax.rsqrt(var + LN_EPS) * g + b


def _proj_kernel(x_ref, w_ref, *refs, embed):
    if embed:
        g_ref, b_ref, h_ref, ssd_ref, dt_ref, mla_ref, gdn_ref, ba_ref = refs
        h = _layer_norm_rows(x_ref[...], g_ref[...], b_ref[...])
        h_ref[...] = h
        x = h.astype(BF16)
    else:
        ssd_ref, dt_ref, mla_ref, gdn_ref, ba_ref = refs
        x = x_ref[...].astype(BF16)
    p = jnp.dot(x, w_ref[:, 0:P_SSD], preferred_element_type=F32)
    ssd_ref[...] = p[:, 0:P_SSD - LANES].astype(BF16)
    dt_ref[...] = p[:, P_SSD - LANES:P_SSD]
    mla_ref[...] = jnp.dot(x, w_ref[:, P_SSD:P_SSD + P_MLA], preferred_element_type=F32).astype(BF16)
    p = jnp.dot(x, w_ref[:, P_SSD + P_MLA:P_SSD + P_MLA + P_GDN], preferred_element_type=F32)
    gdn_ref[...] = p[:, 0:P_GDN - LANES].astype(BF16)
    ba_ref[...] = p[:, P_GDN - LANES:P_GDN]


def _project(x, w, l, tm, emb_ln=None):
    m, k = x.shape
    row = lambda i: (i, 0)
    widths = (P_SSD - LANES, LANES, P_MLA, P_GDN - LANES, LANES)
    dtypes = (BF16, F32, BF16, BF16, F32)
    operands, in_specs = [x, w], [pl.BlockSpec((tm, k), row), _layer_spec(w, l)]
    if emb_ln is not None:
        operands += [v.reshape(1, k) for v in emb_ln]
        in_specs += [_const_spec((1, k)), _const_spec((1, k))]
        widths, dtypes = (k,) + widths, (F32,) + dtypes
    return pl.pallas_call(
        functools.partial(_proj_kernel, embed=emb_ln is not None), grid=(m // tm,),
        in_specs=in_specs,
        out_specs=[pl.BlockSpec((tm, n), row) for n in widths],
        out_shape=[jax.ShapeDtypeStruct((m, n), d) for n, d in zip(widths, dtypes)],
        compiler_params=_cparams(("parallel",)), name="in_proj")(*operands)


def _conv_chunk(xbuf, x_new, cw_ref, L):
    xbuf[SUBLANES:SUBLANES + L, :] = x_new
    y = cw_ref[CONV_WIDTH - 1:CONV_WIDTH, :] * x_new
    for k in range(CONV_WIDTH - 1):
        off = SUBLANES - (CONV_WIDTH - 1) + k
        y = y + cw_ref[k:k + 1, :] * xbuf[off:off + L, :]
    return y


def _ssd_chunk(p_ref, dt_ref, cw_ref, cb_ref, dtb_ref, alog_ref, dexp_ref, nw_ref, e_ref,
               y_ref, cnew_ref, xbuf, ht, *, L, tv):
    z = p_ref[:, 0:SSD_WIDTH].astype(F32)
    xbc_raw = p_ref[:, SSD_WIDTH:SSD_WIDTH + SSD_CONV_DIM].astype(F32)
    dt_raw = dt_ref[...]

    xbc = _silu(_conv_chunk(xbuf, xbc_raw, cw_ref, L) + cb_ref[...])
    cnew_ref[...] = xbuf[tv:tv + SUBLANES, :]
    xbuf[0:SUBLANES, :] = xbuf[L:L + SUBLANES, :]
    yield

    xs = xbc[:, 0:SSD_WIDTH]
    gs = SSD_GROUPS * SSD_STATE
    bm = xbc[:, SSD_WIDTH:SSD_WIDTH + gs]
    cm = xbc[:, SSD_WIDTH + gs:SSD_WIDTH + 2 * gs]

    dt = _softplus(dt_raw + dtb_ref[...])
    if tv < L:
        rows = lax.broadcasted_iota(jnp.int32, (L, LANES), 0)
        dt = jnp.where(rows < tv, dt, 0.0)
    a = -jnp.exp(alog_ref[...])
    da = dt * a
    r_i = lax.broadcasted_iota(jnp.int32, (L, L), 0)
    c_i = lax.broadcasted_iota(jnp.int32, (L, L), 1)
    causal = c_i <= r_i
    tri = jnp.where(causal, 1.0, 0.0).astype(F32)
    acum = _fdot(tri, da)
    last = acum[L - 1:L, :]
    ea = jnp.exp(acum)
    wdec = jnp.exp(last - acum)

    def pad_rows(v):
        if L == LANES:
            return v
        return jnp.concatenate([v, jnp.zeros((LANES - L, v.shape[1]), v.dtype)], axis=0)

    acum_t = jnp.transpose(pad_rows(acum))
    yield

    def hilo(v):
        hi = v.astype(BF16)
        lo = (v - hi.astype(F32)).astype(BF16)
        return hi, lo

    parts = []
    for v in (dt, ea, wdec):
        parts.extend(hilo(v))
    stacked = jnp.concatenate(parts, axis=0)
    expanded = jnp.dot(stacked, e_ref[...], preferred_element_type=F32)
    dt_e = expanded[0:L] + expanded[L:2 * L]
    ea_e = expanded[2 * L:3 * L] + expanded[3 * L:4 * L]
    wd_e = expanded[4 * L:5 * L] + expanded[5 * L:6 * L]

    yield
    xdt = xs * dt_e
    xdt_b = xdt.astype(BF16)
    lane = lax.broadcasted_iota(jnp.int32, (L, LANES), 1)
    hpg = SSD_HEADS // SSD_GROUPS

    y_parts = []
    for g in range(SSD_GROUPS):
        bg = bm[:, g * SSD_STATE:(g + 1) * SSD_STATE]
        cg = cm[:, g * SSD_STATE:(g + 1) * SSD_STATE]
        cb = _bdot_nt(cg, bg)
        for j in range(hpg // 2):
            pair = g * (hpg // 2) + j
            xp = xdt_b[:, pair * LANES:(pair + 1) * LANES]
            ys = []
            for h in (2 * pair, 2 * pair + 1):
                seg = acum[:, h:h + 1] - acum_t[h:h + 1, 0:L]
                dec = jnp.exp(jnp.where(causal, seg, -jnp.inf))
                ys.append(_bdot(cb * dec, xp))
            y_parts.append(jnp.where(lane < SSD_HEAD_DIM, ys[0], ys[1]))
            yield
    y_in = jnp.concatenate(y_parts, axis=1)

    gw = hpg * SSD_HEAD_DIM
    y_st_parts = []
    for g in range(SSD_GROUPS):
        cg = cm[:, g * SSD_STATE:(g + 1) * SSD_STATE]
        y_st_parts.append(_bdot(cg, ht[g]))
    y_st = jnp.concatenate(y_st_parts, axis=1) * ea_e
    yield

    xw = (xdt * wd_e)
    for g in range(SSD_GROUPS):
        bg_t = jnp.transpose(pad_rows(bm[:, g * SSD_STATE:(g + 1) * SSD_STATE]))
        xw_g = pad_rows(xw[:, g * gw:(g + 1) * gw])
        ht[g] = ht[g] * ea_e[L - 1:L, g * gw:(g + 1) * gw] + _bdot(bg_t, xw_g)
    yield

    y = (y_in + y_st + dexp_ref[...] * xs) * _silu(z)
    outs = []
    for g in range(SSD_GROUPS):
        yg = y[:, g * gw:(g + 1) * gw]
        ms = jnp.mean(yg * yg, -1, keepdims=True)
        outs.append(yg * lax.rsqrt(ms + RMS_EPS) * nw_ref[:, g * gw:(g + 1) * gw])
    y_ref[...] = jnp.concatenate(outs, axis=1).astype(y_ref.dtype)


def _lockstep(gens):
    while gens:
        gens = [g for g in gens if next(g, StopIteration) is not StopIteration]


def _ssd_kernel(p_ref, dt_ref, cprev_ref, h0_ref, cw_ref, cb_ref, dtb_ref, alog_ref, dexp_ref, nw_ref, e_ref,
                y_ref, cnew_ref, hout_ref, xbuf, ht, *, L, tv, nbat):
    c = pl.program_id(1)
    nc = pl.num_programs(1)

    @pl.when(c == 0)
    def _():
        xbuf[:, 0:SUBLANES, :] = cprev_ref[...]
        for i in range(nbat):
            for g in range(SSD_GROUPS):
                ht[i, g] = jnp.transpose(h0_ref[i, g])

    _lockstep([_ssd_chunk(p_ref.at[i], dt_ref.at[i], cw_ref, cb_ref, dtb_ref, alog_ref, dexp_ref, nw_ref, e_ref,
                          y_ref.at[i], cnew_ref.at[i], xbuf.at[i], ht.at[i], L=L, tv=tv) for i in range(nbat)])

    @pl.when(c == nc - 1)
    def _():
        for i in range(nbat):
            for g in range(SSD_GROUPS):
                hout_ref[i, g] = jnp.transpose(ht[i, g])


def _ssd_scan(p_ssd, p_dt, conv_prev8, h0, l, cw, cb, dtb, alog, dexp, nw, emat, *, L, tv):
    b, t, _ = p_ssd.shape
    nc = t // L
    gw = SSD_WIDTH // SSD_GROUPS
    nbat = math.gcd(b, SSD_BATCH_PER_STEP)
    kern = functools.partial(_ssd_kernel, L=L, tv=tv, nbat=nbat)
    hshape = (nbat, SSD_GROUPS, gw, SSD_STATE)
    return pl.pallas_call(
        kern, grid=(b // nbat, nc),
        in_specs=[pl.BlockSpec((nbat, L, P_SSD - LANES), lambda i, c: (i, c, 0)),
                  pl.BlockSpec((nbat, L, LANES), lambda i, c: (i, c, 0)),
                  pl.BlockSpec((None, nbat, SUBLANES, SSD_CONV_DIM), lambda i, c: (l, i, 0, 0)),
                  pl.BlockSpec((None,) + hshape, lambda i, c: (l, i, 0, 0, 0)),
                  _layer_spec(cw, l), _layer_spec(cb, l), _layer_spec(dtb, l), _layer_spec(alog, l),
                  _layer_spec(dexp, l), _layer_spec(nw, l), _const_spec((LANES, SSD_WIDTH))],
        out_specs=[pl.BlockSpec((nbat, L, SSD_WIDTH), lambda i, c: (i, c, 0)),
                   pl.BlockSpec((nbat, SUBLANES, SSD_CONV_DIM), lambda i, c: (i, 0, 0)),
                   pl.BlockSpec(hshape, lambda i, c: (i, 0, 0, 0))],
        out_shape=[jax.ShapeDtypeStruct((b, t, SSD_WIDTH), BF16),
                   jax.ShapeDtypeStruct((b, SUBLANES, SSD_CONV_DIM), F32),
                   jax.ShapeDtypeStruct((b,) + hshape[1:], F32)],
        scratch_shapes=[pltpu.VMEM((nbat, L + SUBLANES, SSD_CONV_DIM), F32),
                        pltpu.VMEM((nbat, SSD_GROUPS, SSD_STATE, gw), F32)],
        compiler_params=_cparams(("parallel", "arbitrary")), name="ssd_scan",
    )(p_ssd, p_dt, conv_prev8, h0, cw, cb, dtb, alog, dexp, nw, emat)


def _gdn_chunk(p_ref, ba_ref, cw_ref, bias_ref, alog_ref, nw_ref, y_ref, cnew_ref, xbuf, st, *, L, tv):
    H = GDN_HEADS
    D = GDN_HEAD_DIM
    R = H * L
    S = max(R, LANES)

    qkv_raw = p_ref[:, 0:GDN_CONV_DIM].astype(F32)
    z = p_ref[:, GDN_CONV_DIM:GDN_CONV_DIM + GDN_WIDTH].astype(F32)
    ba = ba_ref[...]

    qkv = _silu(_conv_chunk(xbuf, qkv_raw, cw_ref, L))
    cnew_ref[...] = xbuf[tv:tv + SUBLANES, :]
    xbuf[0:SUBLANES, :] = xbuf[L:L + SUBLANES, :]
    yield

    beta_f = 1.0 / (1.0 + jnp.exp(-ba))
    g_f = -jnp.exp(alog_ref[...]) * _softplus(ba + bias_ref[...])
    if tv < L:
        rows = lax.broadcasted_iota(jnp.int32, (L, LANES), 0)
        beta_f = jnp.where(rows < tv, beta_f, 0.0)
        g_f = jnp.where(rows < tv, g_f, 0.0)
    r_i = lax.broadcasted_iota(jnp.int32, (L, L), 0)
    c_i = lax.broadcasted_iota(jnp.int32, (L, L), 1)
    tri = jnp.where(c_i <= r_i, 1.0, 0.0).astype(F32)
    gcum_f = _fdot(tri, g_f)
    glast_f = jnp.broadcast_to(gcum_f[L - 1:L, :], (L, LANES))

    def pad_s(v):
        if R == S:
            return v
        return jnp.concatenate([v, jnp.zeros((S - R, v.shape[1]), v.dtype)], axis=0)

    def stack(v):
        return pad_s(jnp.concatenate([v[:, h * D:(h + 1) * D] for h in range(H)], axis=0))

    def col(v, off):
        return pad_s(jnp.concatenate([v[:, off + h:off + h + 1] for h in range(H)], axis=0))

    q_s = stack(qkv[:, 0:GDN_WIDTH])
    k_s = stack(qkv[:, GDN_WIDTH:2 * GDN_WIDTH])
    v_s = stack(qkv[:, 2 * GDN_WIDTH:3 * GDN_WIDTH])
    z_s = stack(z)
    q_s = q_s * lax.rsqrt(jnp.sum(q_s * q_s, -1, keepdims=True) + L2_EPS) * (D ** -0.5)
    k_s = k_s * lax.rsqrt(jnp.sum(k_s * k_s, -1, keepdims=True) + L2_EPS)
    beta = col(beta_f, 0)
    gcum = col(gcum_f, H)
    glast = col(glast_f, H)
    yield

    cmat = jnp.broadcast_to(gcum, (S, S))
    diff = cmat - jnp.transpose(cmat)
    rs = lax.broadcasted_iota(jnp.int32, (S, S), 0)
    cs = lax.broadcasted_iota(jnp.int32, (S, S), 1)
    if L & (L - 1) == 0:
        sh = L.bit_length() - 1
        same = (rs >> sh) == (cs >> sh)
    else:
        same = (rs // L) == (cs // L)
    incl = same & (cs <= rs)
    strict = same & (cs < rs)
    dec = jnp.exp(jnp.where(incl, diff, -jnp.inf))

    kb = k_s * beta
    n_mat = -jnp.where(strict, _bdot_nt(kb, k_s) * dec, 0.0)
    eye = jnp.where(rs == cs, 1.0, 0.0).astype(F32)
    t_mat = eye + n_mat
    npow = n_mat
    yield
    span = 2
    while span < L:
        npow = _bdot(npow, npow)
        yield
        t_mat = t_mat + _bdot(t_mat, npow)
        yield
        span *= 2

    eg = jnp.exp(gcum)
    t_b = t_mat.astype(BF16)
    u = _bdot(t_b, v_s * beta)
    w = _bdot(t_b, kb * eg)
    qg = q_s * eg
    attn = jnp.where(incl, _bdot_nt(q_s, k_s) * dec, 0.0)
    yield
    ws = []
    qs_ = []
    for h in range(H):
        sh_b = st[h].astype(BF16)
        ws.append(_bdot(w[h * L:(h + 1) * L], sh_b))
        qs_.append(_bdot(qg[h * L:(h + 1) * L], sh_b))
    v_new = u - pad_s(jnp.concatenate(ws, axis=0))
    yield
    o = pad_s(jnp.concatenate(qs_, axis=0)) + _bdot(attn, v_new)

    kd = k_s * jnp.exp(glast - gcum)
    kd_t = jnp.transpose(kd).astype(BF16)
    row_head = lax.broadcasted_iota(jnp.int32, (S, D), 0)
    eg_last = jnp.exp(glast)
    for h in range(H):
        vm = jnp.where((row_head >= h * L) & (row_head < (h + 1) * L), v_new, 0.0)
        st[h] = st[h] * eg_last[h * L:h * L + 1, :] + _bdot(kd_t, vm)
    yield

    ms = jnp.mean(o * o, -1, keepdims=True)
    o = o * lax.rsqrt(ms + RMS_EPS) * nw_ref[...] * _silu(z_s)
    y_ref[...] = jnp.concatenate([o[h * L:(h + 1) * L] for h in range(H)], axis=1).astype(y_ref.dtype)


def _gdn_kernel(p_ref, ba_ref, cprev_ref, s0_ref, cw_ref, bias_ref, alog_ref, nw_ref,
                y_ref, cnew_ref, sout_ref, xbuf, st, *, L, tv, nbat):
    c = pl.program_id(1)
    nc = pl.num_programs(1)

    @pl.when(c == 0)
    def _():
        xbuf[:, 0:SUBLANES, :] = cprev_ref[...]
        st[...] = s0_ref[...]

    _lockstep([_gdn_chunk(p_ref.at[i], ba_ref.at[i], cw_ref, bias_ref, alog_ref, nw_ref, y_ref.at[i],
                          cnew_ref.at[i], xbuf.at[i], st.at[i], L=L, tv=tv) for i in range(nbat)])

    @pl.when(c == nc - 1)
    def _():
        sout_ref[...] = st[...]


def _gdn_scan(p_gdn, p_ba, conv_prev8, s0, l, cw, bias, alog, nw, *, L, tv):
    b, t, _ = p_gdn.shape
    nc = t // L
    nbat = math.gcd(b, GDN_BATCH_PER_STEP)
    kern = functools.partial(_gdn_kernel, L=L, tv=tv, nbat=nbat)
    sshape = (nbat, GDN_HEADS, GDN_HEAD_DIM, GDN_HEAD_DIM)
    return pl.pallas_call(
        kern, grid=(b // nbat, nc),
        in_specs=[pl.BlockSpec((nbat, L, P_GDN - LANES), lambda i, c: (i, c, 0)),
                  pl.BlockSpec((nbat, L, LANES), lambda i, c: (i, c, 0)),
                  pl.BlockSpec((None, nbat, SUBLANES, GDN_CONV_DIM), lambda i, c: (l, i, 0, 0)),
                  pl.BlockSpec((None,) + sshape, lambda i, c: (l, i, 0, 0, 0)),
                  _layer_spec(cw, l), _layer_spec(bias, l), _layer_spec(alog, l), _layer_spec(nw, l)],
        out_specs=[pl.BlockSpec((nbat, L, GDN_WIDTH), lambda i, c: (i, c, 0)),
                   pl.BlockSpec((nbat, SUBLANES, GDN_CONV_DIM), lambda i, c: (i, 0, 0)),
                   pl.BlockSpec(sshape, lambda i, c: (i, 0, 0, 0))],
        out_shape=[jax.ShapeDtypeStruct((b, t, GDN_WIDTH), BF16),
                   jax.ShapeDtypeStruct((b, SUBLANES, GDN_CONV_DIM), F32),
                   jax.ShapeDtypeStruct((b,) + sshape[1:], F32)],
        scratch_shapes=[pltpu.VMEM((nbat, L + SUBLANES, GDN_CONV_DIM), F32),
                        pltpu.VMEM(sshape, F32)],
        compiler_params=_cparams(("parallel", "arbitrary")), name="gdn_scan",
    )(p_gdn, p_ba, conv_prev8, s0, cw, bias, alog, nw)


def _mla_prep_kernel(p_ref, cos_ref, sin_ref, qnw_ref, kvnw_ref, wuq_ref, wuk_ref,
                     q_ref, kcat_ref, kt_ref, ckv_ref, kr_ref, gate_ref, *, nb, tt, tq_t):
    cq = p_ref[:, 0:MLA_Q_RANK].astype(F32)
    ckv = p_ref[:, MLA_Q_RANK:MLA_Q_RANK + MLA_KV_RANK].astype(F32)
    kr = p_ref[:, MLA_Q_RANK + MLA_KV_RANK:MLA_Q_RANK + MLA_KV_RANK + 2 * LANES].astype(F32)
    gate = p_ref[:, MLA_Q_RANK + MLA_KV_RANK + 2 * LANES:P_MLA].astype(F32)
    cos = cos_ref[...]
    sin = sin_ref[...]

    cqn = cq * lax.rsqrt(jnp.mean(cq * cq, -1, keepdims=True) + RMS_EPS) * qnw_ref[...]
    q = _bdot(cqn, wuq_ref[...])
    nope_w = MLA_HEADS * MLA_NOPE
    x1 = q[:, nope_w:nope_w + LANES]
    x2 = q[:, nope_w + LANES:nope_w + 2 * LANES]
    r_all = jnp.concatenate([x1 * cos - x2 * sin, x2 * cos + x1 * sin], axis=1) * Q_SCALE

    ckvn = ckv * lax.rsqrt(jnp.mean(ckv * ckv, -1, keepdims=True) + RMS_EPS) * kvnw_ref[...]
    k1 = kr[:, 0:LANES]
    k2 = kr[:, LANES:2 * LANES]
    kr_rot = jnp.concatenate([k1 * cos - k2 * sin, k2 * cos + k1 * sin], axis=1)
    ckv_ref[...] = ckvn
    kr_ref[...] = kr_rot
    kcat_ref[...] = jnp.concatenate([ckvn, kr_rot], axis=1).astype(BF16)
    kt_ref[...] = jnp.transpose(ckvn).astype(BF16)
    gate_ref[...] = _silu(gate).astype(gate_ref.dtype)

    tm = q.shape[0]
    if tq_t is None:
        lane = lax.broadcasted_iota(jnp.int32, (1, 2 * LANES), 1)
        head_of_lane = (lane & (LANES - 1)) >> 4
    else:
        r_all_t = jnp.transpose(r_all)
        row_i = lax.broadcasted_iota(jnp.int32, (2 * LANES, 1), 0)
        head_of_row = (row_i & (LANES - 1)) >> 4
    for pair in range(MLA_HEADS // 2):
        qn = q[:, pair * LANES:(pair + 1) * LANES] * Q_SCALE
        qlat = _bdot(qn, wuk_ref[pair])
        for i in range(2):
            h = 2 * pair + i
            ql = qlat[:, i * MLA_KV_RANK:(i + 1) * MLA_KV_RANK]
            if tq_t is None:
                qr = jnp.where(head_of_lane == h, r_all, 0.0).astype(BF16)
                q_ref[:, h, :, 0:MLA_KV_RANK] = ql.astype(BF16).reshape(nb, tt, MLA_KV_RANK)
                q_ref[:, h, :, MLA_KV_RANK:QK_WIDTH] = qr.reshape(nb, tt, 2 * LANES)
            else:
                ql_t = jnp.transpose(ql).astype(BF16)
                qr_t = jnp.where(head_of_row == h, r_all_t, 0.0).astype(BF16)
                for qb in range(tm // tq_t):
                    cols = slice(h * tq_t, (h + 1) * tq_t)
                    toks = slice(qb * tq_t, (qb + 1) * tq_t)
                    q_ref[0, qb, 0:MLA_KV_RANK, cols] = ql_t[:, toks]
                    q_ref[0, qb, MLA_KV_RANK:QK_WIDTH, cols] = qr_t[:, toks]


def _head_major_block(t, tm):
    if tm <= t:
        npb = t // tm
        return 1, tm, (lambda i: (i // npb, 0, i % npb, 0))
    return tm // t, t, (lambda i: (i, 0, 0, 0))


def _mla_prep(p_mla, cos_t, sin_t, l, qnw, kvnw, wuq, wuk_pairs, b, t, tm, tq_t=None):
    m = p_mla.shape[0]
    nt = cos_t.shape[0] // tm
    row = lambda i: (i, 0)
    tab = lambda i: (i % nt, 0)
    nb, tt, hm_map = _head_major_block(t, tm)
    kern = functools.partial(_mla_prep_kernel, nb=nb, tt=tt, tq_t=tq_t)
    if tq_t is None:
        q_spec = pl.BlockSpec((nb, MLA_HEADS, tt, QK_WIDTH), hm_map)
        q_shape = jax.ShapeDtypeStruct((b, MLA_HEADS, t, QK_WIDTH), BF16)
    else:
        npb = t // tm
        q_spec = pl.BlockSpec((1, tm // tq_t, QK_WIDTH, MLA_HEADS * tq_t), lambda i: (i // npb, i % npb, 0, 0))
        q_shape = jax.ShapeDtypeStruct((b, t // tq_t, QK_WIDTH, MLA_HEADS * tq_t), BF16)
    return pl.pallas_call(
        kern, grid=(m // tm,),
        in_specs=[pl.BlockSpec((tm, P_MLA), row), pl.BlockSpec((tm, LANES), tab), pl.BlockSpec((tm, LANES), tab),
                  _layer_spec(qnw, l), _layer_spec(kvnw, l), _layer_spec(wuq, l), _layer_spec(wuk_pairs, l)],
        out_specs=[q_spec, pl.BlockSpec((tm, QK_WIDTH), row),
                   pl.BlockSpec((MLA_KV_RANK, tm), lambda i: (0, i)),
                   pl.BlockSpec((tm, MLA_KV_RANK), row), pl.BlockSpec((tm, 2 * LANES), row),
                   pl.BlockSpec((tm, MLA_WIDTH), row)],
        out_shape=[q_shape,
                   jax.ShapeDtypeStruct((m, QK_WIDTH), BF16), jax.ShapeDtypeStruct((MLA_KV_RANK, m), BF16),
                   jax.ShapeDtypeStruct((m, MLA_KV_RANK), F32), jax.ShapeDtypeStruct((m, 2 * LANES), F32),
                   jax.ShapeDtypeStruct((m, MLA_WIDTH), BF16)],
        compiler_params=_cparams(("parallel",)), name="mla_prep",
    )(p_mla, cos_t, sin_t, qnw, kvnw, wuq, wuk_pairs)


def _softmax_rows(s, m_scr, l_scr):
    n = s.shape[1]
    m_prev = m_scr[...]
    m_new = jnp.maximum(m_prev, jnp.max(s, -1, keepdims=True))
    alpha = jnp.exp2(m_prev - m_new)
    m_wide = jnp.concatenate([m_new] * (n // LANES), axis=1) if n >= LANES else m_new[:, 0:n]
    p = jnp.exp2(s - m_wide)
    l_scr[...] = alpha * l_scr[...] + jnp.sum(p, -1, keepdims=True)
    m_scr[...] = m_new
    return p, alpha


def _softmax_init(m_scr, l_scr, acc_scr):
    m_scr[...] = jnp.full(m_scr.shape, -jnp.inf, F32)
    l_scr[...] = jnp.zeros(l_scr.shape, F32)
    acc_scr[...] = jnp.zeros(acc_scr.shape, F32)


def _flash_kernel(qi_ref, ki_ref, q_ref, k_ref, kt_ref, o_ref, m_scr, l_scr, acc_scr, *, tq, tk):
    step_i = pl.program_id(1)
    qi = qi_ref[step_i]
    ki = ki_ref[step_i]
    rows = tq * MLA_HEADS

    @pl.when(ki == 0)
    def _():
        _softmax_init(m_scr, l_scr, acc_scr)

    def step(masked, nkeys):
        s_t = jnp.dot(k_ref[0, 0:nkeys, :], q_ref[0, 0], preferred_element_type=F32)
        if masked:
            k_pos = ki * tk + lax.broadcasted_iota(jnp.int32, (nkeys, rows), 0)
            q_pos = qi * tq + (lax.broadcasted_iota(jnp.int32, (nkeys, rows), 1) & (tq - 1))
            s_t = jnp.where(k_pos <= q_pos, s_t, -jnp.inf)
        m_prev = m_scr[...]
        m_new = jnp.maximum(m_prev, jnp.max(s_t, axis=0, keepdims=True))
        alpha = jnp.exp2(m_prev - m_new)
        p_t = jnp.exp2(s_t - m_new)
        l_scr[...] = alpha * l_scr[...] + jnp.sum(p_t, axis=0, keepdims=True)
        m_scr[...] = m_new
        acc_scr[...] = acc_scr[...] * alpha + jnp.dot(kt_ref[:, 0:nkeys], p_t.astype(BF16),
                                                      preferred_element_type=F32)

    crosses = (ki + 1) * tk > qi * tq + 1
    short = (qi + 1) * tq - ki * tk
    if tq < tk and tk % tq == 0:
        use_short = crosses & (short == tq)

        @pl.when(use_short)
        def _():
            step(True, tq)
    else:
        use_short = False

    @pl.when(crosses & jnp.logical_not(use_short))
    def _():
        step(True, tk)

    @pl.when(jnp.logical_not(crosses))
    def _():
        step(False, tk)

    @pl.when(ki == ((qi + 1) * tq - 1) // tk)
    def _():
        o_t = acc_scr[...] * (1.0 / l_scr[...])
        o_ref[0] = jnp.transpose(o_t).astype(o_ref.dtype).reshape(MLA_HEADS, tq, MLA_KV_RANK)


def _flash_attention(q_t, kcat, kt, *, tq, tk):
    b, nq, _, rows = q_t.shape
    t = nq * tq
    nkb = t // tk
    kern = functools.partial(_flash_kernel, tq=tq, tk=tk)
    pairs = [(qi, ki) for qi in range(nq) for ki in range(((qi + 1) * tq - 1) // tk + 1)]
    qi_arr = jnp.asarray([p[0] for p in pairs], jnp.int32)
    ki_arr = jnp.asarray([p[1] for p in pairs], jnp.int32)
    grid_spec = pltpu.PrefetchScalarGridSpec(
        num_scalar_prefetch=2, grid=(b, len(pairs)),
        in_specs=[pl.BlockSpec((1, 1, QK_WIDTH, rows), lambda i, s, qa, ka: (i, qa[s], 0, 0)),
                  pl.BlockSpec((1, tk, QK_WIDTH), lambda i, s, qa, ka: (i, ka[s], 0)),
                  pl.BlockSpec((MLA_KV_RANK, tk), lambda i, s, qa, ka: (0, i * nkb + ka[s]))],
        out_specs=pl.BlockSpec((1, MLA_HEADS, tq, MLA_KV_RANK), lambda i, s, qa, ka: (i, 0, qa[s], 0)),
        scratch_shapes=[pltpu.VMEM((1, rows), F32), pltpu.VMEM((1, rows), F32),
                        pltpu.VMEM((MLA_KV_RANK, rows), F32)])
    return pl.pallas_call(
        kern, grid_spec=grid_spec,
        out_shape=jax.ShapeDtypeStruct((b, MLA_HEADS, t, MLA_KV_RANK), BF16),
        compiler_params=_cparams(("parallel", "arbitrary")), name="mla_flash",
    )(qi_arr, ki_arr, q_t, kcat, kt)


def _paged_kernel(pt_ref, q_ref, knew_ref, sel_ref, rowsel_ref, rowselt_ref, lat_hbm, rope_hbm, o_ref,
                  latbuf, ropebuf, sems, m_scr, l_scr, acc_scr, *, tpad, tv, layer, ngroups):
    bi = pl.program_id(0)
    nb = pl.num_programs(0)
    npg = PAGES_PER_GROUP
    rows = tpad * MLA_HEADS
    rows_v = tv * MLA_HEADS
    reps = MLA_KV_RANK // LANES

    def copies(seq, grp, slot):
        out = []
        for i in range(npg):
            page = pt_ref[seq, grp * npg + i]
            dst = pl.ds(i * PAGE_SIZE, PAGE_SIZE)
            out.append(pltpu.make_async_copy(lat_hbm.at[layer, page], latbuf.at[slot, dst], sems.at[0, slot, i]))
            out.append(pltpu.make_async_copy(rope_hbm.at[layer, page], ropebuf.at[slot, :, dst], sems.at[1, slot, i]))
        return out

    def start(seq, grp, slot):
        for n, cp in enumerate(copies(seq, grp, slot)):
            cp.start(priority=n % 2)

    def wait(seq, grp, slot):
        for cp in copies(seq, grp, slot):
            cp.wait()

    @pl.when(bi == 0)
    def _():
        for g in range(ngroups):
            start(0, g, g)

    for a in range(2):
        _softmax_init(m_scr.at[a], l_scr.at[a], acc_scr.at[a])
    q_v = jnp.dot(rowsel_ref[...], q_ref[0].reshape(rows, QK_WIDTH), preferred_element_type=F32).astype(BF16)
    q_lat = q_v[:, 0:MLA_KV_RANK]
    q_rd = jnp.dot(q_v[:, MLA_KV_RANK:QK_WIDTH], sel_ref[...], preferred_element_type=F32).astype(BF16)

    nt = (((1,), (1,)), ((), ()))

    def accumulate(a, s, v_b):
        p, alpha = _softmax_rows(s, m_scr.at[a], l_scr.at[a])
        yield
        acc_scr[a] = (acc_scr[a] * jnp.concatenate([alpha] * reps, axis=1)
                      + jnp.dot(p.astype(BF16), v_b, preferred_element_type=F32))

    def consume(a, slot):
        lat_b = latbuf[slot].astype(BF16)
        rope_b = ropebuf[slot].astype(BF16)
        yield
        s = (lax.dot_general(q_lat, lat_b, nt, preferred_element_type=F32)
             + jnp.dot(q_rd, rope_b, preferred_element_type=F32))
        yield
        yield from accumulate(a, s, lat_b)

    k_b = knew_ref[0]
    s2 = lax.dot_general(q_v, k_b, nt, preferred_element_type=F32)
    q_pos = lax.rem(lax.broadcasted_iota(jnp.int32, (rows_v, tpad), 0), tv)
    k_pos = lax.broadcasted_iota(jnp.int32, (rows_v, tpad), 1)
    _lockstep([accumulate(0, jnp.where(k_pos <= q_pos, s2, -jnp.inf), k_b[:, 0:MLA_KV_RANK])])

    for g0 in range(0, ngroups, 2):
        wait(bi, g0, g0)
        wait(bi, g0 + 1, g0 + 1)
        _lockstep([consume(0, g0), consume(1, g0 + 1)])

        @pl.when(bi + 1 < nb)
        def _():
            start(bi + 1, g0, g0)
            start(bi + 1, g0 + 1, g0 + 1)

    m0, m1 = m_scr[0], m_scr[1]
    m = jnp.maximum(m0, m1)
    a0, a1 = jnp.exp2(m0 - m), jnp.exp2(m1 - m)
    inv_l = 1.0 / (a0 * l_scr[0] + a1 * l_scr[1])
    wide = lambda v: jnp.concatenate([v] * reps, axis=1)
    o_v = ((acc_scr[0] * wide(a0) + acc_scr[1] * wide(a1)) * wide(inv_l)).astype(BF16)
    o = jnp.dot(rowselt_ref[...], o_v, preferred_element_type=F32)
    o_ref[0] = o.astype(o_ref.dtype).reshape(MLA_HEADS, tpad, MLA_KV_RANK)


def _row_select_matrix(tpad, tv):
    s = np.zeros((MLA_HEADS * tv, MLA_HEADS * tpad), np.float32)
    for h in range(MLA_HEADS):
        for t in range(tv):
            s[h * tv + t, h * tpad + t] = 1.0
    return s


def _paged_attention(q, knew, cache_lat, cache_rope_t, page_table, sel, *, tpad, tv, layer):
    b = q.shape[0]
    n_pages = page_table.shape[1]
    npg = PAGES_PER_GROUP
    ngroups = n_pages // npg
    assert n_pages % (2 * npg) == 0, "page groups are consumed in slot pairs"
    rows = tpad * MLA_HEADS
    rows_v = tv * MLA_HEADS
    gk = npg * PAGE_SIZE
    rowsel = _row_select_matrix(tpad, tv)
    kern = functools.partial(_paged_kernel, tpad=tpad, tv=tv, layer=layer, ngroups=ngroups)
    stat = pltpu.VMEM((2, rows_v, LANES), F32)
    grid_spec = pltpu.PrefetchScalarGridSpec(
        num_scalar_prefetch=1, grid=(b,),
        in_specs=[pl.BlockSpec((1, MLA_HEADS, tpad, QK_WIDTH), lambda bi, pt: (bi, 0, 0, 0)),
                  pl.BlockSpec((1, tpad, QK_WIDTH), lambda bi, pt: (bi, 0, 0)),
                  pl.BlockSpec(sel.shape, lambda bi, pt: (0, 0)),
                  pl.BlockSpec((rows_v, rows), lambda bi, pt: (0, 0)),
                  pl.BlockSpec((rows, rows_v), lambda bi, pt: (0, 0)),
                  pl.BlockSpec(memory_space=pl.ANY), pl.BlockSpec(memory_space=pl.ANY)],
        out_specs=pl.BlockSpec((1, MLA_HEADS, tpad, MLA_KV_RANK), lambda bi, pt: (bi, 0, 0, 0)),
        scratch_shapes=[pltpu.VMEM((ngroups, gk, MLA_KV_RANK), F32), pltpu.VMEM((ngroups, MLA_ROPE, gk), F32),
                        pltpu.SemaphoreType.DMA((2, ngroups, npg)),
                        stat, stat, pltpu.VMEM((2, rows_v, MLA_KV_RANK), F32)])
    return pl.pallas_call(
        kern, grid_spec=grid_spec,
        out_shape=jax.ShapeDtypeStruct((b, MLA_HEADS, tpad, MLA_KV_RANK), BF16),
        compiler_params=_cparams(("arbitrary",)), name="mla_paged",
    )(page_table, q, knew, sel, jnp.asarray(rowsel, BF16), jnp.asarray(rowsel.T, BF16), cache_lat, cache_rope_t)


def _out_kernel(x_ref, yssd_ref, olat_ref, gate_ref, ygdn_ref, wuv_ref, wout_ref, g_ref, b_ref, o_ref, *, tm):
    y_mla = []
    for pair in range(MLA_HEADS // 2):
        o_pair = None
        for i in range(2):
            o_h = olat_ref[:, 2 * pair + i].reshape(tm, MLA_KV_RANK)
            part = jnp.dot(o_h, wuv_ref[pair, i * MLA_KV_RANK:(i + 1) * MLA_KV_RANK, :], preferred_element_type=F32)
            o_pair = part if o_pair is None else o_pair + part
        y_mla.append(o_pair * gate_ref[:, pair * LANES:(pair + 1) * LANES])
    y_mla = jnp.concatenate(y_mla, axis=1)
    mm = (_bdot(yssd_ref[...], wout_ref[0:SSD_WIDTH, :])
          + _bdot(y_mla, wout_ref[SSD_WIDTH:SSD_WIDTH + MLA_WIDTH, :])
          + _bdot(ygdn_ref[...], wout_ref[SSD_WIDTH + MLA_WIDTH:MIX_WIDTH, :]))
    o_ref[...] = _layer_norm_rows(DEEPNORM_ALPHA * x_ref[...] + mm, g_ref[...], b_ref[...])


def _out_proj(x, y_ssd, o_lat, gate, y_gdn, l, wuv_pairs, wout, g, b, tm):
    m = x.shape[0]
    t = o_lat.shape[2]
    row = lambda i: (i, 0)
    nb, tt, hm_map = _head_major_block(t, tm)
    return pl.pallas_call(
        functools.partial(_out_kernel, tm=tm), grid=(m // tm,),
        in_specs=[pl.BlockSpec((tm, D_MODEL), row), pl.BlockSpec((tm, SSD_WIDTH), row),
                  pl.BlockSpec((nb, MLA_HEADS, tt, MLA_KV_RANK), hm_map), pl.BlockSpec((tm, MLA_WIDTH), row),
                  pl.BlockSpec((tm, GDN_WIDTH), row),
                  _layer_spec(wuv_pairs, l), _layer_spec(wout, l), _layer_spec(g, l), _layer_spec(b, l)],
        out_specs=pl.BlockSpec((tm, D_MODEL), row),
        out_shape=jax.ShapeDtypeStruct((m, D_MODEL), F32),
        compiler_params=_cparams(("parallel",)), name="out_proj",
    )(x, y_ssd, o_lat, gate, y_gdn, wuv_pairs, wout, g, b)


def _pad_last(v, n=LANES, before=0):
    return jnp.pad(v, ((0, 0),) * (v.ndim - 1) + ((before, n - before - v.shape[-1]),))


def _prep_weights(w_in, ssd_conv_w, ssd_conv_b, ssd_dt_bias, ssd_a_log, ssd_d, ssd_norm_w,
                  mla_q_norm_w, mla_w_uq, mla_kv_norm_w, mla_w_uk, mla_w_uv,
                  gdn_conv_w, gdn_dt_bias, gdn_a_log, gdn_norm_w, w_out, ln_g, ln_b):
    depth = w_in.shape[0]
    (w_z, w_xbc, w_dt, w_cq, w_ckv, w_kr, w_gate, w_qkv, w_gz, w_gb, w_ga) = jnp.split(w_in, IN_OFFSETS, axis=2)
    w_ssd = [w_z, w_xbc, _pad_last(w_dt)]
    kr_tiled = jnp.concatenate([jnp.tile(w_kr[..., :ROPE_HALF], (1, 1, MLA_HEADS)),
                                jnp.tile(w_kr[..., ROPE_HALF:], (1, 1, MLA_HEADS))], axis=2)
    w_mla = [w_cq, w_ckv, kr_tiled, w_gate]
    w_gdn = [w_qkv, w_gz, _pad_last(jnp.concatenate([w_gb, w_ga], axis=2))]

    uq = mla_w_uq.reshape(depth, MLA_Q_RANK, MLA_HEADS, MLA_NOPE + MLA_ROPE)
    wuq = jnp.concatenate([uq[..., :MLA_NOPE].reshape(depth, MLA_Q_RANK, -1),
                           uq[..., MLA_NOPE:MLA_NOPE + ROPE_HALF].reshape(depth, MLA_Q_RANK, -1),
                           uq[..., MLA_NOPE + ROPE_HALF:].reshape(depth, MLA_Q_RANK, -1)], axis=2).astype(BF16)
    hp = MLA_HEADS // 2
    uk = jnp.transpose(mla_w_uk, (0, 2, 3, 1)).reshape(depth, hp, 2, MLA_NOPE, MLA_KV_RANK)
    uv = jnp.transpose(mla_w_uv, (0, 2, 1, 3)).reshape(depth, hp, 2, MLA_KV_RANK, MLA_V)
    zk = jnp.zeros_like(uk[:, :, 0])
    zv = jnp.zeros_like(uv[:, :, 0])
    wuk_pairs = jnp.concatenate([jnp.concatenate([uk[:, :, 0], zk], axis=-1),
                                 jnp.concatenate([zk, uk[:, :, 1]], axis=-1)], axis=-2).astype(BF16)
    wuv_pairs = jnp.concatenate([jnp.concatenate([uv[:, :, 0], zv], axis=-1),
                                 jnp.concatenate([zv, uv[:, :, 1]], axis=-1)], axis=-2).astype(BF16)

    row = lambda v: v[:, None, :]
    return dict(
        w_in=jnp.concatenate(w_ssd + w_mla + w_gdn, axis=2).astype(BF16),
        ssd_cw=ssd_conv_w, ssd_cb=row(ssd_conv_b),
        ssd_dtb=_pad_last(row(ssd_dt_bias)), ssd_alog=_pad_last(row(ssd_a_log)),
        ssd_dexp=row(jnp.repeat(ssd_d, SSD_HEAD_DIM, axis=1)), ssd_nw=row(ssd_norm_w),
        qnw=row(mla_q_norm_w), kvnw=row(mla_kv_norm_w), wuq=wuq,
        wuk_pairs=wuk_pairs, wuv_pairs=wuv_pairs,
        gdn_cw=gdn_conv_w, gdn_bias=_pad_last(row(gdn_dt_bias), before=GDN_HEADS),
        gdn_alog=_pad_last(row(gdn_a_log), before=GDN_HEADS), gdn_nw=row(gdn_norm_w),
        w_out=w_out.astype(BF16), ln_g=row(ln_g), ln_b=row(ln_b))


def _rope_tables(pos):
    inv = ROPE_THETA ** (-jnp.arange(ROPE_HALF, dtype=F32) / ROPE_HALF)
    ang = pos.astype(F32)[:, None] * inv[None, :]
    return jnp.tile(jnp.cos(ang), (1, MLA_HEADS)), jnp.tile(jnp.sin(ang), (1, MLA_HEADS))


def _head_expand_matrix():
    e = np.zeros((LANES, SSD_WIDTH), np.float32)
    for h in range(SSD_HEADS):
        e[h, h * SSD_HEAD_DIM:(h + 1) * SSD_HEAD_DIM] = 1.0
    return jnp.asarray(e, BF16)


def _rope_select_matrix():
    s = np.zeros((2 * LANES, MLA_ROPE), np.float32)
    for j in range(2 * LANES):
        s[j, (j // LANES) * ROPE_HALF + (j % ROPE_HALF)] = 1.0
    return jnp.asarray(s, BF16)


def _tail8(conv_state):
    return jnp.pad(conv_state, ((0, 0), (0, 0), (SUBLANES - (CONV_WIDTH - 1), 0), (0, 0)))


def _trunk(x, pos, tv, ssd_conv, ssd_state, gdn_conv, gdn_state, emb_g, emb_b, w, cfg, paged=None):
    b, t, _ = x.shape
    m = b * t
    tm = cfg["tm"]
    cos_t, sin_t = _rope_tables(pos)
    if cos_t.shape[0] < tm:
        reps = tm // cos_t.shape[0]
        cos_t, sin_t = jnp.tile(cos_t, (reps, 1)), jnp.tile(sin_t, (reps, 1))
    emat = _head_expand_matrix()
    sel = _rope_select_matrix()
    ssd_c8_in, gdn_c8_in = _tail8(ssd_conv), _tail8(gdn_conv)
    gw = SSD_WIDTH // SSD_GROUPS
    ssd_h_in = ssd_state.reshape(DEPTH, b, SSD_GROUPS, gw, SSD_STATE)
    h = x.reshape(m, D_MODEL)
    new_states = []
    for l in range(DEPTH):
        if l == 0:
            h, p_ssd, p_dt, p_mla, p_gdn, p_ba = _project(h, w["w_in"], l, cfg["tm_proj"], emb_ln=(emb_g, emb_b))
        else:
            p_ssd, p_dt, p_mla, p_gdn, p_ba = _project(h, w["w_in"], l, cfg["tm_proj"])

        y_ssd, ssd_c8, ssd_h = _ssd_scan(
            p_ssd.reshape(b, t, -1), p_dt.reshape(b, t, LANES), ssd_c8_in, ssd_h_in, l,
            w["ssd_cw"], w["ssd_cb"], w["ssd_dtb"], w["ssd_alog"], w["ssd_dexp"], w["ssd_nw"], emat,
            L=cfg["ssd_chunk"], tv=min(tv, cfg["ssd_chunk"]))
        y_gdn, gdn_c8, gdn_s = _gdn_scan(
            p_gdn.reshape(b, t, -1), p_ba.reshape(b, t, LANES), gdn_c8_in, gdn_state, l,
            w["gdn_cw"], w["gdn_bias"], w["gdn_alog"], w["gdn_nw"],
            L=cfg["gdn_chunk"], tv=min(tv, cfg["gdn_chunk"]))

        q, kcat, kt, ckv, kr_t, gate = _mla_prep(p_mla, cos_t, sin_t, l, w["qnw"], w["kvnw"], w["wuq"],
                                                 w["wuk_pairs"], b, t, tm, tq_t=cfg["tq"] if paged is None else None)
        kcat = kcat.reshape(b, t, QK_WIDTH)
        if paged is None:
            o_lat = _flash_attention(q, kcat, kt, tq=cfg["tq"], tk=cfg["tk"])
        else:
            cache_lat, cache_rope_t, page_table = paged
            o_lat = _paged_attention(q, kcat, cache_lat, cache_rope_t, page_table, sel, tpad=t, tv=tv, layer=l)

        h = _out_proj(h, y_ssd.reshape(m, SSD_WIDTH), o_lat, gate, y_gdn.reshape(m, GDN_WIDTH), l,
                      w["wuv_pairs"], w["w_out"], w["ln_g"], w["ln_b"], tm)

        kr = jnp.concatenate([kr_t[:, 0:ROPE_HALF], kr_t[:, LANES:LANES + ROPE_HALF]], axis=1)
        sl = slice(SUBLANES - (CONV_WIDTH - 1), SUBLANES)
        new_states.append((ckv.reshape(b, t, MLA_KV_RANK), kr.reshape(b, t, MLA_ROPE), ssd_c8[:, sl],
                           ssd_h.reshape(b, SSD_HEADS, SSD_HEAD_DIM, SSD_STATE), gdn_c8[:, sl], gdn_s))
    return h.reshape(b, t, D_MODEL), tuple(jnp.stack(s) for s in zip(*new_states))


def kernel(x_prompt, x_sample, cache_kv_latent, cache_k_rope, state_ssd_conv, state_ssd, state_gdn_conv, state_gdn, page_table, emb_ln_g, emb_ln_b, w_in, ssd_conv_w, ssd_conv_b, ssd_dt_bias, ssd_a_log, ssd_d, ssd_norm_w, mla_q_norm_w, mla_w_uq, mla_kv_norm_w, mla_w_uk, mla_w_uv, gdn_conv_w, gdn_dt_bias, gdn_a_log, gdn_norm_w, w_out, ln_g, ln_b):
    weights = _prep_weights(w_in, ssd_conv_w, ssd_conv_b, ssd_dt_bias, ssd_a_log, ssd_d, ssd_norm_w,
                            mla_q_norm_w, mla_w_uq, mla_kv_norm_w, mla_w_uk, mla_w_uv,
                            gdn_conv_w, gdn_dt_bias, gdn_a_log, gdn_norm_w, w_out, ln_g, ln_b)

    bp, tp, _ = x_prompt.shape
    zeros = lambda *s: jnp.zeros(s, F32)
    ssd_chunk = SSD_CHUNK if tp % SSD_CHUNK == 0 else tp
    gdn_chunk = GDN_CHUNK if tp % GDN_CHUNK == 0 else tp
    cfg_p = dict(tm=min(1024, bp * tp), tm_proj=min(512, bp * tp), ssd_chunk=ssd_chunk, gdn_chunk=gdn_chunk,
                 tq=min(256, tp), tk=min(512, tp))
    y_prompt, st_p = _trunk(
        x_prompt, jnp.arange(tp), tp,
        zeros(DEPTH, bp, CONV_WIDTH - 1, SSD_CONV_DIM), zeros(DEPTH, bp, SSD_HEADS, SSD_HEAD_DIM, SSD_STATE),
        zeros(DEPTH, bp, CONV_WIDTH - 1, GDN_CONV_DIM), zeros(DEPTH, bp, GDN_HEADS, GDN_HEAD_DIM, GDN_HEAD_DIM),
        emb_ln_g, emb_ln_b, weights, cfg_p)

    bs, ts, _ = x_sample.shape
    tpad = -(-ts // BF16_ROWS) * BF16_ROWS
    past_len = page_table.shape[1] * PAGE_SIZE
    xs = jnp.pad(x_sample, ((0, 0), (0, tpad - ts), (0, 0)))
    cfg_s = dict(tm=min(512, bs * tpad), tm_proj=min(512, bs * tpad), ssd_chunk=tpad, gdn_chunk=tpad,
                 tq=tpad, tk=tpad)
    y_s, st_s = _trunk(
        xs, past_len + jnp.arange(tpad), ts, state_ssd_conv, state_ssd, state_gdn_conv, state_gdn,
        emb_ln_g, emb_ln_b, weights, cfg_s,
        paged=(cache_kv_latent, jnp.swapaxes(cache_k_rope, 2, 3), page_table))
    y_sample = y_s[:, :ts]
    s_lat, s_rope, s_ssd_conv, s_ssd, s_gdn_conv, s_gdn = st_s
    return (y_prompt, y_sample) + tuple(st_p) + (s_lat[:, :, :ts], s_rope[:, :, :ts], s_ssd_conv, s_ssd, s_gdn_conv, s_gdn)
```
